```python
import math
import jax
import jax.numpy as jnp
from jax import lax
import numpy as np

D_MODEL = 2048
BATCH = 16
SEQ = 256
DEPTH = 4
DEC_BATCH = 4
DEC_SEQ = 1024
PAST_LEN = 256

GRID_W = 64
N_AB_LAYERS = (DEPTH + 1) // 2
N_C_LAYERS = DEPTH // 2

A_HEADS = 4
A_DK = 128
A_DV = 256
A_GATE_RANK = 16
A_TAU = 16.0
A_CHUNK = 32
B_HEADS = 8
B_DH = 64
B_DV = 2 * B_DH
C_HEADS = 16
C_Q_RANK = 512
C_KV_RANK = 256
C_NOPE = 128
C_ROPE = 64
C_DV = 128
N_GROUPS = 4
EXPERTS_PER_GROUP = 4
N_EXPERTS = N_GROUPS * EXPERTS_PER_GROUP
EXPERT_TOP_K = 2
D_EXPERT = 512

ROPE_THETA = 10000.0
Q_BLOCK = 128
LN_EPS = 1e-5
RMS_EPS = 1e-6
DEEPNORM_ALPHA = (2.0 * DEPTH) ** 0.25
DEEPNORM_BETA = (8.0 * DEPTH) ** -0.25

AB_SIZES = (A_HEADS * A_DK, A_HEADS * A_DK, A_HEADS * A_DV, A_HEADS * A_DV, 2 * A_GATE_RANK,
            B_HEADS * 2 * B_DH, B_HEADS * 2 * B_DH, B_HEADS * B_DV)
AB_IN = sum(AB_SIZES)
AB_OUT = A_HEADS * A_DV + B_HEADS * B_DV
C_IN = C_Q_RANK + C_KV_RANK + C_ROPE

kernel_name = 'hybrid_gla_diffattn_mla_hmoe_diffusion_step'


def _split(x, sizes):
    out, start = [], 0
    for s in sizes:
        out.append(x[..., start:start + s])
        start += s
    return out


def layer_norm(x, g, b):
    xf = x.astype(jnp.float32)
    mu = jnp.mean(xf, axis=-1, keepdims=True)
    var = jnp.mean(jnp.square(xf - mu), axis=-1, keepdims=True)
    return ((xf - mu) * lax.rsqrt(var + LN_EPS) * g + b).astype(x.dtype)


def rms_norm(x, g):
    xf = x.astype(jnp.float32)
    return (xf * lax.rsqrt(jnp.mean(xf * xf, axis=-1, keepdims=True) + RMS_EPS) * g).astype(x.dtype)


def rope_tables(n_rows, dim):
    rows = jnp.repeat(jnp.arange(n_rows, dtype=jnp.float32), GRID_W)
    cols = jnp.tile(jnp.arange(GRID_W, dtype=jnp.float32), n_rows)
    quarter = dim // 4
    freqs = ROPE_THETA ** (-jnp.arange(quarter, dtype=jnp.float32) / quarter)
    ang = jnp.concatenate([rows[:, None] * freqs, cols[:, None] * freqs], axis=-1)
    return jnp.cos(ang), jnp.sin(ang)


def apply_rope(x, cos, sin):
    half = x.shape[-1] // 2
    shape = (cos.shape[0],) + (1,) * (x.ndim - 3) + (half,)
    cos, sin = cos.reshape(shape), sin.reshape(shape)
    xf = x.astype(jnp.float32)
    x1, x2 = xf[..., :half], xf[..., half:]
    return jnp.concatenate([x1 * cos - x2 * sin, x1 * sin + x2 * cos], axis=-1).astype(x.dtype)


def gla_chunked(q, k, v, log_a, s0):
    bsz, t, h, dk = q.shape
    dv = v.shape[-1]
    n = t // A_CHUNK
    f32 = jnp.float32
    qc = q.astype(f32).reshape(bsz, n, A_CHUNK, h, dk)
    kc = k.astype(f32).reshape(bsz, n, A_CHUNK, h, dk)
    vc = v.astype(f32).reshape(bsz, n, A_CHUNK, h, dv)
    b = jnp.cumsum(log_a.astype(f32).reshape(bsz, n, A_CHUNK, h, dk), axis=2)
    causal = jnp.tril(jnp.ones((A_CHUNK, A_CHUNK), dtype=bool))[:, :, None, None]
    rel = b[:, :, :, None] - b[:, :, None, :]
    decay = jnp.where(causal, jnp.exp(jnp.where(causal, rel, 0.0)), 0.0)
    scores = jnp.einsum('bnthk,bnshk,bntshk->bnhts', qc, kc, decay)
    o_intra = jnp.einsum('bnhts,bnshv->bnthv', scores, vc)
    b_last = b[:, :, -1]
    k_tail = kc * jnp.exp(b_last[:, :, None] - b)
    delta = jnp.einsum('bnshk,bnshv->bnhkv', k_tail, vc)
    chunk_decay = jnp.exp(b_last)

    def step(s, inp):
        dec, dlt = inp
        return dec[..., None] * s + dlt, s

    s_final, s_prev = lax.scan(step, s0.astype(f32),
                               (jnp.moveaxis(chunk_decay, 1, 0), jnp.moveaxis(delta, 1, 0)))
    s_prev = jnp.moveaxis(s_prev, 0, 1)
    o_inter = jnp.einsum('bnthk,bnhkv->bnthv', qc * jnp.exp(b), s_prev)
    o = (o_intra + o_inter).reshape(bsz, t, h, dv)
    return o.astype(v.dtype), s_final.astype(s0.dtype)


def diff_attention(q, k, v, lam):
    bsz, tq, nh = q.shape[:3]
    nb = tq // Q_BLOCK
    qb = jnp.moveaxis(q.reshape((bsz, nb, Q_BLOCK) + q.shape[2:]), 1, 0)
    scale = B_DH ** -0.5

    def block(qi):
        s = jnp.einsum('bqhjd,bkhjd->bhjqk', qi, k).astype(jnp.float32) * scale
        p = jax.nn.softmax(s, axis=-1)
        wgt = p[:, :, 0] - lam * p[:, :, 1]
        return jnp.einsum('bhqk,bkhv->bqhv', wgt.astype(v.dtype), v)

    o = lax.map(block, qb)
    return jnp.moveaxis(o, 0, 1).reshape(bsz, tq, nh, v.shape[-1])


def softmax_attention(q, k, v, scale):
    bsz, tq, nh = q.shape[:3]
    nb = tq // Q_BLOCK
    qb = jnp.moveaxis(q.reshape((bsz, nb, Q_BLOCK) + q.shape[2:]), 1, 0)

    def block(qi):
        s = jnp.einsum('bqhd,bkhd->bhqk', qi, k).astype(jnp.float32) * scale
        p = jax.nn.softmax(s, axis=-1)
        return jnp.einsum('bhqk,bkhv->bqhv', p.astype(v.dtype), v)

    o = lax.map(block, qb)
    return jnp.moveaxis(o, 0, 1).reshape(bsz, tq, nh, v.shape[-1])


def mixer_ab(h, w_in, w_g2, b_g2, gla_g, lam_vec, diff_g, w_out, lam_init, rope, ctx):
    bsz, t, _ = h.shape
    gq, gk, gv, gr, gg, dq, dk, dv = _split(h @ w_in, AB_SIZES)
    q = gq.reshape(bsz, t, A_HEADS, A_DK) * (A_DK ** -0.5)
    k = gk.reshape(bsz, t, A_HEADS, A_DK)
    v = gv.reshape(bsz, t, A_HEADS, A_DV)
    gate_logit = jnp.einsum('btjr,jrk->btjk', gg.reshape(bsz, t, 2, A_GATE_RANK), w_g2) + b_g2
    log_a = (jax.nn.log_sigmoid(gate_logit.astype(jnp.float32)) / A_TAU).reshape(bsz, t, 2, A_HEADS, A_DK)
    if ctx is None:
        s0 = jnp.zeros((bsz, 2, A_HEADS, A_DK, A_DV), h.dtype)
    else:
        s0 = ctx[0]
    o_f, s_f = gla_chunked(q, k, v, log_a[:, :, 0], s0[:, 0])
    o_b, s_b = gla_chunked(q[:, ::-1], k[:, ::-1], v[:, ::-1], log_a[:, ::-1, 1], s0[:, 1])
    o_gla = rms_norm(o_f + o_b[:, ::-1], gla_g).reshape(bsz, t, A_HEADS * A_DV) * jax.nn.silu(gr)
    q2 = dq.reshape(bsz, t, B_HEADS, 2, B_DH)
    k2 = dk.reshape(bsz, t, B_HEADS, 2, B_DH)
    v2 = dv.reshape(bsz, t, B_HEADS, B_DV)
    if ctx is None:
        keys, vals = k2, v2
    else:
        q2 = apply_rope(q2, rope[0], rope[1])
        keys = jnp.concatenate([ctx[1], apply_rope(k2, rope[0], rope[1])], axis=1)
        vals = jnp.concatenate([ctx[2], v2], axis=1)
    lv = lam_vec.astype(jnp.float32)
    lam = jnp.exp(jnp.sum(lv[0] * lv[1])) - jnp.exp(jnp.sum(lv[2] * lv[3])) + lam_init
    o_diff = rms_norm(diff_attention(q2, keys, vals, lam), diff_g) * (1.0 - lam_init)
    out = jnp.concatenate([o_gla, o_diff.reshape(bsz, t, B_HEADS * B_DV)], axis=-1) @ w_out
    return out, (jnp.stack([s_f, s_b], axis=1), k2, v2)


def mixer_mla(h, w_in, q_g, w_uq, kv_g, w_ukv, w_out, rope, ctx):
    bsz, t, _ = h.shape
    cq, ckv, krope = _split(h @ w_in, (C_Q_RANK, C_KV_RANK, C_ROPE))
    q = (rms_norm(cq, q_g) @ w_uq).reshape(bsz, t, C_HEADS, C_NOPE + C_ROPE)
    ckv = rms_norm(ckv, kv_g)
    if ctx is None:
        ckv_all, krope_all = ckv, krope
    else:
        q = jnp.concatenate([q[..., :C_NOPE], apply_rope(q[..., C_NOPE:], rope[0], rope[1])], axis=-1)
        krope_lat = apply_rope(krope[:, :, None, :], rope[0], rope[1])[:, :, 0]
        ckv_all = jnp.concatenate([ctx[0], ckv], axis=1)
        krope_all = jnp.concatenate([ctx[1], krope_lat], axis=1)
    tk = ckv_all.shape[1]
    kv = (ckv_all @ w_ukv).reshape(bsz, tk, C_HEADS, C_NOPE + C_DV)
    k = jnp.concatenate([kv[..., :C_NOPE],
                         jnp.broadcast_to(krope_all[:, :, None, :], (bsz, tk, C_HEADS, C_ROPE))], axis=-1)
    o = softmax_attention(q, k, kv[..., C_NOPE:], (C_NOPE + C_ROPE) ** -0.5)
    return o.reshape(bsz, t, C_HEADS * C_DV) @ w_out, (ckv, krope)


def hier_moe(h, w_rg, b_rg, w_re, b_re, w_gate, w_up, w_down):
    bsz, t, d = h.shape
    xt = h.reshape(bsz * t, d)
    g_logits = (xt @ w_rg + b_rg).astype(jnp.float32)
    g_prob = jax.nn.softmax(g_logits, axis=-1)
    g_sel = jnp.argmax(g_logits, axis=-1)
    g_w = jnp.take_along_axis(g_prob, g_sel[:, None], axis=-1)
    e_logits = (xt @ w_re + b_re).astype(jnp.float32).reshape(-1, N_GROUPS, EXPERTS_PER_GROUP)
    e_in_group = jnp.take_along_axis(e_logits, g_sel[:, None, None], axis=1)[:, 0]
    top_p, top_i = lax.top_k(jax.nn.softmax(e_in_group, axis=-1), EXPERT_TOP_K)
    top_w = g_w * top_p / jnp.sum(top_p, axis=-1, keepdims=True)
    expert_id = g_sel[:, None] * EXPERTS_PER_GROUP + top_i
    combine = jnp.einsum('tk,tke->te', top_w, jax.nn.one_hot(expert_id, N_EXPERTS, dtype=jnp.float32))
    gate = jnp.einsum('td,edf->tef', xt, w_gate)
    up = jnp.einsum('td,edf->tef', xt, w_up)
    act = jax.nn.silu(gate) * up * combine[:, :, None].astype(h.dtype)
    return jnp.einsum('tef,efd->td', act, w_down).reshape(bsz, t, d)


def run_trunk(x, cond, rope, caches, w):
    gla_states, diff_ks, diff_vs, mla_ckvs, mla_kropes = [], [], [], [], []
    for layer in range(DEPTH):
        mod = jax.nn.silu(cond) @ w['ada_w'][layer] + w['ada_b'][layer]
        shift1, scale1, gate1, shift2, scale2, gate2 = jnp.split(mod[:, None, :], 6, axis=-1)
        h = x * (1.0 + scale1) + shift1
        i = layer // 2
        if layer % 2 == 0:
            ctx = None if caches is None else (caches[0][:, i], caches[1][:, i], caches[2][:, i])
            y, (st, kk, vv) = mixer_ab(h, w['ab_w_in'][i], w['gla_w_gate2'][i], w['gla_b_gate2'][i],
                                       w['gla_norm_g'][i], w['diff_lambda'][i], w['diff_norm_g'][i],
                                       w['ab_w_out'][i], 0.8 - 0.6 * math.exp(-0.3 * layer),
                                       None if rope is None else rope[0], ctx)
            gla_states.append(st)
            diff_ks.append(kk)
            diff_vs.append(vv)
        else:
            ctx = None if caches is None else (caches[3][:, i], caches[4][:, i])
            y, (ckv, kr) = mixer_mla(h, w['mla_w_in'][i], w['mla_q_norm_g'][i], w['mla_w_uq'][i],
                                     w['mla_kv_norm_g'][i], w['mla_w_ukv'][i], w['mla_w_out'][i],
                                     None if rope is None else rope[1], ctx)
            mla_ckvs.append(ckv)
            mla_kropes.append(kr)
        x = layer_norm(DEEPNORM_ALPHA * x + gate1 * y, w['ln_g'][layer, 0], w['ln_b'][layer, 0])
        h = x * (1.0 + scale2) + shift2
        y = hier_moe(h, w['moe_w_rg'][layer], w['moe_b_rg'][layer], w['moe_w_re'][layer],
                     w['moe_b_re'][layer], w['moe_w_gate'][layer], w['moe_w_up'][layer], w['moe_w_down'][layer])
        x = layer_norm(DEEPNORM_ALPHA * x + gate2 * y, w['ln_g'][layer, 1], w['ln_b'][layer, 1])
    states = (jnp.stack(gla_states, axis=1), jnp.stack(diff_ks, axis=1), jnp.stack(diff_vs, axis=1),
              jnp.stack(mla_ckvs, axis=1), jnp.stack(mla_kropes, axis=1))
    return x, states


def setup_inputs(seed: int = 0) -> dict:
    key = jax.random.key(seed)
    keys = iter(jax.random.split(key, 64))
    d = D_MODEL

    def nrm(shape, scale):
        return jax.random.normal(next(keys), shape, jnp.float32) * scale

    return {
        'x_prompt': nrm((BATCH, SEQ, d), 1.0),
        'x_sample': nrm((DEC_BATCH, DEC_SEQ, d), 1.0),
        'state_gla': nrm((DEC_BATCH, N_AB_LAYERS, 2, A_HEADS, A_DK, A_DV), 1.0),
        'cache_diff_k': nrm((DEC_BATCH, N_AB_LAYERS, PAST_LEN, B_HEADS, 2, B_DH), 1.0),
        'cache_diff_v': nrm((DEC_BATCH, N_AB_LAYERS, PAST_LEN, B_HEADS, B_DV), 1.0),
        'cache_mla_ckv': nrm((DEC_BATCH, N_C_LAYERS, PAST_LEN, C_KV_RANK), 1.0),
        'cache_mla_krope': nrm((DEC_BATCH, N_C_LAYERS, PAST_LEN, C_ROPE), 1.0),
        'c': nrm((DEC_BATCH, d), 1.0),
        'c_ctx': nrm((d,), 1.0),
        'ada_w': nrm((DEPTH, d, 6 * d), 0.5 * d ** -0.5),
        'ada_b': nrm((DEPTH, 6 * d), 0.02),
        'ln_g': 1.0 + nrm((DEPTH, 2, d), 0.02),
        'ln_b': nrm((DEPTH, 2, d), 0.02),
        'ab_w_in': nrm((N_AB_LAYERS, d, AB_IN), d ** -0.5),
        'gla_w_gate2': nrm((N_AB_LAYERS, 2, A_GATE_RANK, A_HEADS * A_DK), A_GATE_RANK ** -0.5),
        'gla_b_gate2': nrm((N_AB_LAYERS, 2, A_HEADS * A_DK), 0.5),
        'gla_norm_g': 1.0 + nrm((N_AB_LAYERS, A_DV), 0.02),
        'diff_lambda': nrm((N_AB_LAYERS, 4, B_DH), 0.1),
        'diff_norm_g': 1.0 + nrm((N_AB_LAYERS, B_DV), 0.02),
        'ab_w_out': nrm((N_AB_LAYERS, AB_OUT, d), DEEPNORM_BETA * AB_OUT ** -0.5),
        'mla_w_in': nrm((N_C_LAYERS, d, C_IN), d ** -0.5),
        'mla_q_norm_g': 1.0 + nrm((N_C_LAYERS, C_Q_RANK), 0.02),
        'mla_w_uq': nrm((N_C_LAYERS, C_Q_RANK, C_HEADS * (C_NOPE + C_ROPE)), C_Q_RANK ** -0.5),
        'mla_kv_norm_g': 1.0 + nrm((N_C_LAYERS, C_KV_RANK), 0.02),
        'mla_w_ukv': nrm((N_C_LAYERS, C_KV_RANK, C_HEADS * (C_NOPE + C_DV)), C_KV_RANK ** -0.5),
        'mla_w_out': nrm((N_C_LAYERS, C_HEADS * C_DV, d), DEEPNORM_BETA * (C_HEADS * C_DV) ** -0.5),
        'moe_w_rg': nrm((DEPTH, d, N_GROUPS), d ** -0.5),
        'moe_b_rg': nrm((DEPTH, N_GROUPS), 0.01),
        'moe_w_re': nrm((DEPTH, d, N_EXPERTS), d ** -0.5),
        'moe_b_re': nrm((DEPTH, N_EXPERTS), 0.01),
        'moe_w_gate': nrm((DEPTH, N_EXPERTS, d, D_EXPERT), d ** -0.5),
        'moe_w_up': nrm((DEPTH, N_EXPERTS, d, D_EXPERT), d ** -0.5),
        'moe_w_down': nrm((DEPTH, N_EXPERTS, D_EXPERT, d), DEEPNORM_BETA * D_EXPERT ** -0.5),
    }


def reference(x_prompt, x_sample, state_gla, cache_diff_k, cache_diff_v, cache_mla_ckv, cache_mla_krope,
              c, c_ctx, ada_w, ada_b, ln_g, ln_b, ab_w_in, gla_w_gate2, gla_b_gate2, gla_norm_g,
              diff_lambda, diff_norm_g, ab_w_out, mla_w_in, mla_q_norm_g, mla_w_uq, mla_kv_norm_g,
              mla_w_ukv, mla_w_out, moe_w_rg, moe_b_rg, moe_w_re, moe_b_re, moe_w_gate, moe_w_up,
              moe_w_down):
    w = {
        'ada_w': ada_w, 'ada_b': ada_b, 'ln_g': ln_g, 'ln_b': ln_b,
        'ab_w_in': ab_w_in, 'gla_w_gate2': gla_w_gate2, 'gla_b_gate2': gla_b_gate2,
        'gla_norm_g': gla_norm_g, 'diff_lambda': diff_lambda, 'diff_norm_g': diff_norm_g,
        'ab_w_out': ab_w_out, 'mla_w_in': mla_w_in, 'mla_q_norm_g': mla_q_norm_g, 'mla_w_uq': mla_w_uq,
        'mla_kv_norm_g': mla_kv_norm_g, 'mla_w_ukv': mla_w_ukv, 'mla_w_out': mla_w_out,
        'moe_w_rg': moe_w_rg, 'moe_b_rg': moe_b_rg, 'moe_w_re': moe_w_re, 'moe_b_re': moe_b_re,
        'moe_w_gate': moe_w_gate, 'moe_w_up': moe_w_up, 'moe_w_down': moe_w_down,
    }
    y_prompt, (new_gla, new_diff_k, new_diff_v, new_mla_ckv, new_mla_krope) = run_trunk(
        x_prompt, c_ctx[None, :], None, None, w)
    n_rows = x_sample.shape[1] // GRID_W
    rope = (rope_tables(n_rows, B_DH), rope_tables(n_rows, C_ROPE))
    caches = (state_gla, cache_diff_k, cache_diff_v, cache_mla_ckv, cache_mla_krope)
    y_sample, _ = run_trunk(x_sample, c, rope, caches, w)
    return (y_prompt, y_sample, new_gla, new_diff_k, new_diff_v, new_mla_ckv, new_mla_krope)
```

```python
import functools
import math

import numpy as np
import jax
import jax.numpy as jnp
from jax import lax
from jax.experimental import pallas as pl
from jax.experimental.pallas import tpu as pltpu

F32 = jnp.float32
BF16 = jnp.bfloat16

D_MODEL = 2048
BATCH = 16
SEQ = 256
DEPTH = 4
DEC_BATCH = 4
DEC_SEQ = 1024
PAST_LEN = 256
GRID_W = 64
A_HEADS = 4
A_DK = 128
A_DV = 256
A_GATE_RANK = 16
A_TAU = 16.0
B_HEADS = 8
B_DH = 64
B_DV = 2 * B_DH
C_HEADS = 16
C_Q_RANK = 512
C_KV_RANK = 256
C_NOPE = 128
C_ROPE = 64
C_DV = 128
N_GROUPS = 4
EXPERTS_PER_GROUP = 4
N_EXPERTS = N_GROUPS * EXPERTS_PER_GROUP
EXPERT_TOP_K = 2
D_EXPERT = 512
ROPE_THETA = 10000.0
LN_EPS = 1e-5
RMS_EPS = 1e-6
DEEPNORM_ALPHA = (2.0 * DEPTH) ** 0.25

NCTX = BATCH * SEQ
NLAT = DEC_BATCH * DEC_SEQ
NTOK = NCTX + NLAT
N_COND = 8
GLA_CHUNK = 256
GLA_LEVELS = 8
MOE_TM = 256
MOE_ROWS = NTOK * EXPERT_TOP_K + N_EXPERTS * MOE_TM
MOE_TILES = MOE_ROWS // MOE_TM

PA_Q, PA_K, PA_V, PA_R = 0, A_HEADS * A_DK, 2 * A_HEADS * A_DK, 2 * A_HEADS * A_DK + A_HEADS * A_DV
PA_N = PA_R + A_HEADS * A_DV
AB_GG0 = PA_N
AB_D0 = PA_N + 2 * A_GATE_RANK
PD_Q, PD_K, PD_V = 0, B_HEADS * 2 * B_DH, 2 * B_HEADS * 2 * B_DH
PD_GG = PD_V + B_HEADS * B_DV
PD_N = PD_GG + 128


def _params(sem, vmem_mb):
    return pltpu.CompilerParams(dimension_semantics=sem, vmem_limit_bytes=vmem_mb * 2 ** 20)


def _cond_row(i, tm):
    return jnp.where(i * tm < NCTX, 0, 1 + (i * tm - NCTX) // DEC_SEQ)


def _dot(a, b):
    return jnp.dot(a, b, preferred_element_type=F32)


def _dot_nt(a, b):
    return lax.dot_general(a, b, (((1,), (1,)), ((), ())), preferred_element_type=F32)


def _dot_tn(a, b):
    return lax.dot_general(a, b, (((0,), (0,)), ((), ())), preferred_element_type=F32)


def _ada_body(c_ref, w_ref, b_ref, o_ref):
    c = c_ref[...]
    a = (c / (1.0 + jnp.exp(-c))).astype(BF16)
    o_ref[0] = _dot(a, w_ref[0].astype(BF16)) + b_ref[0]


def _ada_mod(cond, ada_w, ada_b):
    tn = 1024
    return pl.pallas_call(
        _ada_body,
        grid=(DEPTH, 6 * D_MODEL // tn),
        in_specs=[pl.BlockSpec((N_COND, D_MODEL), lambda l, j: (0, 0)),
                  pl.BlockSpec((1, D_MODEL, tn), lambda l, j: (l, 0, j)),
                  pl.BlockSpec((1, 1, tn), lambda l, j: (l, 0, j))],
        out_specs=pl.BlockSpec((1, N_COND, tn), lambda l, j: (l, 0, j)),
        out_shape=jax.ShapeDtypeStruct((DEPTH, N_COND, 6 * D_MODEL), F32),
        compiler_params=_params(("arbitrary", "arbitrary"), 40),
        name="ada_mod",
    )(cond, ada_w, ada_b.reshape(DEPTH, 1, 6 * D_MODEL))


def _mod_spec(layer, chunk, tm):
    return pl.BlockSpec((1, 1, 1, D_MODEL), lambda i, *_: (layer, _cond_row(i, tm), 0, chunk))


def _mod_mm_body(x_ref, sc_ref, sh_ref, w_ref, o_ref, h_ref):
    @pl.when(pl.program_id(1) == 0)
    def _():
        h_ref[...] = (x_ref[...] * (1.0 + sc_ref[0, 0]) + sh_ref[0, 0]).astype(BF16)

    o_ref[...] = _dot(h_ref[...], w_ref[...].astype(BF16)).astype(o_ref.dtype)


def _mod_matmul(x, mod4, layer, w, w_layer, n_out, tn, name, tm=1024, out_dtype=F32):
    return pl.pallas_call(
        _mod_mm_body,
        grid=(NTOK // tm, n_out // tn),
        in_specs=[pl.BlockSpec((tm, D_MODEL), lambda i, j: (i, 0)),
                  _mod_spec(layer, 1, tm), _mod_spec(layer, 0, tm),
                  pl.BlockSpec((None, D_MODEL, tn), lambda i, j: (w_layer, 0, j))],
        out_specs=pl.BlockSpec((tm, tn), lambda i, j: (i, j)),
        out_shape=jax.ShapeDtypeStruct((NTOK, n_out), out_dtype),
        scratch_shapes=[pltpu.VMEM((tm, D_MODEL), BF16)],
        compiler_params=_params(("arbitrary", "arbitrary"), 48),
        name=name,
    )(x, mod4, mod4, w)


def _rms_mm_body(x_ref, g_ref, w_ref, *rest, norm, emit_norm):
    if emit_norm:
        o_ref, n_ref, h_ref = rest
    else:
        o_ref, h_ref = rest

    @pl.when(pl.program_id(1) == 0)
    def _():
        x = x_ref[...]
        if norm:
            x = x * lax.rsqrt(jnp.mean(x * x, axis=-1, keepdims=True) + RMS_EPS) * g_ref[...]
        h_ref[...] = x.astype(BF16)
        if emit_norm:
            n_ref[...] = x

    o_ref[...] = _dot(h_ref[...], w_ref[...].astype(BF16)).astype(o_ref.dtype)


def _rms_matmul(x, col_block, k, g, w, col0_blocks, n_out, tn, tm, out_dtype, name, norm=True, emit_norm=False):
    rows = x.shape[0]
    out_shape = [jax.ShapeDtypeStruct((rows, n_out), out_dtype)]
    out_specs = [pl.BlockSpec((tm, tn), lambda i, j: (i, j))]
    if emit_norm:
        out_shape.append(jax.ShapeDtypeStruct((rows, k), F32))
        out_specs.append(pl.BlockSpec((tm, k), lambda i, j: (i, 0)))
    res = pl.pallas_call(
        functools.partial(_rms_mm_body, norm=norm, emit_norm=emit_norm),
        grid=(rows // tm, n_out // tn),
        in_specs=[pl.BlockSpec((tm, k), lambda i, j: (i, col_block)),
                  pl.BlockSpec((1, k), lambda i, j: (0, 0)),
                  pl.BlockSpec((k, tn), lambda i, j: (0, col0_blocks + j))],
        out_specs=out_specs,
        out_shape=out_shape,
        scratch_shapes=[pltpu.VMEM((tm, k), BF16)],
        compiler_params=_params(("arbitrary", "arbitrary"), 40),
        name=name,
    )(x, g, w)
    return res if emit_norm else res[0]


def _layer_norm(z, g, b):
    mu = jnp.mean(z, axis=-1, keepdims=True)
    zc = z - mu
    var = jnp.mean(zc * zc, axis=-1, keepdims=True)
    return zc * lax.rsqrt(var + LN_EPS) * g + b


def _proj_ln_body(*refs, n_lhs, n_j):
    lhs = refs[:n_lhs]
    ws = refs[n_lhs:2 * n_lhs]
    x_ref, gate_ref, g_ref, b_ref, o_ref, acc_ref = refs[2 * n_lhs:]
    j = pl.program_id(1)
    y = _dot(lhs[0][...], ws[0][...].astype(BF16))
    for a, w in zip(lhs[1:], ws[1:]):
        y = y + _dot(a[...], w[...].astype(BF16))
    acc_ref[j] = y

    @pl.when(j == n_j - 1)
    def _():
        y_all = jnp.concatenate([acc_ref[jj] for jj in range(n_j)], axis=1)
        z = DEEPNORM_ALPHA * x_ref[...] + gate_ref[0, 0] * y_all
        o_ref[...] = _layer_norm(z, g_ref[...], b_ref[...])


def _proj_ln(lhs_list, w, w_layer, x, mod4, layer, gate_chunk, ln_g, ln_b, name, tm=512, tn=512):
    n_lhs = len(lhs_list)
    kk = lhs_list[0].shape[1]
    n_j = D_MODEL // tn
    in_specs = [pl.BlockSpec((tm, kk), lambda i, j: (i, 0)) for _ in lhs_list]
    in_specs += [pl.BlockSpec((None, kk, tn), functools.partial(lambda i, j, r: (w_layer, r, j), r=r))
                 for r in range(n_lhs)]
    in_specs += [pl.BlockSpec((tm, D_MODEL), lambda i, j: (i, 0)),
                 _mod_spec(layer, gate_chunk, tm),
                 pl.BlockSpec((1, D_MODEL), lambda i, j: (0, 0)),
                 pl.BlockSpec((1, D_MODEL), lambda i, j: (0, 0))]
    return pl.pallas_call(
        functools.partial(_proj_ln_body, n_lhs=n_lhs, n_j=n_j),
        grid=(NTOK // tm, n_j),
        in_specs=in_specs,
        out_specs=pl.BlockSpec((tm, D_MODEL), lambda i, j: (i, 0)),
        out_shape=jax.ShapeDtypeStruct((NTOK, D_MODEL), F32),
        scratch_shapes=[pltpu.VMEM((n_j, tm, tn), F32)],
        compiler_params=_params(("arbitrary", "arbitrary"), 48),
        name=name,
    )(*lhs_list, *([w] * n_lhs), x, mod4, ln_g, ln_b)


def _res_ln_body(x_ref, y_ref, gate_ref, g_ref, b_ref, o_ref):
    z = DEEPNORM_ALPHA * x_ref[...] + gate_ref[0, 0] * y_ref[...]
    o_ref[...] = _layer_norm(z, g_ref[...], b_ref[...])


def _res_ln(x, y, mod4, layer, gate_chunk, ln_g, ln_b, tm=512):
    row = pl.BlockSpec((tm, D_MODEL), lambda i: (i, 0))
    vec = pl.BlockSpec((1, D_MODEL), lambda i: (0, 0))
    return pl.pallas_call(
        _res_ln_body,
        grid=(NTOK // tm,),
        in_specs=[row, row, _mod_spec(layer, gate_chunk, tm), vec, vec],
        out_specs=row,
        out_shape=jax.ShapeDtypeStruct((NTOK, D_MODEL), F32),
        compiler_params=_params(("arbitrary",), 40),
        name="moe_res_ln",
    )(x, y, mod4, ln_g, ln_b)


def _gla_level_table(reverse):
    t = np.arange(GLA_CHUNK)[:, None]
    s = np.arange(GLA_CHUNK)[None, :]
    x = t ^ s
    lev = np.where(x > 0, np.floor(np.log2(np.maximum(x, 1))).astype(np.int32), -1)
    live = (t < s) if reverse else (t > s)
    return np.where(live | (t == s), lev, -2).astype(np.int32)


def _anchor_rows(u, level, reverse):
    c = u.shape[0]
    n = c // 8
    u3 = u.reshape(n, 8, 128)
    half = 1 << level
    if level < 3:
        blk = 2 * half
        sub = lax.broadcasted_iota(jnp.int32, (n, 8, 128), 1)
        out = None
        for j in range(8 // blk):
            r = j * blk + (half if reverse else half - 1)
            piece = jnp.broadcast_to(u3[:, r:r + 1, :], (n, 8, 128))
            out = piece if out is None else jnp.where(sub >= j * blk, piece, out)
        return out.reshape(c, 128)
    m = 1 << (level - 2)
    r = 0 if reverse else 7
    e = jnp.broadcast_to(u3[:, r:r + 1, :], (n, 8, 128)).reshape(n // m, m, 8, 128)
    idx = m // 2 if reverse else m // 2 - 1
    return jnp.broadcast_to(e[:, idx:idx + 1], (n // m, m, 8, 128)).reshape(c, 128)


def _gla_intra(q, k, la, lev, reverse):
    rowk = lax.broadcasted_iota(jnp.int32, (GLA_CHUNK, 128), 0)
    scores = jnp.where(lev == -1, _dot_nt(q.astype(BF16), k.astype(BF16)), 0.0)
    u = la
    for level in range(GLA_LEVELS):
        anchor = _anchor_rows(u, level, reverse)
        qa = (q * jnp.exp(u)).astype(BF16)
        ka = (k * jnp.exp(jnp.minimum(anchor - u, 0.0))).astype(BF16)
        scores = jnp.where(lev == level, _dot_nt(qa, ka), scores)
        query_side = ((rowk >> level) & 1) == (0 if reverse else 1)
        u = u + jnp.where(query_side, anchor, 0.0)
    return scores, u


def _gla_body(*refs, t_len, has_s0, emit_state):
    it = iter(refs)
    q_ref, k_ref, v_ref, r_ref, gg_ref, wg_ref, bg_ref, g_ref, levf_ref, levb_ref = [next(it) for _ in range(10)]
    s0_ref = next(it) if has_s0 else None
    o_ref = next(it)
    st_ref = next(it) if emit_state else None
    acc_ref, sf_ref, sb_ref, la_ref = [next(it) for _ in range(4)]
    c = GLA_CHUNK
    n_ch = t_len // c

    gl = _dot(gg_ref[...].astype(BF16), wg_ref[...].astype(BF16)) + bg_ref[...]
    la_ref[...] = (jnp.minimum(gl, 0.0) - jnp.log1p(jnp.exp(-jnp.abs(gl)))) * (1.0 / A_TAU)
    if has_s0:
        sf_ref[...] = s0_ref[0]
        sb_ref[...] = s0_ref[1]

    eye = (lax.broadcasted_iota(jnp.int32, (A_DK, A_DK), 0) == lax.broadcasted_iota(jnp.int32, (A_DK, A_DK), 1))

    def chunk_step(ci, reverse, with_state):
        rows = pl.ds(ci * c if isinstance(ci, int) else pl.multiple_of(ci * c, c), c)
        q = q_ref[rows, :] * (A_DK ** -0.5)
        k = k_ref[rows, :]
        v = v_ref[rows, :].astype(BF16)
        la = la_ref[rows, A_DK:2 * A_DK] if reverse else la_ref[rows, 0:A_DK]
        s_ref = sb_ref if reverse else sf_ref
        scores, cum = _gla_intra(q, k, la, (levb_ref if reverse else levf_ref)[...], reverse)
        o = _dot(scores.astype(BF16), v)
        if with_state:
            o = o + _dot((q * jnp.exp(cum)).astype(BF16), s_ref[...].astype(BF16))
        tot = jnp.broadcast_to(cum[0:1, :] if reverse else cum[c - 1:c, :], (c, A_DK))
        delta = _dot_tn((k * jnp.exp(tot - cum)).astype(BF16), v)
        if with_state:
            decay = jnp.sum(jnp.where(eye, jnp.exp(tot[0:A_DK, :]), 0.0), axis=1, keepdims=True)
            s_ref[...] = decay * s_ref[...] + delta
        else:
            s_ref[...] = delta
        if not reverse:
            acc_ref[rows, :] = o
        else:
            y = acc_ref[rows, :] + o
            y = y * lax.rsqrt(jnp.mean(y * y, axis=-1, keepdims=True) + RMS_EPS) * g_ref[...]
            r = r_ref[rows, :]
            o_ref[rows, :] = (y * (r / (1.0 + jnp.exp(-r)))).astype(o_ref.dtype)

    for reverse in (False, True):
        first = (n_ch - 1) if reverse else 0
        chunk_step(first, reverse, has_s0)
        if n_ch > 1:
            def body(i, carry, reverse=reverse):
                chunk_step((n_ch - 1 - i) if reverse else i, reverse, True)
                return carry
            lax.fori_loop(1, n_ch, body, 0)

    if emit_state:
        st_ref[0] = sf_ref[...]
        st_ref[1] = sb_ref[...]


def _gla(pa, pd, wg, bg, gla_g, levf, levb, row0, n_seq, t_len, s0, s0_layer, emit_state):
    rb = row0 // t_len
    kb, vb, rbk = PA_K // A_DK, PA_V // A_DV, PA_R // A_DV
    in_specs = [
        pl.BlockSpec((t_len, A_DK), lambda n, h: (rb + n, h)),
        pl.BlockSpec((t_len, A_DK), lambda n, h: (rb + n, kb + h)),
        pl.BlockSpec((t_len, A_DV), lambda n, h: (rb + n, vb + h)),
        pl.BlockSpec((t_len, A_DV), lambda n, h: (rb + n, rbk + h)),
        pl.BlockSpec((t_len, 128), lambda n, h: (rb + n, PD_GG // 128)),
        pl.BlockSpec((None, 128, 2 * A_DK), lambda n, h: (h, 0, 0)),
        pl.BlockSpec((None, 1, 2 * A_DK), lambda n, h: (h, 0, 0)),
        pl.BlockSpec((1, A_DV), lambda n, h: (0, 0)),
        pl.BlockSpec((GLA_CHUNK, GLA_CHUNK), lambda n, h: (0, 0)),
        pl.BlockSpec((GLA_CHUNK, GLA_CHUNK), lambda n, h: (0, 0)),
    ]
    args = [pa, pa, pa, pa, pd, wg, bg, gla_g, levf, levb]
    if s0 is not None:
        in_specs.append(pl.BlockSpec((None, None, 2, None, A_DK, A_DV), lambda n, h: (n, s0_layer, 0, h, 0, 0)))
        args.append(s0)
    out_shape = [jax.ShapeDtypeStruct((n_seq * t_len, A_HEADS * A_DV), BF16)]
    out_specs = [pl.BlockSpec((t_len, A_DV), lambda n, h: (n, h))]
    if emit_state:
        out_shape.append(jax.ShapeDtypeStruct((n_seq, 2, A_HEADS, A_DK, A_DV), F32))
        out_specs.append(pl.BlockSpec((None, 2, None, A_DK, A_DV), lambda n, h: (n, 0, h, 0, 0)))
    res = pl.pallas_call(
        functools.partial(_gla_body, t_len=t_len, has_s0=s0 is not None, emit_state=emit_state),
        grid=(n_seq, A_HEADS),
        in_specs=in_specs,
        out_specs=out_specs,
        out_shape=out_shape,
        scratch_shapes=[pltpu.VMEM((t_len, A_DV), F32), pltpu.VMEM((A_DK, A_DV), F32),
                        pltpu.VMEM((A_DK, A_DV), F32), pltpu.VMEM((t_len, 2 * A_DK), F32)],
        compiler_params=_params(("arbitrary", "arbitrary"), 40),
        name="gla_ctx" if s0 is None else "gla_lat",
    )(*args)
    return res


def _rope(x, cos, sin):
    lane = lax.broadcasted_iota(jnp.int32, x.shape, 1)
    partner = jnp.where((lane & 32) == 0, pltpu.roll(x, 96, 1), pltpu.roll(x, 32, 1))
    return x * cos + partner * sin


def _rope_tables():
    n_rows = DEC_SEQ // GRID_W
    rows = jnp.repeat(jnp.arange(n_rows, dtype=F32), GRID_W)
    cols = jnp.tile(jnp.arange(GRID_W, dtype=F32), n_rows)
    quarter = C_ROPE // 4
    freqs = ROPE_THETA ** (-jnp.arange(quarter, dtype=F32) / quarter)
    ang = jnp.concatenate([rows[:, None] * freqs, cols[:, None] * freqs], axis=-1)
    cos, sin = jnp.cos(ang), jnp.sin(ang)
    cos_t = jnp.tile(cos, (1, 4))
    sin_t = jnp.tile(jnp.concatenate([-sin, sin], axis=-1), (1, 2))
    return cos_t, sin_t


def _softmax_segments(scores):
    m = functools.reduce(jnp.maximum, [jnp.max(s, axis=-1, keepdims=True) for s in scores])
    ps = [jnp.exp(s - m) for s in scores]
    inv = 1.0 / functools.reduce(jnp.add, [jnp.sum(p, axis=-1, keepdims=True) for p in ps])
    return [p * inv for p in ps]


def _diff_body(*refs, latent, lam_init):
    it = iter(refs)
    lam_ref, q_ref, k_ref, v_ref, g_ref = [next(it) for _ in range(5)]
    if latent:
        kc_ref, vc_ref, cq_ref, sq_ref, ck_ref, sk_ref = [next(it) for _ in range(6)]
    o_ref = next(it)
    lam = lam_ref[0, 0]
    q = q_ref[...]
    k = k_ref[...]
    segs = []
    if latent:
        q = _rope(q, cq_ref[...], sq_ref[...])
        k = _rope(k, ck_ref[...], sk_ref[...])
        segs.append((kc_ref[...].astype(BF16), vc_ref[...].astype(BF16)))
    segs.append((k.astype(BF16), v_ref[...].astype(BF16)))
    lane = lax.broadcasted_iota(jnp.int32, q.shape, 1)
    q1 = jnp.where(lane < B_DH, q, 0.0).astype(BF16)
    q2 = jnp.where(lane >= B_DH, q, 0.0).astype(BF16)
    scale = B_DH ** -0.5
    p1 = _softmax_segments([_dot_nt(q1, kk) * scale for kk, _ in segs])
    p2 = _softmax_segments([_dot_nt(q2, kk) * scale for kk, _ in segs])
    o = None
    for a, b, (_, vv) in zip(p1, p2, segs):
        part = _dot((a - lam * b).astype(BF16), vv)
        o = part if o is None else o + part
    o = o * lax.rsqrt(jnp.mean(o * o, axis=-1, keepdims=True) + RMS_EPS) * g_ref[...] * (1.0 - lam_init)
    o_ref[...] = o.astype(o_ref.dtype)


def _diff_attn(pd, lam, diff_g, lam_init, row0, n_seq, t_len, cache=None, rope=None, bq=256):
    latent = cache is not None
    rb = row0 // t_len
    qb0 = row0 // bq
    nqb = t_len // bq
    in_specs = [
        pl.BlockSpec(memory_space=pltpu.SMEM),
        pl.BlockSpec((bq, 128), lambda n, h, b: (qb0 + n * nqb + b, PD_Q // 128 + h)),
        pl.BlockSpec((t_len, 128), lambda n, h, b: (rb + n, PD_K // 128 + h)),
        pl.BlockSpec((t_len, 128), lambda n, h, b: (rb + n, PD_V // 128 + h)),
        pl.BlockSpec((1, B_DV), lambda n, h, b: (0, 0)),
    ]
    args = [lam, pd, pd, pd, diff_g]
    if latent:
        kc, vc, layer = cache
        cos_t, sin_t = rope
        in_specs += [
            pl.BlockSpec((None, None, PAST_LEN, 128), lambda n, h, b: (n, layer, 0, h)),
            pl.BlockSpec((None, None, PAST_LEN, 128), lambda n, h, b: (n, layer, 0, h)),
            pl.BlockSpec((bq, 128), lambda n, h, b: (b, 0)),
            pl.BlockSpec((bq, 128), lambda n, h, b: (b, 0)),
            pl.BlockSpec((t_len, 128), lambda n, h, b: (0, 0)),
            pl.BlockSpec((t_len, 128), lambda n, h, b: (0, 0)),
        ]
        args += [kc, vc, cos_t, sin_t, cos_t, sin_t]
    return pl.pallas_call(
        functools.partial(_diff_body, latent=latent, lam_init=lam_init),
        grid=(n_seq, B_HEADS, nqb),
        in_specs=in_specs,
        out_specs=pl.BlockSpec((bq, B_DV), lambda n, h, b: (n * nqb + b, h)),
        out_shape=jax.ShapeDtypeStruct((n_seq * t_len, B_HEADS * B_DV), BF16),
        compiler_params=_params(("arbitrary", "arbitrary", "arbitrary"), 40),
        name="diff_lat" if latent else "diff_ctx",
    )(*args)


def _mla_body(*refs, latent):
    it = iter(refs)
    qn_ref, qr_ref, kn_ref, v_ref, kr_ref = [next(it) for _ in range(5)]
    if latent:
        knc_ref, vc_ref, krc_ref, cq_ref, sq_ref, ck_ref, sk_ref = [next(it) for _ in range(7)]
    o_ref = next(it)
    head = pl.program_id(1)
    qr = qr_ref[...]
    kr = kr_ref[...]
    segs = []
    if latent:
        qr = _rope(qr, cq_ref[...], sq_ref[...])
        kr = _rope(kr, ck_ref[...], sk_ref[...])
        segs.append((knc_ref[...], krc_ref[...].astype(BF16), vc_ref[...]))
    segs.append((kn_ref[...], kr.astype(BF16), v_ref[...]))
    lane = lax.broadcasted_iota(jnp.int32, qr.shape, 1)
    qr = jnp.where((lane >> 6) == head % 2, qr, 0.0).astype(BF16)
    qn = qn_ref[...]
    scale = (C_NOPE + C_ROPE) ** -0.5
    ps = _softmax_segments([(_dot_nt(qn, kn) + _dot_nt(qr, krr)) * scale for kn, krr, _ in segs])
    o = None
    for p, (_, _, vv) in zip(ps, segs):
        part = _dot(p.astype(BF16), vv)
        o = part if o is None else o + part
    o_ref[...] = o.astype(o_ref.dtype)


def _mla_attn(qn, qr, kv, kr2, row0, n_seq, t_len, cache=None, rope=None, bq=256):
    latent = cache is not None
    rb = row0 // t_len
    qb0 = row0 // bq
    nqb = t_len // bq
    in_specs = [
        pl.BlockSpec((bq, C_NOPE), lambda n, h, b: (qb0 + n * nqb + b, h)),
        pl.BlockSpec((bq, 128), lambda n, h, b: (qb0 + n * nqb + b, h // 2)),
        pl.BlockSpec((t_len, C_NOPE), lambda n, h, b: (rb + n, 2 * h)),
        pl.BlockSpec((t_len, C_DV), lambda n, h, b: (rb + n, 2 * h + 1)),
        pl.BlockSpec((t_len, 128), lambda n, h, b: (rb + n, 0)),
    ]
    args = [qn, qr, kv, kv, kr2]
    if latent:
        kvc, krc = cache
        cos_t, sin_t = rope
        in_specs += [
            pl.BlockSpec((PAST_LEN, C_NOPE), lambda n, h, b: (n, 2 * h)),
            pl.BlockSpec((PAST_LEN, C_DV), lambda n, h, b: (n, 2 * h + 1)),
            pl.BlockSpec((PAST_LEN, 128), lambda n, h, b: (n, 0)),
            pl.BlockSpec((bq, 128), lambda n, h, b: (b, 0)),
            pl.BlockSpec((bq, 128), lambda n, h, b: (b, 0)),
            pl.BlockSpec((t_len, 128), lambda n, h, b: (0, 0)),
            pl.BlockSpec((t_len, 128), lambda n, h, b: (0, 0)),
        ]
        args += [kvc, kvc, krc, cos_t, sin_t, cos_t, sin_t]
    return pl.pallas_call(
        functools.partial(_mla_body, latent=latent),
        grid=(n_seq, C_HEADS, nqb),
        in_specs=in_specs,
        out_specs=pl.BlockSpec((bq, C_DV), lambda n, h, b: (n * nqb + b, h)),
        out_shape=jax.ShapeDtypeStruct((n_seq * t_len, C_HEADS * C_DV), BF16),
        compiler_params=_params(("arbitrary", "arbitrary", "arbitrary"), 40),
        name="mla_lat" if latent else "mla_ctx",
    )(*args)


def _router_body(x_ref, sc_ref, sh_ref, w_ref, b_ref, lg_ref, h_ref):
    h = x_ref[...] * (1.0 + sc_ref[0, 0]) + sh_ref[0, 0]
    h_ref[...] = h.astype(BF16)
    lg_ref[...] = jnp.dot(h, w_ref[...], precision=lax.Precision.HIGHEST, preferred_element_type=F32) + b_ref[...]


def _router(x, mod4, layer, w_r, b_r, tm=512):
    return pl.pallas_call(
        _router_body,
        grid=(NTOK // tm,),
        in_specs=[pl.BlockSpec((tm, D_MODEL), lambda i: (i, 0)),
                  _mod_spec(layer, 4, tm), _mod_spec(layer, 3, tm),
                  pl.BlockSpec((D_MODEL, 128), lambda i: (0, 0)),
                  pl.BlockSpec((1, 128), lambda i: (0, 0))],
        out_specs=[pl.BlockSpec((tm, 128), lambda i: (i, 0)),
                   pl.BlockSpec((tm, D_MODEL), lambda i: (i, 0))],
        out_shape=[jax.ShapeDtypeStruct((NTOK, 128), F32), jax.ShapeDtypeStruct((NTOK, D_MODEL), BF16)],
        compiler_params=_params(("arbitrary",), 40),
        name="moe_router",
    )(x, mod4, mod4, w_r, b_r)


def _expert_body(te_ref, tv_ref, x_ref, cw_ref, wg_ref, wu_ref, wd_ref, o_ref, wg_s, wu_s, wd_s):
    t = pl.program_id(0)
    prev = te_ref[jnp.maximum(t - 1, 0)]

    @pl.when((t == 0) | (te_ref[t] != prev))
    def _():
        wg_s[...] = wg_ref[...].astype(BF16)
        wu_s[...] = wu_ref[...].astype(BF16)
        wd_s[...] = wd_ref[...].astype(BF16)

    @pl.when(tv_ref[t] > 0)
    def _():
        x = x_ref[...]
        g = _dot(x, wg_s[...])
        u = _dot(x, wu_s[...])
        act = (g / (1.0 + jnp.exp(-g))) * u * cw_ref[...]
        o_ref[...] = _dot(act.astype(BF16), wd_s[...])

    @pl.when(tv_ref[t] == 0)
    def _():
        o_ref[...] = jnp.zeros(o_ref.shape, o_ref.dtype)


def _experts(xs, cw, tile_e, tile_v, w_gate, w_up, w_down, layer):
    grid_spec = pltpu.PrefetchScalarGridSpec(
        num_scalar_prefetch=2,
        grid=(MOE_TILES,),
        in_specs=[
            pl.BlockSpec((MOE_TM, D_MODEL), lambda t, te, tv: (t, 0)),
            pl.BlockSpec((MOE_TM, 1), lambda t, te, tv: (t, 0)),
            pl.BlockSpec((None, None, D_MODEL, D_EXPERT), lambda t, te, tv: (layer, te[t], 0, 0)),
            pl.BlockSpec((None, None, D_MODEL, D_EXPERT), lambda t, te, tv: (layer, te[t], 0, 0)),
            pl.BlockSpec((None, None, D_EXPERT, D_MODEL), lambda t, te, tv: (layer, te[t], 0, 0)),
        ],
        out_specs=pl.BlockSpec((MOE_TM, D_MODEL), lambda t, te, tv: (t, 0)),
        scratch_shapes=[pltpu.VMEM((D_MODEL, D_EXPERT), BF16), pltpu.VMEM((D_MODEL, D_EXPERT), BF16),
                        pltpu.VMEM((D_EXPERT, D_MODEL), BF16)],
    )
    return pl.pallas_call(
        _expert_body,
        grid_spec=grid_spec,
        out_shape=jax.ShapeDtypeStruct((MOE_ROWS, D_MODEL), F32),
        compiler_params=_params(("arbitrary",), 48),
        name="moe_experts",
    )(tile_e, tile_v, xs, cw, w_gate, w_up, w_down)


def _route(logits):
    g_logits = logits[:, :N_GROUPS]
    e_logits = logits[:, N_GROUPS:N_GROUPS + N_EXPERTS].reshape(NTOK, N_GROUPS, EXPERTS_PER_GROUP)
    g_prob = jax.nn.softmax(g_logits, axis=-1)
    g_sel = jnp.argmax(g_logits, axis=-1)
    g_w = jnp.take_along_axis(g_prob, g_sel[:, None], axis=-1)
    e_in_group = jnp.take_along_axis(e_logits, g_sel[:, None, None], axis=1)[:, 0]
    top_p, top_i = lax.top_k(jax.nn.softmax(e_in_group, axis=-1), EXPERT_TOP_K)
    top_w = g_w * top_p / jnp.sum(top_p, axis=-1, keepdims=True)
    expert_id = (g_sel[:, None] * EXPERTS_PER_GROUP + top_i).astype(jnp.int32)

    flat_e = expert_id.reshape(-1)
    onehot = (flat_e[:, None] == jnp.arange(N_EXPERTS, dtype=jnp.int32)[None, :]).astype(jnp.int32)
    csum = jnp.cumsum(onehot, axis=0)
    rank = jnp.sum(onehot * csum, axis=1) - 1
    counts = csum[-1]
    padded = ((counts + MOE_TM - 1) // MOE_TM) * MOE_TM
    pend = jnp.cumsum(padded)
    dest = (pend - padded)[flat_e] + rank
    tile_start = jnp.arange(MOE_TILES, dtype=jnp.int32) * MOE_TM
    tile_e = jnp.minimum(jnp.sum((tile_start[:, None] >= pend[None, :]).astype(jnp.int32), axis=1), N_EXPERTS - 1)
    tile_v = (tile_start < pend[-1]).astype(jnp.int32)
    src = jnp.zeros((MOE_ROWS,), jnp.int32).at[dest].set(jnp.arange(NTOK * EXPERT_TOP_K, dtype=jnp.int32) // EXPERT_TOP_K)
    cw = jnp.zeros((MOE_ROWS,), F32).at[dest].set(top_w.reshape(-1))
    return src, cw.reshape(MOE_ROWS, 1), dest.reshape(NTOK, EXPERT_TOP_K), tile_e.astype(jnp.int32), tile_v


def _moe(x, mod4, layer, w_r, b_r, w_gate, w_up, w_down, ln_g, ln_b):
    logits, h2 = _router(x, mod4, layer, w_r, b_r)
    src, cw, dest, tile_e, tile_v = _route(logits)
    xs = jnp.take(h2, src, axis=0)
    rows = _experts(xs, cw, tile_e, tile_v, w_gate, w_up, w_down, layer)
    y = jnp.take(rows, dest[:, 0], axis=0) + jnp.take(rows, dest[:, 1], axis=0)
    return _res_ln(x, y, mod4, layer, 5, ln_g, ln_b)


def kernel(x_prompt, x_sample, state_gla, cache_diff_k, cache_diff_v, cache_mla_ckv, cache_mla_krope, c, c_ctx, ada_w, ada_b, ln_g, ln_b, ab_w_in, gla_w_gate2, gla_b_gate2, gla_norm_g, diff_lambda, diff_norm_g, ab_w_out, mla_w_in, mla_q_norm_g, mla_w_uq, mla_kv_norm_g, mla_w_ukv, mla_w_out, moe_w_rg, moe_b_rg, moe_w_re, moe_b_re, moe_w_gate, moe_w_up, moe_w_down):
    x = jnp.concatenate([x_prompt.reshape(NCTX, D_MODEL), x_sample.reshape(NLAT, D_MODEL)], axis=0)
    cond = jnp.concatenate([c_ctx[None, :], c, jnp.zeros((N_COND - 1 - DEC_BATCH, D_MODEL), F32)], axis=0)
    mod4 = _ada_mod(cond, ada_w, ada_b).reshape(DEPTH, N_COND, 1, 6 * D_MODEL)
    cos_t, sin_t = _rope_tables()
    levf = jnp.asarray(_gla_level_table(False))
    levb = jnp.asarray(_gla_level_table(True))
    cache_k = cache_diff_k.reshape(DEC_BATCH, -1, PAST_LEN, B_HEADS * 2 * B_DH)
    cache_v = cache_diff_v.reshape(DEC_BATCH, -1, PAST_LEN, B_HEADS * B_DV)

    new_gla, new_dk, new_dv, new_ckv, new_kr = [], [], [], [], []
    for layer in range(DEPTH):
        i = layer // 2
        if layer % 2 == 0:
            w_d = jnp.concatenate([ab_w_in[i, :, AB_D0:], ab_w_in[i, :, AB_GG0:AB_D0],
                                   jnp.zeros((D_MODEL, 128 - 2 * A_GATE_RANK), F32)], axis=1)[None]
            pa = _mod_matmul(x, mod4, layer, ab_w_in, i, PA_N, 512, "ab_in_gla")
            pd = _mod_matmul(x, mod4, layer, w_d, 0, PD_N, 640, "ab_in_diff")
            wg = jnp.zeros((A_HEADS, 128, 2 * A_DK), F32)
            for j in range(2):
                blk = gla_w_gate2[i, j].reshape(A_GATE_RANK, A_HEADS, A_DK).transpose(1, 0, 2)
                wg = wg.at[:, j * A_GATE_RANK:(j + 1) * A_GATE_RANK, j * A_DK:(j + 1) * A_DK].set(blk)
            bg = gla_b_gate2[i].reshape(2, A_HEADS, A_DK).transpose(1, 0, 2).reshape(A_HEADS, 1, 2 * A_DK)
            g_gla = gla_norm_g[i][None, :]
            o_gla_c, st = _gla(pa, pd, wg, bg, g_gla, levf, levb, 0, BATCH, SEQ, None, 0, True)
            (o_gla_l,) = _gla(pa, pd, wg, bg, g_gla, levf, levb, NCTX, DEC_BATCH, DEC_SEQ, state_gla, i, False)
            lv = diff_lambda[i]
            lam_init = 0.8 - 0.6 * math.exp(-0.3 * layer)
            lam = (jnp.exp(jnp.sum(lv[0] * lv[1])) - jnp.exp(jnp.sum(lv[2] * lv[3])) + lam_init).reshape(1, 1)
            g_diff = diff_norm_g[i][None, :]
            o_diff_c = _diff_attn(pd, lam, g_diff, lam_init, 0, BATCH, SEQ)
            o_diff_l = _diff_attn(pd, lam, g_diff, lam_init, NCTX, DEC_BATCH, DEC_SEQ,
                                  cache=(cache_k, cache_v, i), rope=(cos_t, sin_t))
            o_gla = jnp.concatenate([o_gla_c, o_gla_l], axis=0)
            o_diff = jnp.concatenate([o_diff_c, o_diff_l], axis=0)
            x = _proj_ln([o_gla, o_diff], ab_w_out, i, x, mod4, layer, 2,
                         ln_g[layer, 0][None, :], ln_b[layer, 0][None, :], "ab_out_ln")
            new_gla.append(st)
            new_dk.append(pd[:NCTX, PD_K:PD_V].reshape(BATCH, SEQ, B_HEADS, 2, B_DH))
            new_dv.append(pd[:NCTX, PD_V:PD_GG].reshape(BATCH, SEQ, B_HEADS, B_DV))
        else:
            c_in = C_Q_RANK + C_KV_RANK + C_ROPE
            pm = _mod_matmul(x, mod4, layer, mla_w_in, i, c_in, c_in, "mla_in", tm=512)
            w_uq = mla_w_uq[i].reshape(C_Q_RANK, C_HEADS, C_NOPE + C_ROPE)
            w_uq = jnp.concatenate([w_uq[:, :, :C_NOPE].reshape(C_Q_RANK, -1),
                                    w_uq[:, :, C_NOPE:].reshape(C_Q_RANK, -1)], axis=1)
            g_q = mla_q_norm_g[i][None, :]
            g_kv = mla_kv_norm_g[i][None, :]
            qn = _rms_matmul(pm, 0, C_Q_RANK, g_q, w_uq, 0, C_HEADS * C_NOPE, 512, 1024, BF16, "mla_q_nope")
            qr = _rms_matmul(pm, 0, C_Q_RANK, g_q, w_uq, C_HEADS * C_NOPE // 512, C_HEADS * C_ROPE, 512, 1024, F32,
                             "mla_q_rope")
            kv, ckv_n = _rms_matmul(pm, C_Q_RANK // C_KV_RANK, C_KV_RANK, g_kv, mla_w_ukv[i], 0,
                                    C_HEADS * (C_NOPE + C_DV), 1024, 1024, BF16, "mla_kv", emit_norm=True)
            kv_c = _rms_matmul(cache_mla_ckv[:, i].reshape(DEC_BATCH * PAST_LEN, C_KV_RANK), 0, C_KV_RANK, g_kv,
                               mla_w_ukv[i], 0, C_HEADS * (C_NOPE + C_DV), 1024, 1024, BF16, "mla_kv_cache",
                               norm=False)
            krope = pm[:, C_Q_RANK + C_KV_RANK:]
            kr2 = jnp.concatenate([krope, krope], axis=1)
            krc = cache_mla_krope[:, i].reshape(DEC_BATCH * PAST_LEN, C_ROPE)
            krc2 = jnp.concatenate([krc, krc], axis=1)
            o_c = _mla_attn(qn, qr, kv, kr2, 0, BATCH, SEQ)
            o_l = _mla_attn(qn, qr, kv, kr2, NCTX, DEC_BATCH, DEC_SEQ, cache=(kv_c, krc2), rope=(cos_t, sin_t))
            o_mla = jnp.concatenate([o_c, o_l], axis=0)
            x = _proj_ln([o_mla], mla_w_out, i, x, mod4, layer, 2,
                         ln_g[layer, 0][None, :], ln_b[layer, 0][None, :], "mla_out_ln")
            new_ckv.append(ckv_n[:NCTX].reshape(BATCH, SEQ, C_KV_RANK))
            new_kr.append(krope[:NCTX].reshape(BATCH, SEQ, C_ROPE))
        w_r = jnp.concatenate([moe_w_rg[layer], moe_w_re[layer],
                               jnp.zeros((D_MODEL, 128 - N_GROUPS - N_EXPERTS), F32)], axis=1)
        b_r = jnp.concatenate([moe_b_rg[layer], moe_b_re[layer],
                               jnp.zeros((128 - N_GROUPS - N_EXPERTS,), F32)])[None, :]
        x = _moe(x, mod4, layer, w_r, b_r, moe_w_gate, moe_w_up, moe_w_down,
                 ln_g[layer, 1][None, :], ln_b[layer, 1][None, :])

    return (x[:NCTX].reshape(BATCH, SEQ, D_MODEL),
            x[NCTX:].reshape(DEC_BATCH, DEC_SEQ, D_MODEL),
            jnp.stack(new_gla, axis=1), jnp.stack(new_dk, axis=1), jnp.stack(new_dv, axis=1),
            jnp.stack(new_ckv, axis=1), jnp.stack(new_kr, axis=1))
```

```python
import functools
import math

import numpy as np
import jax
import jax.numpy as jnp
from jax import lax
from jax.experimental import pallas as pl
from jax.experimental.pallas import tpu as pltpu

F32 = jnp.float32
BF16 = jnp.bfloat16

D_MODEL = 2048
BATCH = 16
SEQ = 256
DEPTH = 4
DEC_BATCH = 4
DEC_SEQ = 1024
PAST_LEN = 256
GRID_W = 64
A_HEADS = 4
A_DK = 128
A_DV = 256
A_GATE_RANK = 16
A_TAU = 16.0
B_HEADS = 8
B_DH = 64
B_DV = 2 * B_DH
C_HEADS = 16
C_Q_RANK = 512
C_KV_RANK = 256
C_NOPE = 128
C_ROPE = 64
C_DV = 128
N_GROUPS = 4
EXPERTS_PER_GROUP = 4
N_EXPERTS = N_GROUPS * EXPERTS_PER_GROUP
EXPERT_TOP_K = 2
D_EXPERT = 512
ROPE_THETA = 10000.0
LN_EPS = 1e-5
RMS_EPS = 1e-6
DEEPNORM_ALPHA = (2.0 * DEPTH) ** 0.25

NCTX = BATCH * SEQ
NLAT = DEC_BATCH * DEC_SEQ
NTOK = NCTX + NLAT
N_COND = 8
GLA_CHUNK = 256
GLA_LEVELS = 8
MOE_TM = 512
LOG2E = 1.4426950408889634
MOE_ROWS = NTOK * EXPERT_TOP_K + N_EXPERTS * MOE_TM
MOE_TILES = MOE_ROWS // MOE_TM

PA_Q, PA_K, PA_V, PA_R = 0, A_HEADS * A_DK, 2 * A_HEADS * A_DK, 2 * A_HEADS * A_DK + A_HEADS * A_DV
PA_N = PA_R + A_HEADS * A_DV
AB_GG0 = PA_N
AB_D0 = PA_N + 2 * A_GATE_RANK
PD_Q, PD_K, PD_V = 0, B_HEADS * 2 * B_DH, 2 * B_HEADS * 2 * B_DH
PD_GG = PD_V + B_HEADS * B_DV
PD_N = PD_GG + 128


def _params(sem, vmem_mb):
    return pltpu.CompilerParams(dimension_semantics=sem, vmem_limit_bytes=vmem_mb * 2 ** 20)


def _cond_row(i, tm):
    return jnp.where(i * tm < NCTX, 0, 1 + (i * tm - NCTX) // DEC_SEQ)


def _dot(a, b):
    return jnp.dot(a, b, preferred_element_type=F32)


def _dot_nt(a, b):
    return lax.dot_general(a, b, (((1,), (1,)), ((), ())), preferred_element_type=F32)


def _dot_tn(a, b):
    return lax.dot_general(a, b, (((0,), (0,)), ((), ())), preferred_element_type=F32)


def _ada_body(c_ref, w_ref, b_ref, o_ref):
    c = c_ref[...]
    a = (c / (1.0 + jnp.exp(-c))).astype(BF16)
    o_ref[0] = _dot(a, w_ref[0].astype(BF16)) + b_ref[0]


def _ada_mod(cond, ada_w, ada_b):
    tn = 1024
    return pl.pallas_call(
        _ada_body,
        grid=(DEPTH, 6 * D_MODEL // tn),
        in_specs=[pl.BlockSpec((N_COND, D_MODEL), lambda l, j: (0, 0)),
                  pl.BlockSpec((1, D_MODEL, tn), lambda l, j: (l, 0, j)),
                  pl.BlockSpec((1, 1, tn), lambda l, j: (l, 0, j))],
        out_specs=pl.BlockSpec((1, N_COND, tn), lambda l, j: (l, 0, j)),
        out_shape=jax.ShapeDtypeStruct((DEPTH, N_COND, 6 * D_MODEL), F32),
        compiler_params=_params(("arbitrary", "arbitrary"), 40),
        name="ada_mod",
    )(cond, ada_w, ada_b.reshape(DEPTH, 1, 6 * D_MODEL))


def _mod_spec(layer, chunk, tm):
    return pl.BlockSpec((1, 1, 1, D_MODEL), lambda i, *_: (layer, _cond_row(i, tm), 0, chunk))


def _mod_mm_body(x_ref, sc_ref, sh_ref, w_ref, o_ref, h_ref):
    @pl.when(pl.program_id(1) == 0)
    def _():
        h_ref[...] = (x_ref[...] * (1.0 + sc_ref[0, 0]) + sh_ref[0, 0]).astype(BF16)

    o_ref[...] = _dot(h_ref[...], w_ref[...].astype(BF16)).astype(o_ref.dtype)


def _mod_matmul(x, mod4, layer, w, w_layer, n_out, tn, name, tm=1024, out_dtype=F32):
    return pl.pallas_call(
        _mod_mm_body,
        grid=(NTOK // tm, n_out // tn),
        in_specs=[pl.BlockSpec((tm, D_MODEL), lambda i, j: (i, 0)),
                  _mod_spec(layer, 1, tm), _mod_spec(layer, 0, tm),
                  pl.BlockSpec((None, D_MODEL, tn), lambda i, j: (w_layer, 0, j))],
        out_specs=pl.BlockSpec((tm, tn), lambda i, j: (i, j)),
        out_shape=jax.ShapeDtypeStruct((NTOK, n_out), out_dtype),
        scratch_shapes=[pltpu.VMEM((tm, D_MODEL), BF16)],
        compiler_params=_params(("arbitrary", "arbitrary"), 48),
        name=name,
    )(x, mod4, mod4, w)


def _rms_mm_body(x_ref, g_ref, w_ref, *rest, norm, emit_norm):
    if emit_norm:
        o_ref, n_ref, h_ref = rest
    else:
        o_ref, h_ref = rest

    @pl.when(pl.program_id(1) == 0)
    def _():
        x = x_ref[...]
        if norm:
            x = x * lax.rsqrt(jnp.mean(x * x, axis=-1, keepdims=True) + RMS_EPS) * g_ref[...]
        h_ref[...] = x.astype(BF16)
        if emit_norm:
            n_ref[...] = x

    o_ref[...] = _dot(h_ref[...], w_ref[...].astype(BF16)).astype(o_ref.dtype)


def _rms_matmul(x, col_block, k, g, w, col0_blocks, n_out, tn, tm, out_dtype, name, norm=True, emit_norm=False):
    rows = x.shape[0]
    out_shape = [jax.ShapeDtypeStruct((rows, n_out), out_dtype)]
    out_specs = [pl.BlockSpec((tm, tn), lambda i, j: (i, j))]
    if emit_norm:
        out_shape.append(jax.ShapeDtypeStruct((rows, k), F32))
        out_specs.append(pl.BlockSpec((tm, k), lambda i, j: (i, 0)))
    res = pl.pallas_call(
        functools.partial(_rms_mm_body, norm=norm, emit_norm=emit_norm),
        grid=(rows // tm, n_out // tn),
        in_specs=[pl.BlockSpec((tm, k), lambda i, j: (i, col_block)),
                  pl.BlockSpec((1, k), lambda i, j: (0, 0)),
                  pl.BlockSpec((k, tn), lambda i, j: (0, col0_blocks + j))],
        out_specs=out_specs,
        out_shape=out_shape,
        scratch_shapes=[pltpu.VMEM((tm, k), BF16)],
        compiler_params=_params(("arbitrary", "arbitrary"), 40),
        name=name,
    )(x, g, w)
    return res if emit_norm else res[0]


def _layer_norm(z, g, b):
    mu = jnp.mean(z, axis=-1, keepdims=True)
    zc = z - mu
    var = jnp.mean(zc * zc, axis=-1, keepdims=True)
    return zc * lax.rsqrt(var + LN_EPS) * g + b


LN_ROWS = 256


def _proj_ln_body(*refs, n_lhs, n_j, tm, tn):
    lhs = refs[:n_lhs]
    ws = refs[n_lhs:2 * n_lhs]
    x_ref, gate_ref, g_ref, b_ref, o_ref = refs[2 * n_lhs:]
    j = pl.program_id(1)
    y = _dot(lhs[0][...], ws[0][...].astype(BF16))
    for a, w in zip(lhs[1:], ws[1:]):
        y = y + _dot(a[...], w[...].astype(BF16))
    o_ref[:, pl.ds(pl.multiple_of(j * tn, tn), tn)] = y

    @pl.when(j == n_j - 1)
    def _():
        def body(r, carry):
            rows = pl.ds(pl.multiple_of(r * LN_ROWS, LN_ROWS), LN_ROWS)
            z = DEEPNORM_ALPHA * x_ref[rows, :] + gate_ref[0, 0] * o_ref[rows, :]
            o_ref[rows, :] = _layer_norm(z, g_ref[...], b_ref[...])
            return carry
        lax.fori_loop(0, tm // LN_ROWS, body, 0)


def _proj_ln(lhs_list, w, w_layer, x, mod4, layer, gate_chunk, ln_g, ln_b, name, tm=1024, tn=512):
    n_lhs = len(lhs_list)
    kk = lhs_list[0].shape[1]
    n_j = D_MODEL // tn
    in_specs = [pl.BlockSpec((tm, kk), lambda i, j: (i, 0)) for _ in lhs_list]
    in_specs += [pl.BlockSpec((None, kk, tn), functools.partial(lambda i, j, r: (w_layer, r, j), r=r))
                 for r in range(n_lhs)]
    in_specs += [pl.BlockSpec((tm, D_MODEL), lambda i, j: (i, 0)),
                 _mod_spec(layer, gate_chunk, tm),
                 pl.BlockSpec((1, D_MODEL), lambda i, j: (0, 0)),
                 pl.BlockSpec((1, D_MODEL), lambda i, j: (0, 0))]
    return pl.pallas_call(
        functools.partial(_proj_ln_body, n_lhs=n_lhs, n_j=n_j, tm=tm, tn=tn),
        grid=(NTOK // tm, n_j),
        in_specs=in_specs,
        out_specs=pl.BlockSpec((tm, D_MODEL), lambda i, j: (i, 0)),
        out_shape=jax.ShapeDtypeStruct((NTOK, D_MODEL), F32),
        compiler_params=_params(("arbitrary", "arbitrary"), 56),
        name=name,
    )(*lhs_list, *([w] * n_lhs), x, mod4, ln_g, ln_b)


def _res_ln_body(x_ref, y0_ref, y1_ref, rt_ref, gate_ref, g_ref, b_ref, o_ref):
    rt = rt_ref[...]
    y = rt[:, RT_W0:RT_W0 + 1] * y0_ref[...] + rt[:, RT_W1:RT_W1 + 1] * y1_ref[...]
    z = DEEPNORM_ALPHA * x_ref[...] + gate_ref[0, 0] * y
    o_ref[...] = _layer_norm(z, g_ref[...], b_ref[...])


def _res_ln(x, rows2, route, mod4, layer, gate_chunk, ln_g, ln_b, tm=512):
    n_i = NTOK // tm
    row = pl.BlockSpec((tm, D_MODEL), lambda i: (i, 0))
    vec = pl.BlockSpec((1, D_MODEL), lambda i: (0, 0))
    return pl.pallas_call(
        _res_ln_body,
        grid=(n_i,),
        in_specs=[row, row, pl.BlockSpec((tm, D_MODEL), lambda i: (n_i + i, 0)),
                  pl.BlockSpec((tm, 128), lambda i: (i, 0)),
                  _mod_spec(layer, gate_chunk, tm), vec, vec],
        out_specs=row,
        out_shape=jax.ShapeDtypeStruct((NTOK, D_MODEL), F32),
        compiler_params=_params(("arbitrary",), 48),
        name="moe_res_ln",
    )(x, rows2, rows2, route, mod4, ln_g, ln_b)


def _gla_level_table(reverse):
    t = np.arange(GLA_CHUNK)[:, None]
    s = np.arange(GLA_CHUNK)[None, :]
    x = t ^ s
    lev = np.where(x > 0, np.floor(np.log2(np.maximum(x, 1))).astype(np.int32), -1)
    live = (t < s) if reverse else (t > s)
    return np.where(live | (t == s), lev, -2).astype(np.int32)


def _anchor_rows(u, level, reverse):
    c = u.shape[0]
    n = c // 8
    u3 = u.reshape(n, 8, 128)
    half = 1 << level
    if level < 3:
        blk = 2 * half
        sub = lax.broadcasted_iota(jnp.int32, (n, 8, 128), 1)
        out = None
        for j in range(8 // blk):
            r = j * blk + (half if reverse else half - 1)
            piece = jnp.broadcast_to(u3[:, r:r + 1, :], (n, 8, 128))
            out = piece if out is None else jnp.where(sub >= j * blk, piece, out)
        return out.reshape(c, 128)
    m = 1 << (level - 2)
    r = 0 if reverse else 7
    e = jnp.broadcast_to(u3[:, r:r + 1, :], (n, 8, 128)).reshape(n // m, m, 8, 128)
    idx = m // 2 if reverse else m // 2 - 1
    return jnp.broadcast_to(e[:, idx:idx + 1], (n // m, m, 8, 128)).reshape(c, 128)


def _gla_intra(q, k, la, lev, reverse):
    rowk = lax.broadcasted_iota(jnp.int32, (GLA_CHUNK, 128), 0)
    scores = jnp.where(lev == -1, _dot_nt(q.astype(BF16), k.astype(BF16)), 0.0)
    u = la
    for level in range(GLA_LEVELS):
        anchor = _anchor_rows(u, level, reverse)
        qa = (q * jnp.exp(u)).astype(BF16)
        ka = (k * jnp.exp(jnp.minimum(anchor - u, 0.0))).astype(BF16)
        scores = jnp.where(lev == level, _dot_nt(qa, ka), scores)
        query_side = ((rowk >> level) & 1) == (0 if reverse else 1)
        u = u + jnp.where(query_side, anchor, 0.0)
    return scores, u


def _gla_body(*refs, t_len, has_s0, emit_state, has_prev):
    it = iter(refs)
    q_ref, k_ref, v_ref, r_ref, gg_ref, wg_ref, bg_ref, g_ref, levf_ref, levb_ref = [next(it) for _ in range(10)]
    s0_ref = next(it) if has_s0 else None
    if has_prev:
        next(it)
    o_ref = next(it)
    st_ref = next(it) if emit_state else None
    acc_ref, sf_ref, sb_ref, la_ref = [next(it) for _ in range(4)]
    c = GLA_CHUNK
    n_ch = t_len // c

    gl = _dot(gg_ref[...].astype(BF16), wg_ref[...].astype(BF16)) + bg_ref[...]
    la_ref[...] = (jnp.minimum(gl, 0.0) - jnp.log1p(jnp.exp(-jnp.abs(gl)))) * (1.0 / A_TAU)
    if has_s0:
        sf_ref[...] = s0_ref[0]
        sb_ref[...] = s0_ref[1]

    eye = (lax.broadcasted_iota(jnp.int32, (A_DK, A_DK), 0) == lax.broadcasted_iota(jnp.int32, (A_DK, A_DK), 1))

    def chunk_step(ci, reverse, with_state):
        rows = pl.ds(ci * c if isinstance(ci, int) else pl.multiple_of(ci * c, c), c)
        q = q_ref[rows, :] * (A_DK ** -0.5)
        k = k_ref[rows, :]
        v = v_ref[rows, :].astype(BF16)
        la = la_ref[rows, A_DK:2 * A_DK] if reverse else la_ref[rows, 0:A_DK]
        s_ref = sb_ref if reverse else sf_ref
        scores, cum = _gla_intra(q, k, la, (levb_ref if reverse else levf_ref)[...], reverse)
        o = _dot(scores.astype(BF16), v)
        if with_state:
            o = o + _dot((q * jnp.exp(cum)).astype(BF16), s_ref[...].astype(BF16))
        tot = jnp.broadcast_to(cum[0:1, :] if reverse else cum[c - 1:c, :], (c, A_DK))
        delta = _dot_tn((k * jnp.exp(tot - cum)).astype(BF16), v)
        if with_state:
            decay = jnp.sum(jnp.where(eye, jnp.exp(tot[0:A_DK, :]), 0.0), axis=1, keepdims=True)
            s_ref[...] = decay * s_ref[...] + delta
        else:
            s_ref[...] = delta
        if not reverse:
            acc_ref[rows, :] = o
        else:
            y = acc_ref[rows, :] + o
            y = y * lax.rsqrt(jnp.mean(y * y, axis=-1, keepdims=True) + RMS_EPS) * g_ref[...]
            r = r_ref[rows, :]
            o_ref[rows, :] = (y * (r / (1.0 + jnp.exp(-r)))).astype(o_ref.dtype)

    for reverse in (False, True):
        first = (n_ch - 1) if reverse else 0
        chunk_step(first, reverse, has_s0)
        if n_ch > 1:
            def body(i, carry, reverse=reverse):
                chunk_step((n_ch - 1 - i) if reverse else i, reverse, True)
                return carry
            lax.fori_loop(1, n_ch, body, 0)

    if emit_state:
        st_ref[0] = sf_ref[...]
        st_ref[1] = sb_ref[...]


def _alias_prev(in_specs, args, prev):
    if prev is None:
        return {}
    in_specs.append(pl.BlockSpec(memory_space=pl.ANY))
    args.append(prev)
    return {len(args) - 1: 0}


def _gla(pa, pd, wg, bg, gla_g, levf, levb, row0, n_seq, t_len, s0, s0_layer, emit_state, prev=None):
    rb = row0 // t_len
    kb, vb, rbk = PA_K // A_DK, PA_V // A_DV, PA_R // A_DV
    in_specs = [
        pl.BlockSpec((t_len, A_DK), lambda n, h: (rb + n, h)),
        pl.BlockSpec((t_len, A_DK), lambda n, h: (rb + n, kb + h)),
        pl.BlockSpec((t_len, A_DV), lambda n, h: (rb + n, vb + h)),
        pl.BlockSpec((t_len, A_DV), lambda n, h: (rb + n, rbk + h)),
        pl.BlockSpec((t_len, 128), lambda n, h: (rb + n, PD_GG // 128)),
        pl.BlockSpec((None, 128, 2 * A_DK), lambda n, h: (h, 0, 0)),
        pl.BlockSpec((None, 1, 2 * A_DK), lambda n, h: (h, 0, 0)),
        pl.BlockSpec((1, A_DV), lambda n, h: (0, 0)),
        pl.BlockSpec((GLA_CHUNK, GLA_CHUNK), lambda n, h: (0, 0)),
        pl.BlockSpec((GLA_CHUNK, GLA_CHUNK), lambda n, h: (0, 0)),
    ]
    args = [pa, pa, pa, pa, pd, wg, bg, gla_g, levf, levb]
    if s0 is not None:
        in_specs.append(pl.BlockSpec((None, None, 2, None, A_DK, A_DV), lambda n, h: (n, s0_layer, 0, h, 0, 0)))
        args.append(s0)
    aliases = _alias_prev(in_specs, args, prev)
    out_shape = [jax.ShapeDtypeStruct((NTOK, A_HEADS * A_DV), BF16)]
    out_specs = [pl.BlockSpec((t_len, A_DV), lambda n, h: (rb + n, h))]
    if emit_state:
        out_shape.append(jax.ShapeDtypeStruct((n_seq, 2, A_HEADS, A_DK, A_DV), F32))
        out_specs.append(pl.BlockSpec((None, 2, None, A_DK, A_DV), lambda n, h: (n, 0, h, 0, 0)))
    res = pl.pallas_call(
        functools.partial(_gla_body, t_len=t_len, has_s0=s0 is not None, emit_state=emit_state,
                          has_prev=prev is not None),
        grid=(n_seq, A_HEADS),
        in_specs=in_specs,
        out_specs=out_specs,
        out_shape=out_shape,
        input_output_aliases=aliases,
        scratch_shapes=[pltpu.VMEM((t_len, A_DV), F32), pltpu.VMEM((A_DK, A_DV), F32),
                        pltpu.VMEM((A_DK, A_DV), F32), pltpu.VMEM((t_len, 2 * A_DK), F32)],
        compiler_params=_params(("arbitrary", "arbitrary"), 40),
        name="gla_ctx" if s0 is None else "gla_lat",
    )(*args)
    return res


def _rope(x, cos, sin):
    lane = lax.broadcasted_iota(jnp.int32, x.shape, 1)
    partner = jnp.where((lane & 32) == 0, pltpu.roll(x, 96, 1), pltpu.roll(x, 32, 1))
    return x * cos + partner * sin


def _rope_tables():
    n_rows = DEC_SEQ // GRID_W
    rows = jnp.repeat(jnp.arange(n_rows, dtype=F32), GRID_W)
    cols = jnp.tile(jnp.arange(GRID_W, dtype=F32), n_rows)
    quarter = C_ROPE // 4
    freqs = ROPE_THETA ** (-jnp.arange(quarter, dtype=F32) / quarter)
    ang = jnp.concatenate([rows[:, None] * freqs, cols[:, None] * freqs], axis=-1)
    cos, sin = jnp.cos(ang), jnp.sin(ang)
    cos_t = jnp.tile(cos, (1, 4))
    sin_t = jnp.tile(jnp.concatenate([-sin, sin], axis=-1), (1, 2))
    return cos_t, sin_t


def _softmax_parts(scores, scale):
    m = functools.reduce(jnp.maximum, [jnp.max(s, axis=-1, keepdims=True) for s in scores])
    ps = [jnp.exp2((s - m) * (scale * LOG2E)) for s in scores]
    inv = 1.0 / functools.reduce(jnp.add, [jnp.sum(p, axis=-1, keepdims=True) for p in ps])
    return ps, inv


def _pv(ps, values):
    return functools.reduce(jnp.add, [_dot(p.astype(BF16), v) for p, v in zip(ps, values)])


def _diff_body(*refs, latent, lam_init, has_prev):
    it = iter(refs)
    lam_ref, q_ref, k_ref, v_ref, g_ref = [next(it) for _ in range(5)]
    if latent:
        kc_ref, vc_ref, cq_ref, sq_ref, ck_ref, sk_ref = [next(it) for _ in range(6)]
    if has_prev:
        next(it)
    o_ref = next(it)
    k_s, v_s = next(it), next(it)
    lam = lam_ref[0, 0]

    @pl.when(pl.program_id(2) == 0)
    def _():
        k = k_ref[...]
        if latent:
            k = _rope(k, ck_ref[...], sk_ref[...])
        k_s[...] = k.astype(BF16)
        v_s[...] = v_ref[...].astype(BF16)

    q = q_ref[...]
    keys, values = [k_s[...]], [v_s[...]]
    if latent:
        q = _rope(q, cq_ref[...], sq_ref[...])
        keys.insert(0, kc_ref[...].astype(BF16))
        values.insert(0, vc_ref[...].astype(BF16))
    lane = lax.broadcasted_iota(jnp.int32, q.shape, 1)
    q1 = jnp.where(lane < B_DH, q, 0.0).astype(BF16)
    q2 = jnp.where(lane >= B_DH, q, 0.0).astype(BF16)
    scale = B_DH ** -0.5
    p1, inv1 = _softmax_parts([_dot_nt(q1, kk) for kk in keys], scale)
    p2, inv2 = _softmax_parts([_dot_nt(q2, kk) for kk in keys], scale)
    o = _pv(p1, values) * inv1 - _pv(p2, values) * (lam * inv2)
    o = o * lax.rsqrt(jnp.mean(o * o, axis=-1, keepdims=True) + RMS_EPS) * g_ref[...] * (1.0 - lam_init)
    o_ref[...] = o.astype(o_ref.dtype)


def _diff_attn(pd, lam, diff_g, lam_init, row0, n_seq, t_len, cache=None, rope=None, bq=256, prev=None):
    latent = cache is not None
    rb = row0 // t_len
    qb0 = row0 // bq
    nqb = t_len // bq
    in_specs = [
        pl.BlockSpec(memory_space=pltpu.SMEM),
        pl.BlockSpec((bq, 128), lambda n, h, b: (qb0 + n * nqb + b, PD_Q // 128 + h)),
        pl.BlockSpec((t_len, 128), lambda n, h, b: (rb + n, PD_K // 128 + h)),
        pl.BlockSpec((t_len, 128), lambda n, h, b: (rb + n, PD_V // 128 + h)),
        pl.BlockSpec((1, B_DV), lambda n, h, b: (0, 0)),
    ]
    args = [lam, pd, pd, pd, diff_g]
    if latent:
        kc, vc, layer = cache
        cos_t, sin_t = rope
        in_specs += [
            pl.BlockSpec((None, None, PAST_LEN, 128), lambda n, h, b: (n, layer, 0, h)),
            pl.BlockSpec((None, None, PAST_LEN, 128), lambda n, h, b: (n, layer, 0, h)),
            pl.BlockSpec((bq, 128), lambda n, h, b: (b, 0)),
            pl.BlockSpec((bq, 128), lambda n, h, b: (b, 0)),
            pl.BlockSpec((t_len, 128), lambda n, h, b: (0, 0)),
            pl.BlockSpec((t_len, 128), lambda n, h, b: (0, 0)),
        ]
        args += [kc, vc, cos_t, sin_t, cos_t, sin_t]
    aliases = _alias_prev(in_specs, args, prev)
    return pl.pallas_call(
        functools.partial(_diff_body, latent=latent, lam_init=lam_init, has_prev=prev is not None),
        grid=(n_seq, B_HEADS, nqb),
        in_specs=in_specs,
        out_specs=pl.BlockSpec((bq, B_DV), lambda n, h, b: (qb0 + n * nqb + b, h)),
        out_shape=jax.ShapeDtypeStruct((NTOK, B_HEADS * B_DV), BF16),
        input_output_aliases=aliases,
        scratch_shapes=[pltpu.VMEM((t_len, 128), BF16), pltpu.VMEM((t_len, B_DV), BF16)],
        compiler_params=_params(("arbitrary", "arbitrary", "arbitrary"), 40),
        name="diff_lat" if latent else "diff_ctx",
    )(*args)


def _mla_body(*refs, latent, has_prev):
    it = iter(refs)
    qn_ref, qr_ref, kn_ref, v_ref, kr_ref = [next(it) for _ in range(5)]
    if latent:
        knc_ref, vc_ref, krc_ref, cq_ref, sq_ref, ck_ref, sk_ref = [next(it) for _ in range(7)]
    if has_prev:
        next(it)
    o_ref = next(it)
    kr_s = next(it)
    head = pl.program_id(1)

    @pl.when((head == 0) & (pl.program_id(2) == 0))
    def _():
        kr = kr_ref[...]
        if latent:
            kr = _rope(kr, ck_ref[...], sk_ref[...])
        kr_s[...] = kr.astype(BF16)

    segs = [(kn_ref[...], kr_s[...], v_ref[...])]
    if latent:
        segs.insert(0, (knc_ref[...], krc_ref[...].astype(BF16), vc_ref[...]))
    scale = (C_NOPE + C_ROPE) ** -0.5
    qr = qr_ref[...]
    if latent:
        qr = _rope(qr, cq_ref[...], sq_ref[...])
    lane = lax.broadcasted_iota(jnp.int32, qr.shape, 1)
    qr = jnp.where((lane >> 6) == head % 2, qr, 0.0).astype(BF16)
    qn = qn_ref[...]
    ps, inv = _softmax_parts([_dot_nt(qn, kn) + _dot_nt(qr, krr) for kn, krr, _ in segs], scale)
    o_ref[...] = (_pv(ps, [vv for _, _, vv in segs]) * inv).astype(o_ref.dtype)


def _mla_attn(qn, qr, kv, kr2, row0, n_seq, t_len, cache=None, rope=None, bq=256, prev=None):
    latent = cache is not None
    rb = row0 // t_len
    qb0 = row0 // bq
    nqb = t_len // bq
    in_specs = [
        pl.BlockSpec((bq, C_NOPE), lambda n, h, b: (qb0 + n * nqb + b, h)),
        pl.BlockSpec((bq, 128), lambda n, h, b: (qb0 + n * nqb + b, h // 2)),
        pl.BlockSpec((t_len, C_NOPE), lambda n, h, b: (rb + n, 2 * h)),
        pl.BlockSpec((t_len, C_DV), lambda n, h, b: (rb + n, 2 * h + 1)),
        pl.BlockSpec((t_len, 128), lambda n, h, b: (rb + n, 0)),
    ]
    args = [qn, qr, kv, kv, kr2]
    if latent:
        kvc, krc = cache
        cos_t, sin_t = rope
        in_specs += [
            pl.BlockSpec((PAST_LEN, C_NOPE), lambda n, h, b: (n, 2 * h)),
            pl.BlockSpec((PAST_LEN, C_DV), lambda n, h, b: (n, 2 * h + 1)),
            pl.BlockSpec((PAST_LEN, 128), lambda n, h, b: (n, 0)),
            pl.BlockSpec((bq, 128), lambda n, h, b: (b, 0)),
            pl.BlockSpec((bq, 128), lambda n, h, b: (b, 0)),
            pl.BlockSpec((t_len, 128), lambda n, h, b: (0, 0)),
            pl.BlockSpec((t_len, 128), lambda n, h, b: (0, 0)),
        ]
        args += [kvc, kvc, krc, cos_t, sin_t, cos_t, sin_t]
    aliases = _alias_prev(in_specs, args, prev)
    return pl.pallas_call(
        functools.partial(_mla_body, latent=latent, has_prev=prev is not None),
        grid=(n_seq, C_HEADS, nqb),
        in_specs=in_specs,
        out_specs=pl.BlockSpec((bq, C_DV), lambda n, h, b: (qb0 + n * nqb + b, h)),
        out_shape=jax.ShapeDtypeStruct((NTOK, C_HEADS * C_DV), BF16),
        input_output_aliases=aliases,
        scratch_shapes=[pltpu.VMEM((t_len, 128), BF16)],
        compiler_params=_params(("arbitrary", "arbitrary", "arbitrary"), 40),
        name="mla_lat" if latent else "mla_ctx",
    )(*args)


RT_E0, RT_E1, RT_R0, RT_R1, RT_W0, RT_W1 = range(6)
ROUTER_TM = 512


def _router_body(x_ref, sc_ref, sh_ref, w_ref, b_ref, tri_ref, rt_ref, cnt_ref, h_ref, carry_ref):
    h = x_ref[...] * (1.0 + sc_ref[0, 0]) + sh_ref[0, 0]
    h_ref[...] = h.astype(BF16)
    lg = jnp.dot(h, w_ref[...], precision=lax.Precision.HIGHEST, preferred_element_type=F32) + b_ref[...]

    @pl.when(pl.program_id(0) == 0)
    def _():
        carry_ref[...] = jnp.zeros(carry_ref.shape, F32)

    lane = lax.broadcasted_iota(jnp.int32, lg.shape, 1)
    ninf = -jnp.inf
    is_g = lane < N_GROUPS
    gl = jnp.where(is_g, lg, ninf)
    gmax = jnp.max(gl, axis=1, keepdims=True)
    g_sel = jnp.min(jnp.where(gl == gmax, lane, 128), axis=1, keepdims=True)
    g_w = 1.0 / jnp.sum(jnp.where(is_g, jnp.exp(lg - gmax), 0.0), axis=1, keepdims=True)
    el = jnp.where(((lane - N_GROUPS) >> 2) == g_sel, lg, ninf)
    m1 = jnp.max(el, axis=1, keepdims=True)
    i1 = jnp.min(jnp.where(el == m1, lane, 128), axis=1, keepdims=True)
    el2 = jnp.where(lane == i1, ninf, el)
    m2 = jnp.max(el2, axis=1, keepdims=True)
    i2 = jnp.min(jnp.where(el2 == m2, lane, 128), axis=1, keepdims=True)
    p2 = jnp.exp(m2 - m1)
    w0 = g_w / (1.0 + p2)
    w1 = w0 * p2
    e0 = i1 - N_GROUPS
    e1 = i2 - N_GROUPS
    hit0 = lane == e0
    hit1 = lane == e1
    onehot = jnp.where(hit0, 1.0, jnp.where(hit1, 1.0, 0.0))
    before = _dot(tri_ref[...], onehot.astype(BF16)) + carry_ref[...]
    r0 = jnp.sum(jnp.where(hit0, before, 0.0), axis=1, keepdims=True)
    r1 = jnp.sum(jnp.where(hit1, before, 0.0), axis=1, keepdims=True)
    carry = carry_ref[...] + jnp.sum(onehot, axis=0, keepdims=True)
    carry_ref[...] = carry
    cnt_ref[...] = jnp.broadcast_to(carry, cnt_ref.shape)
    rec = [e0.astype(F32), e1.astype(F32), r0, r1, w0, w1]
    out = jnp.zeros(lg.shape, F32)
    for idx, val in enumerate(rec):
        out = jnp.where(lane == idx, val, out)
    rt_ref[...] = out


def _router(x, mod4, layer, w_r, b_r, tri):
    tm = ROUTER_TM
    return pl.pallas_call(
        _router_body,
        grid=(NTOK // tm,),
        in_specs=[pl.BlockSpec((tm, D_MODEL), lambda i: (i, 0)),
                  _mod_spec(layer, 4, tm), _mod_spec(layer, 3, tm),
                  pl.BlockSpec((D_MODEL, 128), lambda i: (0, 0)),
                  pl.BlockSpec((1, 128), lambda i: (0, 0)),
                  pl.BlockSpec((tm, tm), lambda i: (0, 0))],
        out_specs=[pl.BlockSpec((tm, 128), lambda i: (i, 0)),
                   pl.BlockSpec((8, 128), lambda i: (0, 0)),
                   pl.BlockSpec((tm, D_MODEL), lambda i: (i, 0))],
        out_shape=[jax.ShapeDtypeStruct((NTOK, 128), F32), jax.ShapeDtypeStruct((8, 128), F32),
                   jax.ShapeDtypeStruct((NTOK, D_MODEL), BF16)],
        scratch_shapes=[pltpu.VMEM((1, 128), F32)],
        compiler_params=_params(("arbitrary",), 40),
        name="moe_router",
    )(x, mod4, mod4, w_r, b_r, tri)


def _expert_body(te_ref, tv_ref, x_ref, wg_ref, wu_ref, wd_ref, o_ref, wg_s, wu_s, wd_s):
    t = pl.program_id(0)
    prev = te_ref[jnp.maximum(t - 1, 0)]

    @pl.when((t == 0) | (te_ref[t] != prev))
    def _():
        wg_s[...] = wg_ref[...].astype(BF16)
        wu_s[...] = wu_ref[...].astype(BF16)
        wd_s[...] = wd_ref[...].astype(BF16)

    @pl.when(tv_ref[t] > 0)
    def _():
        x = x_ref[...]
        g = _dot(x, wg_s[...])
        u = _dot(x, wu_s[...])
        act = (g / (1.0 + jnp.exp(-g))) * u
        o_ref[...] = _dot(act.astype(BF16), wd_s[...])

    @pl.when(tv_ref[t] == 0)
    def _():
        o_ref[...] = jnp.zeros(o_ref.shape, o_ref.dtype)


def _experts(xs, tile_e, tile_v, w_gate, w_up, w_down, layer):
    grid_spec = pltpu.PrefetchScalarGridSpec(
        num_scalar_prefetch=2,
        grid=(MOE_TILES,),
        in_specs=[
            pl.BlockSpec((MOE_TM, D_MODEL), lambda t, te, tv: (t, 0)),
            pl.BlockSpec((None, None, D_MODEL, D_EXPERT), lambda t, te, tv: (layer, te[t], 0, 0)),
            pl.BlockSpec((None, None, D_MODEL, D_EXPERT), lambda t, te, tv: (layer, te[t], 0, 0)),
            pl.BlockSpec((None, None, D_EXPERT, D_MODEL), lambda t, te, tv: (layer, te[t], 0, 0)),
        ],
        out_specs=pl.BlockSpec((MOE_TM, D_MODEL), lambda t, te, tv: (t, 0)),
        scratch_shapes=[pltpu.VMEM((D_MODEL, D_EXPERT), BF16), pltpu.VMEM((D_MODEL, D_EXPERT), BF16),
                        pltpu.VMEM((D_EXPERT, D_MODEL), BF16)],
    )
    return pl.pallas_call(
        _expert_body,
        grid_spec=grid_spec,
        out_shape=jax.ShapeDtypeStruct((MOE_ROWS, D_MODEL), F32),
        compiler_params=_params(("arbitrary",), 52),
        name="moe_experts",
    )(tile_e, tile_v, xs, w_gate, w_up, w_down)


def _layout(route, counts):
    counts = counts[0, :N_EXPERTS].astype(jnp.int32)
    padded = ((counts + MOE_TM - 1) // MOE_TM) * MOE_TM
    pend = jnp.cumsum(padded)
    pstart = pend - padded
    experts = jnp.arange(N_EXPERTS, dtype=jnp.int32)[None, :]

    def dest_of(e_lane, r_lane):
        e = route[:, e_lane].astype(jnp.int32)
        base = jnp.sum(jnp.where(e[:, None] == experts, pstart[None, :], 0), axis=1)
        return base + route[:, r_lane].astype(jnp.int32)

    dest = jnp.concatenate([dest_of(RT_E0, RT_R0), dest_of(RT_E1, RT_R1)])
    tok = jnp.arange(NTOK, dtype=jnp.int32)
    src = jnp.zeros((MOE_ROWS,), jnp.int32).at[dest].set(jnp.concatenate([tok, tok]))
    tile_start = jnp.arange(MOE_TILES, dtype=jnp.int32) * MOE_TM
    tile_e = jnp.minimum(jnp.sum((tile_start[:, None] >= pend[None, :]).astype(jnp.int32), axis=1), N_EXPERTS - 1)
    tile_v = (tile_start < pend[-1]).astype(jnp.int32)
    return dest, src, tile_e.astype(jnp.int32), tile_v


def _moe(x, mod4, layer, w_r, b_r, tri, w_gate, w_up, w_down, ln_g, ln_b):
    route, counts, h2 = _router(x, mod4, layer, w_r, b_r, tri)
    dest, src, tile_e, tile_v = _layout(route, counts)
    xs = jnp.take(h2, src, axis=0)
    rows = _experts(xs, tile_e, tile_v, w_gate, w_up, w_down, layer)
    rows2 = jnp.take(rows, dest, axis=0)
    return _res_ln(x, rows2, route, mod4, layer, 5, ln_g, ln_b)


def kernel(x_prompt, x_sample, state_gla, cache_diff_k, cache_diff_v, cache_mla_ckv, cache_mla_krope, c, c_ctx, ada_w, ada_b, ln_g, ln_b, ab_w_in, gla_w_gate2, gla_b_gate2, gla_norm_g, diff_lambda, diff_norm_g, ab_w_out, mla_w_in, mla_q_norm_g, mla_w_uq, mla_kv_norm_g, mla_w_ukv, mla_w_out, moe_w_rg, moe_b_rg, moe_w_re, moe_b_re, moe_w_gate, moe_w_up, moe_w_down):
    x = jnp.concatenate([x_prompt.reshape(NCTX, D_MODEL), x_sample.reshape(NLAT, D_MODEL)], axis=0)
    cond = jnp.concatenate([c_ctx[None, :], c, jnp.zeros((N_COND - 1 - DEC_BATCH, D_MODEL), F32)], axis=0)
    mod4 = _ada_mod(cond, ada_w, ada_b).reshape(DEPTH, N_COND, 1, 6 * D_MODEL)
    cos_t, sin_t = _rope_tables()
    levf = jnp.asarray(_gla_level_table(False))
    levb = jnp.asarray(_gla_level_table(True))
    tri = jnp.asarray(np.tril(np.ones((ROUTER_TM, ROUTER_TM), np.float32), -1), BF16)
    cache_k = cache_diff_k.reshape(DEC_BATCH, -1, PAST_LEN, B_HEADS * 2 * B_DH)
    cache_v = cache_diff_v.reshape(DEC_BATCH, -1, PAST_LEN, B_HEADS * B_DV)

    new_gla, new_dk, new_dv, new_ckv, new_kr = [], [], [], [], []
    for layer in range(DEPTH):
        i = layer // 2
        if layer % 2 == 0:
            w_d = jnp.concatenate([ab_w_in[i, :, AB_D0:], ab_w_in[i, :, AB_GG0:AB_D0],
                                   jnp.zeros((D_MODEL, 128 - 2 * A_GATE_RANK), F32)], axis=1)[None]
            pa = _mod_matmul(x, mod4, layer, ab_w_in, i, PA_N, 512, "ab_in_gla")
            pd = _mod_matmul(x, mod4, layer, w_d, 0, PD_N, 640, "ab_in_diff")
            wg = jnp.zeros((A_HEADS, 128, 2 * A_DK), F32)
            for j in range(2):
                blk = gla_w_gate2[i, j].reshape(A_GATE_RANK, A_HEADS, A_DK).transpose(1, 0, 2)
                wg = wg.at[:, j * A_GATE_RANK:(j + 1) * A_GATE_RANK, j * A_DK:(j + 1) * A_DK].set(blk)
            bg = gla_b_gate2[i].reshape(2, A_HEADS, A_DK).transpose(1, 0, 2).reshape(A_HEADS, 1, 2 * A_DK)
            g_gla = gla_norm_g[i][None, :]
            o_gla_c, st = _gla(pa, pd, wg, bg, g_gla, levf, levb, 0, BATCH, SEQ, None, 0, True)
            (o_gla,) = _gla(pa, pd, wg, bg, g_gla, levf, levb, NCTX, DEC_BATCH, DEC_SEQ, state_gla, i, False,
                            prev=o_gla_c)
            lv = diff_lambda[i]
            lam_init = 0.8 - 0.6 * math.exp(-0.3 * layer)
            lam = (jnp.exp(jnp.sum(lv[0] * lv[1])) - jnp.exp(jnp.sum(lv[2] * lv[3])) + lam_init).reshape(1, 1)
            g_diff = diff_norm_g[i][None, :]
            o_diff_c = _diff_attn(pd, lam, g_diff, lam_init, 0, BATCH, SEQ)
            o_diff = _diff_attn(pd, lam, g_diff, lam_init, NCTX, DEC_BATCH, DEC_SEQ,
                                cache=(cache_k, cache_v, i), rope=(cos_t, sin_t), bq=DEC_SEQ, prev=o_diff_c)
            x = _proj_ln([o_gla, o_diff], ab_w_out, i, x, mod4, layer, 2,
                         ln_g[layer, 0][None, :], ln_b[layer, 0][None, :], "ab_out_ln")
            new_gla.append(st)
            new_dk.append(pd[:NCTX, PD_K:PD_V].reshape(BATCH, SEQ, B_HEADS, 2, B_DH))
            new_dv.append(pd[:NCTX, PD_V:PD_GG].reshape(BATCH, SEQ, B_HEADS, B_DV))
        else:
            c_in = C_Q_RANK + C_KV_RANK + C_ROPE
            pm = _mod_matmul(x, mod4, layer, mla_w_in, i, c_in, c_in, "mla_in", tm=512)
            w_uq = mla_w_uq[i].reshape(C_Q_RANK, C_HEADS, C_NOPE + C_ROPE)
            w_uq = jnp.concatenate([w_uq[:, :, :C_NOPE].reshape(C_Q_RANK, -1),
                                    w_uq[:, :, C_NOPE:].reshape(C_Q_RANK, -1)], axis=1)
            g_q = mla_q_norm_g[i][None, :]
            g_kv = mla_kv_norm_g[i][None, :]
            qn = _rms_matmul(pm, 0, C_Q_RANK, g_q, w_uq, 0, C_HEADS * C_NOPE, 512, 1024, BF16, "mla_q_nope")
            qr = _rms_matmul(pm, 0, C_Q_RANK, g_q, w_uq, C_HEADS * C_NOPE // 512, C_HEADS * C_ROPE, 512, 1024, F32,
                             "mla_q_rope")
            kv, ckv_n = _rms_matmul(pm, C_Q_RANK // C_KV_RANK, C_KV_RANK, g_kv, mla_w_ukv[i], 0,
                                    C_HEADS * (C_NOPE + C_DV), 1024, 1024, BF16, "mla_kv", emit_norm=True)
            kv_c = _rms_matmul(cache_mla_ckv[:, i].reshape(DEC_BATCH * PAST_LEN, C_KV_RANK), 0, C_KV_RANK, g_kv,
                               mla_w_ukv[i], 0, C_HEADS * (C_NOPE + C_DV), 1024, 1024, BF16, "mla_kv_cache",
                               norm=False)
            krope = pm[:, C_Q_RANK + C_KV_RANK:]
            kr2 = jnp.concatenate([krope, krope], axis=1)
            krc = cache_mla_krope[:, i].reshape(DEC_BATCH * PAST_LEN, C_ROPE)
            krc2 = jnp.concatenate([krc, krc], axis=1)
            o_c = _mla_attn(qn, qr, kv, kr2, 0, BATCH, SEQ)
            o_mla = _mla_attn(qn, qr, kv, kr2, NCTX, DEC_BATCH, DEC_SEQ, cache=(kv_c, krc2), rope=(cos_t, sin_t),
                              bq=DEC_SEQ, prev=o_c)
            x = _proj_ln([o_mla], mla_w_out, i, x, mod4, layer, 2,
                         ln_g[layer, 0][None, :], ln_b[layer, 0][None, :], "mla_out_ln")
            new_ckv.append(ckv_n[:NCTX].reshape(BATCH, SEQ, C_KV_RANK))
            new_kr.append(krope[:NCTX].reshape(BATCH, SEQ, C_ROPE))
        w_r = jnp.concatenate([moe_w_rg[layer], moe_w_re[layer],
                               jnp.zeros((D_MODEL, 128 - N_GROUPS - N_EXPERTS), F32)], axis=1)
        b_r = jnp.concatenate([moe_b_rg[layer], moe_b_re[layer],
                               jnp.zeros((128 - N_GROUPS - N_EXPERTS,), F32)])[None, :]
        x = _moe(x, mod4, layer, w_r, b_r, tri, moe_w_gate, moe_w_up, moe_w_down,
                 ln_g[layer, 1][None, :], ln_b[layer, 1][None, :])

    return (x[:NCTX].reshape(BATCH, SEQ, D_MODEL),
            x[NCTX:].reshape(DEC_BATCH, DEC_SEQ, D_MODEL),
            jnp.stack(new_gla, axis=1), jnp.stack(new_dk, axis=1), jnp.stack(new_dv, axis=1),
            jnp.stack(new_ckv, axis=1), jnp.stack(new_kr, axis=1))
```

```python
import functools
import math

import numpy as np
import jax
import jax.numpy as jnp
from jax import lax
from jax.experimental import pallas as pl
from jax.experimental.pallas import tpu as pltpu

F32 = jnp.float32
BF16 = jnp.bfloat16

D_MODEL = 2048
BATCH = 16
SEQ = 256
DEPTH = 4
DEC_BATCH = 4
DEC_SEQ = 1024
PAST_LEN = 256
GRID_W = 64
A_HEADS = 4
A_DK = 128
A_DV = 256
A_GATE_RANK = 16
A_TAU = 16.0
B_HEADS = 8
B_DH = 64
B_DV = 2 * B_DH
C_HEADS = 16
C_Q_RANK = 512
C_KV_RANK = 256
C_NOPE = 128
C_ROPE = 64
C_DV = 128
N_GROUPS = 4
EXPERTS_PER_GROUP = 4
N_EXPERTS = N_GROUPS * EXPERTS_PER_GROUP
EXPERT_TOP_K = 2
D_EXPERT = 512
ROPE_THETA = 10000.0
LN_EPS = 1e-5
RMS_EPS = 1e-6
DEEPNORM_ALPHA = (2.0 * DEPTH) ** 0.25

NCTX = BATCH * SEQ
NLAT = DEC_BATCH * DEC_SEQ
NTOK = NCTX + NLAT
N_COND = 8
GLA_CHUNK = 256
GLA_LEVELS = 8
MOE_TM = 512
LOG2E = 1.4426950408889634
MOE_ROWS = NTOK * EXPERT_TOP_K + N_EXPERTS * MOE_TM
MOE_TILES = MOE_ROWS // MOE_TM

PA_Q, PA_K, PA_V, PA_R = 0, A_HEADS * A_DK, 2 * A_HEADS * A_DK, 2 * A_HEADS * A_DK + A_HEADS * A_DV
PA_N = PA_R + A_HEADS * A_DV
AB_GG0 = PA_N
AB_D0 = PA_N + 2 * A_GATE_RANK
PD_Q, PD_K, PD_V = 0, B_HEADS * 2 * B_DH, 2 * B_HEADS * 2 * B_DH
PD_GG = PD_V + B_HEADS * B_DV
PD_N = PD_GG + 128


def _params(sem, vmem_mb):
    return pltpu.CompilerParams(dimension_semantics=sem, vmem_limit_bytes=vmem_mb * 2 ** 20)


def _cond_row(i, tm):
    return jnp.where(i * tm < NCTX, 0, 1 + (i * tm - NCTX) // DEC_SEQ)


def _dot(a, b):
    return jnp.dot(a, b, preferred_element_type=F32)


def _dot_nt(a, b):
    return lax.dot_general(a, b, (((1,), (1,)), ((), ())), preferred_element_type=F32)


def _dot_tn(a, b):
    return lax.dot_general(a, b, (((0,), (0,)), ((), ())), preferred_element_type=F32)


def _ada_body(c_ref, w_ref, b_ref, o_ref):
    c = c_ref[...]
    a = (c / (1.0 + jnp.exp(-c))).astype(BF16)
    o_ref[0] = _dot(a, w_ref[0].astype(BF16)) + b_ref[0]


def _ada_mod(cond, ada_w, ada_b):
    tn = 1024
    return pl.pallas_call(
        _ada_body,
        grid=(DEPTH, 6 * D_MODEL // tn),
        in_specs=[pl.BlockSpec((N_COND, D_MODEL), lambda l, j: (0, 0)),
                  pl.BlockSpec((1, D_MODEL, tn), lambda l, j: (l, 0, j)),
                  pl.BlockSpec((1, 1, tn), lambda l, j: (l, 0, j))],
        out_specs=pl.BlockSpec((1, N_COND, tn), lambda l, j: (l, 0, j)),
        out_shape=jax.ShapeDtypeStruct((DEPTH, N_COND, 6 * D_MODEL), F32),
        compiler_params=_params(("arbitrary", "arbitrary"), 40),
        name="ada_mod",
    )(cond, ada_w, ada_b.reshape(DEPTH, 1, 6 * D_MODEL))


def _mod_spec(layer, chunk, tm):
    return pl.BlockSpec((1, 1, 1, D_MODEL), lambda i, *_: (layer, _cond_row(i, tm), 0, chunk))


def _mod_mm_body(x_ref, sc_ref, sh_ref, w_ref, o_ref, h_ref):
    @pl.when(pl.program_id(1) == 0)
    def _():
        h_ref[...] = (x_ref[...] * (1.0 + sc_ref[0, 0]) + sh_ref[0, 0]).astype(BF16)

    o_ref[...] = _dot(h_ref[...], w_ref[...].astype(BF16)).astype(o_ref.dtype)


def _mod_matmul(x, mod4, layer, w, w_layer, n_out, tn, name, tm=1024, out_dtype=F32):
    return pl.pallas_call(
        _mod_mm_body,
        grid=(NTOK // tm, n_out // tn),
        in_specs=[pl.BlockSpec((tm, D_MODEL), lambda i, j: (i, 0)),
                  _mod_spec(layer, 1, tm), _mod_spec(layer, 0, tm),
                  pl.BlockSpec((None, D_MODEL, tn), lambda i, j: (w_layer, 0, j))],
        out_specs=pl.BlockSpec((tm, tn), lambda i, j: (i, j)),
        out_shape=jax.ShapeDtypeStruct((NTOK, n_out), out_dtype),
        scratch_shapes=[pltpu.VMEM((tm, D_MODEL), BF16)],
        compiler_params=_params(("arbitrary", "arbitrary"), 48),
        name=name,
    )(x, mod4, mod4, w)


def _rms_mm_body(x_ref, g_ref, w_ref, *rest, norm, emit_norm):
    if emit_norm:
        o_ref, n_ref, h_ref = rest
    else:
        o_ref, h_ref = rest

    @pl.when(pl.program_id(1) == 0)
    def _():
        x = x_ref[...]
        if norm:
            x = x * lax.rsqrt(jnp.mean(x * x, axis=-1, keepdims=True) + RMS_EPS) * g_ref[...]
        h_ref[...] = x.astype(BF16)
        if emit_norm:
            n_ref[...] = x

    o_ref[...] = _dot(h_ref[...], w_ref[...].astype(BF16)).astype(o_ref.dtype)


def _rms_matmul(x, col_block, k, g, w, col0_blocks, n_out, tn, tm, out_dtype, name, norm=True, emit_norm=False):
    rows = x.shape[0]
    out_shape = [jax.ShapeDtypeStruct((rows, n_out), out_dtype)]
    out_specs = [pl.BlockSpec((tm, tn), lambda i, j: (i, j))]
    if emit_norm:
        out_shape.append(jax.ShapeDtypeStruct((rows, k), F32))
        out_specs.append(pl.BlockSpec((tm, k), lambda i, j: (i, 0)))
    res = pl.pallas_call(
        functools.partial(_rms_mm_body, norm=norm, emit_norm=emit_norm),
        grid=(rows // tm, n_out // tn),
        in_specs=[pl.BlockSpec((tm, k), lambda i, j: (i, col_block)),
                  pl.BlockSpec((1, k), lambda i, j: (0, 0)),
                  pl.BlockSpec((k, tn), lambda i, j: (0, col0_blocks + j))],
        out_specs=out_specs,
        out_shape=out_shape,
        scratch_shapes=[pltpu.VMEM((tm, k), BF16)],
        compiler_params=_params(("arbitrary", "arbitrary"), 40),
        name=name,
    )(x, g, w)
    return res if emit_norm else res[0]


def _layer_norm(z, g, b):
    mu = jnp.mean(z, axis=-1, keepdims=True)
    zc = z - mu
    var = jnp.mean(zc * zc, axis=-1, keepdims=True)
    return zc * lax.rsqrt(var + LN_EPS) * g + b


LN_ROWS = 256


def _proj_ln_body(*refs, n_lhs, n_j, tm, tn):
    lhs = refs[:n_lhs]
    ws = refs[n_lhs:2 * n_lhs]
    x_ref, gate_ref, g_ref, b_ref, o_ref = refs[2 * n_lhs:]
    j = pl.program_id(1)
    y = _dot(lhs[0][...], ws[0][...].astype(BF16))
    for a, w in zip(lhs[1:], ws[1:]):
        y = y + _dot(a[...], w[...].astype(BF16))
    o_ref[:, pl.ds(pl.multiple_of(j * tn, tn), tn)] = y

    @pl.when(j == n_j - 1)
    def _():
        def body(r, carry):
            rows = pl.ds(pl.multiple_of(r * LN_ROWS, LN_ROWS), LN_ROWS)
            z = DEEPNORM_ALPHA * x_ref[rows, :] + gate_ref[0, 0] * o_ref[rows, :]
            o_ref[rows, :] = _layer_norm(z, g_ref[...], b_ref[...])
            return carry
        lax.fori_loop(0, tm // LN_ROWS, body, 0)


def _proj_ln(lhs_list, w, w_layer, x, mod4, layer, gate_chunk, ln_g, ln_b, name, tm=1024, tn=512):
    n_lhs = len(lhs_list)
    kk = lhs_list[0].shape[1]
    n_j = D_MODEL // tn
    in_specs = [pl.BlockSpec((tm, kk), lambda i, j: (i, 0)) for _ in lhs_list]
    in_specs += [pl.BlockSpec((None, kk, tn), functools.partial(lambda i, j, r: (w_layer, r, j), r=r))
                 for r in range(n_lhs)]
    in_specs += [pl.BlockSpec((tm, D_MODEL), lambda i, j: (i, 0)),
                 _mod_spec(layer, gate_chunk, tm),
                 pl.BlockSpec((1, D_MODEL), lambda i, j: (0, 0)),
                 pl.BlockSpec((1, D_MODEL), lambda i, j: (0, 0))]
    return pl.pallas_call(
        functools.partial(_proj_ln_body, n_lhs=n_lhs, n_j=n_j, tm=tm, tn=tn),
        grid=(NTOK // tm, n_j),
        in_specs=in_specs,
        out_specs=pl.BlockSpec((tm, D_MODEL), lambda i, j: (i, 0)),
        out_shape=jax.ShapeDtypeStruct((NTOK, D_MODEL), F32),
        compiler_params=_params(("arbitrary", "arbitrary"), 56),
        name=name,
    )(*lhs_list, *([w] * n_lhs), x, mod4, ln_g, ln_b)


COMBINE_TM = 256


def _row_copy(src, i, dst, j, sem):
    return pltpu.make_async_copy(src.at[pl.ds(i, 1), :], dst.at[pl.ds(j, 1), :], sem)


def _combine_ln_body(dest_ref, x_ref, rt_ref, gate_ref, g_ref, b_ref, rows_hbm, o_ref, buf, sem):
    tm = COMBINE_TM
    i = pl.program_id(0)
    n_i = pl.num_programs(0)

    def fetch(step, slot):
        def body(t, carry):
            for k in range(EXPERT_TOP_K):
                d = dest_ref[k * NTOK + step * tm + t]
                _row_copy(rows_hbm, d, buf.at[slot, k], t, sem.at[slot]).start()
            return carry
        lax.fori_loop(0, tm, body, 0, unroll=8)

    @pl.when(i == 0)
    def _():
        fetch(0, 0)

    @pl.when(i + 1 < n_i)
    def _():
        fetch(i + 1, (i + 1) % 2)

    slot = i % 2

    def drain(t, carry):
        for k in range(EXPERT_TOP_K):
            _row_copy(rows_hbm, 0, buf.at[slot, k], 0, sem.at[slot]).wait()
        return carry
    lax.fori_loop(0, tm, drain, 0, unroll=8)

    rt = rt_ref[...]
    y = rt[:, RT_W0:RT_W0 + 1] * buf[slot, 0] + rt[:, RT_W1:RT_W1 + 1] * buf[slot, 1]
    z = DEEPNORM_ALPHA * x_ref[...] + gate_ref[0, 0] * y
    o_ref[...] = _layer_norm(z, g_ref[...], b_ref[...])


def _combine_ln(x, rows, dest, route, mod4, layer, gate_chunk, ln_g, ln_b):
    tm = COMBINE_TM
    row = pl.BlockSpec((tm, D_MODEL), lambda i, d: (i, 0))
    vec = pl.BlockSpec((1, D_MODEL), lambda i, d: (0, 0))
    grid_spec = pltpu.PrefetchScalarGridSpec(
        num_scalar_prefetch=1,
        grid=(NTOK // tm,),
        in_specs=[row, pl.BlockSpec((tm, 128), lambda i, d: (i, 0)),
                  _mod_spec(layer, gate_chunk, tm), vec, vec,
                  pl.BlockSpec(memory_space=pl.ANY)],
        out_specs=row,
        scratch_shapes=[pltpu.VMEM((2, EXPERT_TOP_K, tm, D_MODEL), F32), pltpu.SemaphoreType.DMA((2,))],
    )
    return pl.pallas_call(
        _combine_ln_body,
        grid_spec=grid_spec,
        out_shape=jax.ShapeDtypeStruct((NTOK, D_MODEL), F32),
        compiler_params=_params(("arbitrary",), 40),
        name="moe_combine_ln",
    )(dest, x, route, mod4, ln_g, ln_b, rows)


def _gla_level_table(reverse):
    t = np.arange(GLA_CHUNK)[:, None]
    s = np.arange(GLA_CHUNK)[None, :]
    x = t ^ s
    lev = np.where(x > 0, np.floor(np.log2(np.maximum(x, 1))).astype(np.int32), -1)
    live = (t < s) if reverse else (t > s)
    return np.where(live | (t == s), lev, -2).astype(np.int32)


def _anchor_rows(u, level, reverse):
    c = u.shape[0]
    n = c // 8
    u3 = u.reshape(n, 8, 128)
    half = 1 << level
    if level < 3:
        blk = 2 * half
        sub = lax.broadcasted_iota(jnp.int32, (n, 8, 128), 1)
        out = None
        for j in range(8 // blk):
            r = j * blk + (half if reverse else half - 1)
            piece = jnp.broadcast_to(u3[:, r:r + 1, :], (n, 8, 128))
            out = piece if out is None else jnp.where(sub >= j * blk, piece, out)
        return out.reshape(c, 128)
    m = 1 << (level - 2)
    r = 0 if reverse else 7
    e = jnp.broadcast_to(u3[:, r:r + 1, :], (n, 8, 128)).reshape(n // m, m, 8, 128)
    idx = m // 2 if reverse else m // 2 - 1
    return jnp.broadcast_to(e[:, idx:idx + 1], (n // m, m, 8, 128)).reshape(c, 128)


def _gla_intra(q, k, la, lev, reverse):
    rowk = lax.broadcasted_iota(jnp.int32, (GLA_CHUNK, 128), 0)
    scores = jnp.where(lev == -1, _dot_nt(q.astype(BF16), k.astype(BF16)), 0.0)
    u = la
    for level in range(GLA_LEVELS):
        anchor = _anchor_rows(u, level, reverse)
        qa = (q * jnp.exp(u)).astype(BF16)
        ka = (k * jnp.exp(jnp.minimum(anchor - u, 0.0))).astype(BF16)
        scores = jnp.where(lev == level, _dot_nt(qa, ka), scores)
        query_side = ((rowk >> level) & 1) == (0 if reverse else 1)
        u = u + jnp.where(query_side, anchor, 0.0)
    return scores, u


def _gla_body(*refs, t_len, has_s0, emit_state, has_prev):
    it = iter(refs)
    q_ref, k_ref, v_ref, r_ref, gg_ref, wg_ref, bg_ref, g_ref, levf_ref, levb_ref = [next(it) for _ in range(10)]
    s0_ref = next(it) if has_s0 else None
    if has_prev:
        next(it)
    o_ref = next(it)
    st_ref = next(it) if emit_state else None
    acc_ref, sf_ref, sb_ref, la_ref = [next(it) for _ in range(4)]
    c = GLA_CHUNK
    n_ch = t_len // c

    gl = _dot(gg_ref[...].astype(BF16), wg_ref[...].astype(BF16)) + bg_ref[...]
    la_ref[...] = (jnp.minimum(gl, 0.0) - jnp.log1p(jnp.exp(-jnp.abs(gl)))) * (1.0 / A_TAU)
    if has_s0:
        sf_ref[...] = s0_ref[0]
        sb_ref[...] = s0_ref[1]

    eye = (lax.broadcasted_iota(jnp.int32, (A_DK, A_DK), 0) == lax.broadcasted_iota(jnp.int32, (A_DK, A_DK), 1))

    def chunk_step(ci, reverse, with_state):
        rows = pl.ds(ci * c if isinstance(ci, int) else pl.multiple_of(ci * c, c), c)
        q = q_ref[rows, :] * (A_DK ** -0.5)
        k = k_ref[rows, :]
        v = v_ref[rows, :].astype(BF16)
        la = la_ref[rows, A_DK:2 * A_DK] if reverse else la_ref[rows, 0:A_DK]
        s_ref = sb_ref if reverse else sf_ref
        scores, cum = _gla_intra(q, k, la, (levb_ref if reverse else levf_ref)[...], reverse)
        o = _dot(scores.astype(BF16), v)
        if with_state:
            o = o + _dot((q * jnp.exp(cum)).astype(BF16), s_ref[...].astype(BF16))
        tot = jnp.broadcast_to(cum[0:1, :] if reverse else cum[c - 1:c, :], (c, A_DK))
        delta = _dot_tn((k * jnp.exp(tot - cum)).astype(BF16), v)
        if with_state:
            decay = jnp.sum(jnp.where(eye, jnp.exp(tot[0:A_DK, :]), 0.0), axis=1, keepdims=True)
            s_ref[...] = decay * s_ref[...] + delta
        else:
            s_ref[...] = delta
        if not reverse:
            acc_ref[rows, :] = o
        else:
            y = acc_ref[rows, :] + o
            y = y * lax.rsqrt(jnp.mean(y * y, axis=-1, keepdims=True) + RMS_EPS) * g_ref[...]
            r = r_ref[rows, :]
            o_ref[rows, :] = (y * (r / (1.0 + jnp.exp(-r)))).astype(o_ref.dtype)

    for reverse in (False, True):
        first = (n_ch - 1) if reverse else 0
        chunk_step(first, reverse, has_s0)
        if n_ch > 1:
            def body(i, carry, reverse=reverse):
                chunk_step((n_ch - 1 - i) if reverse else i, reverse, True)
                return carry
            lax.fori_loop(1, n_ch, body, 0)

    if emit_state:
        st_ref[0] = sf_ref[...]
        st_ref[1] = sb_ref[...]


def _alias_prev(in_specs, args, prev):
    if prev is None:
        return {}
    in_specs.append(pl.BlockSpec(memory_space=pl.ANY))
    args.append(prev)
    return {len(args) - 1: 0}


def _gla(pa, pd, wg, bg, gla_g, levf, levb, row0, n_seq, t_len, s0, s0_layer, emit_state, prev=None):
    rb = row0 // t_len
    kb, vb, rbk = PA_K // A_DK, PA_V // A_DV, PA_R // A_DV
    in_specs = [
        pl.BlockSpec((t_len, A_DK), lambda n, h: (rb + n, h)),
        pl.BlockSpec((t_len, A_DK), lambda n, h: (rb + n, kb + h)),
        pl.BlockSpec((t_len, A_DV), lambda n, h: (rb + n, vb + h)),
        pl.BlockSpec((t_len, A_DV), lambda n, h: (rb + n, rbk + h)),
        pl.BlockSpec((t_len, 128), lambda n, h: (rb + n, PD_GG // 128)),
        pl.BlockSpec((None, 128, 2 * A_DK), lambda n, h: (h, 0, 0)),
        pl.BlockSpec((None, 1, 2 * A_DK), lambda n, h: (h, 0, 0)),
        pl.BlockSpec((1, A_DV), lambda n, h: (0, 0)),
        pl.BlockSpec((GLA_CHUNK, GLA_CHUNK), lambda n, h: (0, 0)),
        pl.BlockSpec((GLA_CHUNK, GLA_CHUNK), lambda n, h: (0, 0)),
    ]
    args = [pa, pa, pa, pa, pd, wg, bg, gla_g, levf, levb]
    if s0 is not None:
        in_specs.append(pl.BlockSpec((None, None, 2, None, A_DK, A_DV), lambda n, h: (n, s0_layer, 0, h, 0, 0)))
        args.append(s0)
    aliases = _alias_prev(in_specs, args, prev)
    out_shape = [jax.ShapeDtypeStruct((NTOK, A_HEADS * A_DV), BF16)]
    out_specs = [pl.BlockSpec((t_len, A_DV), lambda n, h: (rb + n, h))]
    if emit_state:
        out_shape.append(jax.ShapeDtypeStruct((n_seq, 2, A_HEADS, A_DK, A_DV), F32))
        out_specs.append(pl.BlockSpec((None, 2, None, A_DK, A_DV), lambda n, h: (n, 0, h, 0, 0)))
    res = pl.pallas_call(
        functools.partial(_gla_body, t_len=t_len, has_s0=s0 is not None, emit_state=emit_state,
                          has_prev=prev is not None),
        grid=(n_seq, A_HEADS),
        in_specs=in_specs,
        out_specs=out_specs,
        out_shape=out_shape,
        input_output_aliases=aliases,
        scratch_shapes=[pltpu.VMEM((t_len, A_DV), F32), pltpu.VMEM((A_DK, A_DV), F32),
                        pltpu.VMEM((A_DK, A_DV), F32), pltpu.VMEM((t_len, 2 * A_DK), F32)],
        compiler_params=_params(("arbitrary", "arbitrary"), 40),
        name="gla_ctx" if s0 is None else "gla_lat",
    )(*args)
    return res


def _rope(x, cos, sin):
    lane = lax.broadcasted_iota(jnp.int32, x.shape, 1)
    partner = jnp.where((lane & 32) == 0, pltpu.roll(x, 96, 1), pltpu.roll(x, 32, 1))
    return x * cos + partner * sin


def _rope_tables():
    n_rows = DEC_SEQ // GRID_W
    rows = jnp.repeat(jnp.arange(n_rows, dtype=F32), GRID_W)
    cols = jnp.tile(jnp.arange(GRID_W, dtype=F32), n_rows)
    quarter = C_ROPE // 4
    freqs = ROPE_THETA ** (-jnp.arange(quarter, dtype=F32) / quarter)
    ang = jnp.concatenate([rows[:, None] * freqs, cols[:, None] * freqs], axis=-1)
    cos, sin = jnp.cos(ang), jnp.sin(ang)
    cos_t = jnp.tile(cos, (1, 4))
    sin_t = jnp.tile(jnp.concatenate([-sin, sin], axis=-1), (1, 2))
    return cos_t, sin_t


def _softmax_parts(scores, scale):
    m = functools.reduce(jnp.maximum, [jnp.max(s, axis=-1, keepdims=True) for s in scores])
    ps = [jnp.exp2((s - m) * (scale * LOG2E)) for s in scores]
    inv = 1.0 / functools.reduce(jnp.add, [jnp.sum(p, axis=-1, keepdims=True) for p in ps])
    return ps, inv


def _pv(ps, values):
    return functools.reduce(jnp.add, [_dot(p.astype(BF16), v) for p, v in zip(ps, values)])


def _diff_body(*refs, latent, lam_init, has_prev):
    it = iter(refs)
    lam_ref, q_ref, k_ref, v_ref, g_ref = [next(it) for _ in range(5)]
    if latent:
        kc_ref, vc_ref, cq_ref, sq_ref, ck_ref, sk_ref = [next(it) for _ in range(6)]
    if has_prev:
        next(it)
    o_ref = next(it)
    k_s, v_s = next(it), next(it)
    lam = lam_ref[0, 0]

    @pl.when(pl.program_id(2) == 0)
    def _():
        k = k_ref[...]
        if latent:
            k = _rope(k, ck_ref[...], sk_ref[...])
        k_s[...] = k.astype(BF16)
        v_s[...] = v_ref[...].astype(BF16)

    q = q_ref[...]
    keys, values = [k_s[...]], [v_s[...]]
    if latent:
        q = _rope(q, cq_ref[...], sq_ref[...])
        keys.insert(0, kc_ref[...].astype(BF16))
        values.insert(0, vc_ref[...].astype(BF16))
    lane = lax.broadcasted_iota(jnp.int32, q.shape, 1)
    q1 = jnp.where(lane < B_DH, q, 0.0).astype(BF16)
    q2 = jnp.where(lane >= B_DH, q, 0.0).astype(BF16)
    scale = B_DH ** -0.5
    p1, inv1 = _softmax_parts([_dot_nt(q1, kk) for kk in keys], scale)
    p2, inv2 = _softmax_parts([_dot_nt(q2, kk) for kk in keys], scale)
    o = _pv(p1, values) * inv1 - _pv(p2, values) * (lam * inv2)
    o = o * lax.rsqrt(jnp.mean(o * o, axis=-1, keepdims=True) + RMS_EPS) * g_ref[...] * (1.0 - lam_init)
    o_ref[...] = o.astype(o_ref.dtype)


def _diff_attn(pd, lam, diff_g, lam_init, row0, n_seq, t_len, cache=None, rope=None, bq=256, prev=None):
    latent = cache is not None
    rb = row0 // t_len
    qb0 = row0 // bq
    nqb = t_len // bq
    in_specs = [
        pl.BlockSpec(memory_space=pltpu.SMEM),
        pl.BlockSpec((bq, 128), lambda n, h, b: (qb0 + n * nqb + b, PD_Q // 128 + h)),
        pl.BlockSpec((t_len, 128), lambda n, h, b: (rb + n, PD_K // 128 + h)),
        pl.BlockSpec((t_len, 128), lambda n, h, b: (rb + n, PD_V // 128 + h)),
        pl.BlockSpec((1, B_DV), lambda n, h, b: (0, 0)),
    ]
    args = [lam, pd, pd, pd, diff_g]
    if latent:
        kc, vc, layer = cache
        cos_t, sin_t = rope
        in_specs += [
            pl.BlockSpec((None, None, PAST_LEN, 128), lambda n, h, b: (n, layer, 0, h)),
            pl.BlockSpec((None, None, PAST_LEN, 128), lambda n, h, b: (n, layer, 0, h)),
            pl.BlockSpec((bq, 128), lambda n, h, b: (b, 0)),
            pl.BlockSpec((bq, 128), lambda n, h, b: (b, 0)),
            pl.BlockSpec((t_len, 128), lambda n, h, b: (0, 0)),
            pl.BlockSpec((t_len, 128), lambda n, h, b: (0, 0)),
        ]
        args += [kc, vc, cos_t, sin_t, cos_t, sin_t]
    aliases = _alias_prev(in_specs, args, prev)
    return pl.pallas_call(
        functools.partial(_diff_body, latent=latent, lam_init=lam_init, has_prev=prev is not None),
        grid=(n_seq, B_HEADS, nqb),
        in_specs=in_specs,
        out_specs=pl.BlockSpec((bq, B_DV), lambda n, h, b: (qb0 + n * nqb + b, h)),
        out_shape=jax.ShapeDtypeStruct((NTOK, B_HEADS * B_DV), BF16),
        input_output_aliases=aliases,
        scratch_shapes=[pltpu.VMEM((t_len, 128), BF16), pltpu.VMEM((t_len, B_DV), BF16)],
        compiler_params=_params(("arbitrary", "arbitrary", "arbitrary"), 40),
        name="diff_lat" if latent else "diff_ctx",
    )(*args)


def _mla_body(*refs, latent, has_prev):
    it = iter(refs)
    qn_ref, qr_ref, kn_ref, v_ref, kr_ref = [next(it) for _ in range(5)]
    if latent:
        knc_ref, vc_ref, krc_ref, cq_ref, sq_ref, ck_ref, sk_ref = [next(it) for _ in range(7)]
    if has_prev:
        next(it)
    o_ref = next(it)
    kr_s = next(it)
    head = pl.program_id(1)

    @pl.when((head == 0) & (pl.program_id(2) == 0))
    def _():
        kr = kr_ref[...]
        if latent:
            kr = _rope(kr, ck_ref[...], sk_ref[...])
        kr_s[...] = kr.astype(BF16)

    segs = [(kn_ref[...], kr_s[...], v_ref[...])]
    if latent:
        segs.insert(0, (knc_ref[...], krc_ref[...].astype(BF16), vc_ref[...]))
    scale = (C_NOPE + C_ROPE) ** -0.5
    qr = qr_ref[...]
    if latent:
        qr = _rope(qr, cq_ref[...], sq_ref[...])
    lane = lax.broadcasted_iota(jnp.int32, qr.shape, 1)
    qr = jnp.where((lane >> 6) == head % 2, qr, 0.0).astype(BF16)
    qn = qn_ref[...]
    ps, inv = _softmax_parts([_dot_nt(qn, kn) + _dot_nt(qr, krr) for kn, krr, _ in segs], scale)
    o_ref[...] = (_pv(ps, [vv for _, _, vv in segs]) * inv).astype(o_ref.dtype)


def _mla_attn(qn, qr, kv, kr2, row0, n_seq, t_len, cache=None, rope=None, bq=256, prev=None):
    latent = cache is not None
    rb = row0 // t_len
    qb0 = row0 // bq
    nqb = t_len // bq
    in_specs = [
        pl.BlockSpec((bq, C_NOPE), lambda n, h, b: (qb0 + n * nqb + b, h)),
        pl.BlockSpec((bq, 128), lambda n, h, b: (qb0 + n * nqb + b, h // 2)),
        pl.BlockSpec((t_len, C_NOPE), lambda n, h, b: (rb + n, 2 * h)),
        pl.BlockSpec((t_len, C_DV), lambda n, h, b: (rb + n, 2 * h + 1)),
        pl.BlockSpec((t_len, 128), lambda n, h, b: (rb + n, 0)),
    ]
    args = [qn, qr, kv, kv, kr2]
    if latent:
        kvc, krc = cache
        cos_t, sin_t = rope
        in_specs += [
            pl.BlockSpec((PAST_LEN, C_NOPE), lambda n, h, b: (n, 2 * h)),
            pl.BlockSpec((PAST_LEN, C_DV), lambda n, h, b: (n, 2 * h + 1)),
            pl.BlockSpec((PAST_LEN, 128), lambda n, h, b: (n, 0)),
            pl.BlockSpec((bq, 128), lambda n, h, b: (b, 0)),
            pl.BlockSpec((bq, 128), lambda n, h, b: (b, 0)),
            pl.BlockSpec((t_len, 128), lambda n, h, b: (0, 0)),
            pl.BlockSpec((t_len, 128), lambda n, h, b: (0, 0)),
        ]
        args += [kvc, kvc, krc, cos_t, sin_t, cos_t, sin_t]
    aliases = _alias_prev(in_specs, args, prev)
    return pl.pallas_call(
        functools.partial(_mla_body, latent=latent, has_prev=prev is not None),
        grid=(n_seq, C_HEADS, nqb),
        in_specs=in_specs,
        out_specs=pl.BlockSpec((bq, C_DV), lambda n, h, b: (qb0 + n * nqb + b, h)),
        out_shape=jax.ShapeDtypeStruct((NTOK, C_HEADS * C_DV), BF16),
        input_output_aliases=aliases,
        scratch_shapes=[pltpu.VMEM((t_len, 128), BF16)],
        compiler_params=_params(("arbitrary", "arbitrary", "arbitrary"), 40),
        name="mla_lat" if latent else "mla_ctx",
    )(*args)


RT_E0, RT_E1, RT_R0, RT_R1, RT_W0, RT_W1 = range(6)
ROUTER_TM = 512


def _router_body(x_ref, sc_ref, sh_ref, w_ref, b_ref, tri_ref, rt_ref, cnt_ref, carry_ref):
    h = x_ref[...] * (1.0 + sc_ref[0, 0]) + sh_ref[0, 0]
    lg = jnp.dot(h, w_ref[...], precision=lax.Precision.HIGHEST, preferred_element_type=F32) + b_ref[...]

    @pl.when(pl.program_id(0) == 0)
    def _():
        carry_ref[...] = jnp.zeros(carry_ref.shape, F32)

    lane = lax.broadcasted_iota(jnp.int32, lg.shape, 1)
    ninf = -jnp.inf
    is_g = lane < N_GROUPS
    gl = jnp.where(is_g, lg, ninf)
    gmax = jnp.max(gl, axis=1, keepdims=True)
    g_sel = jnp.min(jnp.where(gl == gmax, lane, 128), axis=1, keepdims=True)
    g_w = 1.0 / jnp.sum(jnp.where(is_g, jnp.exp(lg - gmax), 0.0), axis=1, keepdims=True)
    el = jnp.where(((lane - N_GROUPS) >> 2) == g_sel, lg, ninf)
    m1 = jnp.max(el, axis=1, keepdims=True)
    i1 = jnp.min(jnp.where(el == m1, lane, 128), axis=1, keepdims=True)
    el2 = jnp.where(lane == i1, ninf, el)
    m2 = jnp.max(el2, axis=1, keepdims=True)
    i2 = jnp.min(jnp.where(el2 == m2, lane, 128), axis=1, keepdims=True)
    p2 = jnp.exp(m2 - m1)
    w0 = g_w / (1.0 + p2)
    w1 = w0 * p2
    e0 = i1 - N_GROUPS
    e1 = i2 - N_GROUPS
    hit0 = lane == e0
    hit1 = lane == e1
    onehot = jnp.where(hit0, 1.0, jnp.where(hit1, 1.0, 0.0))
    before = _dot(tri_ref[...], onehot.astype(BF16)) + carry_ref[...]
    r0 = jnp.sum(jnp.where(hit0, before, 0.0), axis=1, keepdims=True)
    r1 = jnp.sum(jnp.where(hit1, before, 0.0), axis=1, keepdims=True)
    carry = carry_ref[...] + jnp.sum(onehot, axis=0, keepdims=True)
    carry_ref[...] = carry
    cnt_ref[...] = jnp.broadcast_to(carry, cnt_ref.shape)
    rec = [e0.astype(F32), e1.astype(F32), r0, r1, w0, w1]
    out = jnp.zeros(lg.shape, F32)
    for idx, val in enumerate(rec):
        out = jnp.where(lane == idx, val, out)
    rt_ref[...] = out


def _router(x, mod4, layer, w_r, b_r, tri):
    tm = ROUTER_TM
    return pl.pallas_call(
        _router_body,
        grid=(NTOK // tm,),
        in_specs=[pl.BlockSpec((tm, D_MODEL), lambda i: (i, 0)),
                  _mod_spec(layer, 4, tm), _mod_spec(layer, 3, tm),
                  pl.BlockSpec((D_MODEL, 128), lambda i: (0, 0)),
                  pl.BlockSpec((1, 128), lambda i: (0, 0)),
                  pl.BlockSpec((tm, tm), lambda i: (0, 0))],
        out_specs=[pl.BlockSpec((tm, 128), lambda i: (i, 0)),
                   pl.BlockSpec((8, 128), lambda i: (0, 0))],
        out_shape=[jax.ShapeDtypeStruct((NTOK, 128), F32), jax.ShapeDtypeStruct((8, 128), F32)],
        scratch_shapes=[pltpu.VMEM((1, 128), F32)],
        compiler_params=_params(("arbitrary",), 40),
        name="moe_router",
    )(x, mod4, mod4, w_r, b_r, tri)


DISPATCH_TM = 512


def _dispatch_body(dest_ref, pend_ref, padded_ref, x_ref, sc_ref, sh_ref, xs_hbm, hbuf, zbuf, sem, zsem):
    tm = DISPATCH_TM
    i = pl.program_id(0)

    @pl.when(i == 0)
    def _():
        zbuf[...] = jnp.zeros(zbuf.shape, F32)

        def tail(e):
            start = pl.multiple_of(pend_ref[e] - MOE_TM, MOE_TM)
            return pltpu.make_async_copy(zbuf, xs_hbm.at[pl.ds(start, MOE_TM), :], zsem)

        for e in range(N_EXPERTS):
            @pl.when(padded_ref[e] > 0)
            def _():
                tail(e).start()
        for e in range(N_EXPERTS):
            @pl.when(padded_ref[e] > 0)
            def _():
                tail(e).wait()

    hbuf[...] = x_ref[...] * (1.0 + sc_ref[0, 0]) + sh_ref[0, 0]

    def send(t, carry):
        for k in range(EXPERT_TOP_K):
            d = dest_ref[k * NTOK + i * tm + t]
            _row_copy(hbuf, t, xs_hbm, d, sem).start()
        return carry
    lax.fori_loop(0, tm, send, 0, unroll=8)

    def drain(t, carry):
        for k in range(EXPERT_TOP_K):
            _row_copy(hbuf, 0, xs_hbm, 0, sem).wait()
        return carry
    lax.fori_loop(0, tm, drain, 0, unroll=8)


def _dispatch(x, dest, pend, padded, mod4, layer):
    tm = DISPATCH_TM
    grid_spec = pltpu.PrefetchScalarGridSpec(
        num_scalar_prefetch=3,
        grid=(NTOK // tm,),
        in_specs=[pl.BlockSpec((tm, D_MODEL), lambda i, *_: (i, 0)),
                  _mod_spec(layer, 4, tm), _mod_spec(layer, 3, tm)],
        out_specs=pl.BlockSpec(memory_space=pl.ANY),
        scratch_shapes=[pltpu.VMEM((tm, D_MODEL), F32), pltpu.VMEM((MOE_TM, D_MODEL), F32),
                        pltpu.SemaphoreType.DMA(()), pltpu.SemaphoreType.DMA(())],
    )
    return pl.pallas_call(
        _dispatch_body,
        grid_spec=grid_spec,
        out_shape=jax.ShapeDtypeStruct((MOE_ROWS, D_MODEL), F32),
        compiler_params=_params(("arbitrary",), 40),
        name="moe_dispatch",
    )(dest, pend, padded, x, mod4, mod4)


def _expert_body(te_ref, tv_ref, x_ref, wg_ref, wu_ref, wd_ref, o_ref, wg_s, wu_s, wd_s):
    t = pl.program_id(0)
    prev = te_ref[jnp.maximum(t - 1, 0)]

    @pl.when((t == 0) | (te_ref[t] != prev))
    def _():
        wg_s[...] = wg_ref[...].astype(BF16)
        wu_s[...] = wu_ref[...].astype(BF16)
        wd_s[...] = wd_ref[...].astype(BF16)

    @pl.when(tv_ref[t] > 0)
    def _():
        x = x_ref[...].astype(BF16)
        g = _dot(x, wg_s[...])
        u = _dot(x, wu_s[...])
        act = (g / (1.0 + jnp.exp(-g))) * u
        o_ref[...] = _dot(act.astype(BF16), wd_s[...])

    @pl.when(tv_ref[t] == 0)
    def _():
        o_ref[...] = jnp.zeros(o_ref.shape, o_ref.dtype)


def _experts(xs, tile_e, tile_v, w_gate, w_up, w_down, layer):
    grid_spec = pltpu.PrefetchScalarGridSpec(
        num_scalar_prefetch=2,
        grid=(MOE_TILES,),
        in_specs=[
            pl.BlockSpec((MOE_TM, D_MODEL), lambda t, te, tv: (jnp.where(tv[t] > 0, t, 0), 0)),
            pl.BlockSpec((None, None, D_MODEL, D_EXPERT), lambda t, te, tv: (layer, te[t], 0, 0)),
            pl.BlockSpec((None, None, D_MODEL, D_EXPERT), lambda t, te, tv: (layer, te[t], 0, 0)),
            pl.BlockSpec((None, None, D_EXPERT, D_MODEL), lambda t, te, tv: (layer, te[t], 0, 0)),
        ],
        out_specs=pl.BlockSpec((MOE_TM, D_MODEL), lambda t, te, tv: (t, 0)),
        scratch_shapes=[pltpu.VMEM((D_MODEL, D_EXPERT), BF16), pltpu.VMEM((D_MODEL, D_EXPERT), BF16),
                        pltpu.VMEM((D_EXPERT, D_MODEL), BF16)],
    )
    return pl.pallas_call(
        _expert_body,
        grid_spec=grid_spec,
        out_shape=jax.ShapeDtypeStruct((MOE_ROWS, D_MODEL), F32),
        compiler_params=_params(("arbitrary",), 56),
        name="moe_experts",
    )(tile_e, tile_v, xs, w_gate, w_up, w_down)


def _layout(route, counts):
    counts = counts[0, :N_EXPERTS].astype(jnp.int32)
    padded = ((counts + MOE_TM - 1) // MOE_TM) * MOE_TM
    pend = jnp.cumsum(padded)
    pstart = pend - padded
    experts = jnp.arange(N_EXPERTS, dtype=jnp.int32)[None, :]

    def dest_of(e_lane, r_lane):
        e = route[:, e_lane].astype(jnp.int32)
        base = jnp.sum(jnp.where(e[:, None] == experts, pstart[None, :], 0), axis=1)
        return base + route[:, r_lane].astype(jnp.int32)

    dest = jnp.concatenate([dest_of(RT_E0, RT_R0), dest_of(RT_E1, RT_R1)])
    tile_start = jnp.arange(MOE_TILES, dtype=jnp.int32) * MOE_TM
    tile_e = jnp.minimum(jnp.sum((tile_start[:, None] >= pend[None, :]).astype(jnp.int32), axis=1), N_EXPERTS - 1)
    tile_v = (tile_start < pend[-1]).astype(jnp.int32)
    return dest, pend.astype(jnp.int32), padded, tile_e.astype(jnp.int32), tile_v


def _moe(x, mod4, layer, w_r, b_r, tri, w_gate, w_up, w_down, ln_g, ln_b):
    route, counts = _router(x, mod4, layer, w_r, b_r, tri)
    dest, pend, padded, tile_e, tile_v = _layout(route, counts)
    xs = _dispatch(x, dest, pend, padded, mod4, layer)
    rows = _experts(xs, tile_e, tile_v, w_gate, w_up, w_down, layer)
    return _combine_ln(x, rows, dest, route, mod4, layer, 5, ln_g, ln_b)


def kernel(x_prompt, x_sample, state_gla, cache_diff_k, cache_diff_v, cache_mla_ckv, cache_mla_krope, c, c_ctx, ada_w, ada_b, ln_g, ln_b, ab_w_in, gla_w_gate2, gla_b_gate2, gla_norm_g, diff_lambda, diff_norm_g, ab_w_out, mla_w_in, mla_q_norm_g, mla_w_uq, mla_kv_norm_g, mla_w_ukv, mla_w_out, moe_w_rg, moe_b_rg, moe_w_re, moe_b_re, moe_w_gate, moe_w_up, moe_w_down):
    x = jnp.concatenate([x_prompt.reshape(NCTX, D_MODEL), x_sample.reshape(NLAT, D_MODEL)], axis=0)
    cond = jnp.concatenate([c_ctx[None, :], c, jnp.zeros((N_COND - 1 - DEC_BATCH, D_MODEL), F32)], axis=0)
    mod4 = _ada_mod(cond, ada_w, ada_b).reshape(DEPTH, N_COND, 1, 6 * D_MODEL)
    cos_t, sin_t = _rope_tables()
    levf = jnp.asarray(_gla_level_table(False))
    levb = jnp.asarray(_gla_level_table(True))
    tri = jnp.asarray(np.tril(np.ones((ROUTER_TM, ROUTER_TM), np.float32), -1), BF16)
    cache_k = cache_diff_k.reshape(DEC_BATCH, -1, PAST_LEN, B_HEADS * 2 * B_DH)
    cache_v = cache_diff_v.reshape(DEC_BATCH, -1, PAST_LEN, B_HEADS * B_DV)

    new_gla, new_dk, new_dv, new_ckv, new_kr = [], [], [], [], []
    for layer in range(DEPTH):
        i = layer // 2
        if layer % 2 == 0:
            w_d = jnp.concatenate([ab_w_in[i, :, AB_D0:], ab_w_in[i, :, AB_GG0:AB_D0],
                                   jnp.zeros((D_MODEL, 128 - 2 * A_GATE_RANK), F32)], axis=1)[None]
            pa = _mod_matmul(x, mod4, layer, ab_w_in, i, PA_N, 512, "ab_in_gla")
            pd = _mod_matmul(x, mod4, layer, w_d, 0, PD_N, 640, "ab_in_diff")
            wg = jnp.zeros((A_HEADS, 128, 2 * A_DK), F32)
            for j in range(2):
                blk = gla_w_gate2[i, j].reshape(A_GATE_RANK, A_HEADS, A_DK).transpose(1, 0, 2)
                wg = wg.at[:, j * A_GATE_RANK:(j + 1) * A_GATE_RANK, j * A_DK:(j + 1) * A_DK].set(blk)
            bg = gla_b_gate2[i].reshape(2, A_HEADS, A_DK).transpose(1, 0, 2).reshape(A_HEADS, 1, 2 * A_DK)
            g_gla = gla_norm_g[i][None, :]
            o_gla_c, st = _gla(pa, pd, wg, bg, g_gla, levf, levb, 0, BATCH, SEQ, None, 0, True)
            (o_gla,) = _gla(pa, pd, wg, bg, g_gla, levf, levb, NCTX, DEC_BATCH, DEC_SEQ, state_gla, i, False,
                            prev=o_gla_c)
            lv = diff_lambda[i]
            lam_init = 0.8 - 0.6 * math.exp(-0.3 * layer)
            lam = (jnp.exp(jnp.sum(lv[0] * lv[1])) - jnp.exp(jnp.sum(lv[2] * lv[3])) + lam_init).reshape(1, 1)
            g_diff = diff_norm_g[i][None, :]
            o_diff_c = _diff_attn(pd, lam, g_diff, lam_init, 0, BATCH, SEQ)
            o_diff = _diff_attn(pd, lam, g_diff, lam_init, NCTX, DEC_BATCH, DEC_SEQ,
                                cache=(cache_k, cache_v, i), rope=(cos_t, sin_t), bq=DEC_SEQ, prev=o_diff_c)
            x = _proj_ln([o_gla, o_diff], ab_w_out, i, x, mod4, layer, 2,
                         ln_g[layer, 0][None, :], ln_b[layer, 0][None, :], "ab_out_ln")
            new_gla.append(st)
            new_dk.append(pd[:NCTX, PD_K:PD_V].reshape(BATCH, SEQ, B_HEADS, 2, B_DH))
            new_dv.append(pd[:NCTX, PD_V:PD_GG].reshape(BATCH, SEQ, B_HEADS, B_DV))
        else:
            c_in = C_Q_RANK + C_KV_RANK + C_ROPE
            pm = _mod_matmul(x, mod4, layer, mla_w_in, i, c_in, c_in, "mla_in", tm=512)
            w_uq = mla_w_uq[i].reshape(C_Q_RANK, C_HEADS, C_NOPE + C_ROPE)
            w_uq = jnp.concatenate([w_uq[:, :, :C_NOPE].reshape(C_Q_RANK, -1),
                                    w_uq[:, :, C_NOPE:].reshape(C_Q_RANK, -1)], axis=1)
            g_q = mla_q_norm_g[i][None, :]
            g_kv = mla_kv_norm_g[i][None, :]
            qn = _rms_matmul(pm, 0, C_Q_RANK, g_q, w_uq, 0, C_HEADS * C_NOPE, 512, 1024, BF16, "mla_q_nope")
            qr = _rms_matmul(pm, 0, C_Q_RANK, g_q, w_uq, C_HEADS * C_NOPE // 512, C_HEADS * C_ROPE, 512, 1024, F32,
                             "mla_q_rope")
            kv, ckv_n = _rms_matmul(pm, C_Q_RANK // C_KV_RANK, C_KV_RANK, g_kv, mla_w_ukv[i], 0,
                                    C_HEADS * (C_NOPE + C_DV), 1024, 1024, BF16, "mla_kv", emit_norm=True)
            kv_c = _rms_matmul(cache_mla_ckv[:, i].reshape(DEC_BATCH * PAST_LEN, C_KV_RANK), 0, C_KV_RANK, g_kv,
                               mla_w_ukv[i], 0, C_HEADS * (C_NOPE + C_DV), 1024, 1024, BF16, "mla_kv_cache",
                               norm=False)
            krope = pm[:, C_Q_RANK + C_KV_RANK:]
            kr2 = jnp.concatenate([krope, krope], axis=1)
            krc = cache_mla_krope[:, i].reshape(DEC_BATCH * PAST_LEN, C_ROPE)
            krc2 = jnp.concatenate([krc, krc], axis=1)
            o_c = _mla_attn(qn, qr, kv, kr2, 0, BATCH, SEQ)
            o_mla = _mla_attn(qn, qr, kv, kr2, NCTX, DEC_BATCH, DEC_SEQ, cache=(kv_c, krc2), rope=(cos_t, sin_t),
                              bq=DEC_SEQ, prev=o_c)
            x = _proj_ln([o_mla], mla_w_out, i, x, mod4, layer, 2,
                         ln_g[layer, 0][None, :], ln_b[layer, 0][None, :], "mla_out_ln")
            new_ckv.append(ckv_n[:NCTX].reshape(BATCH, SEQ, C_KV_RANK))
            new_kr.append(krope[:NCTX].reshape(BATCH, SEQ, C_ROPE))
        w_r = jnp.concatenate([moe_w_rg[layer], moe_w_re[layer],
                               jnp.zeros((D_MODEL, 128 - N_GROUPS - N_EXPERTS), F32)], axis=1)
        b_r = jnp.concatenate([moe_b_rg[layer], moe_b_re[layer],
                               jnp.zeros((128 - N_GROUPS - N_EXPERTS,), F32)])[None, :]
        x = _moe(x, mod4, layer, w_r, b_r, tri, moe_w_gate, moe_w_up, moe_w_down,
                 ln_g[layer, 1][None, :], ln_b[layer, 1][None, :])

    return (x[:NCTX].reshape(BATCH, SEQ, D_MODEL),
            x[NCTX:].reshape(DEC_BATCH, DEC_SEQ, D_MODEL),
            jnp.stack(new_gla, axis=1), jnp.stack(new_dk, axis=1), jnp.stack(new_dv, axis=1),
            jnp.stack(new_ckv, axis=1), jnp.stack(new_kr, axis=1))
```

```python
import functools
import math

import numpy as np
import jax
import jax.numpy as jnp
from jax import lax
from jax.experimental import pallas as pl
from jax.experimental.pallas import tpu as pltpu

F32 = jnp.float32
BF16 = jnp.bfloat16

D_MODEL = 2048
BATCH = 16
SEQ = 256
DEPTH = 4
DEC_BATCH = 4
DEC_SEQ = 1024
PAST_LEN = 256
GRID_W = 64
A_HEADS = 4
A_DK = 128
A_DV = 256
A_GATE_RANK = 16
A_TAU = 16.0
B_HEADS = 8
B_DH = 64
B_DV = 2 * B_DH
C_HEADS = 16
C_Q_RANK = 512
C_KV_RANK = 256
C_NOPE = 128
C_ROPE = 64
C_DV = 128
N_GROUPS = 4
EXPERTS_PER_GROUP = 4
N_EXPERTS = N_GROUPS * EXPERTS_PER_GROUP
EXPERT_TOP_K = 2
D_EXPERT = 512
ROPE_THETA = 10000.0
LN_EPS = 1e-5
RMS_EPS = 1e-6
DEEPNORM_ALPHA = (2.0 * DEPTH) ** 0.25

NCTX = BATCH * SEQ
NLAT = DEC_BATCH * DEC_SEQ
NTOK = NCTX + NLAT
N_COND = 8
GLA_CHUNK = 256
GLA_LEVELS = 8
MOE_TM = 512
LOG2E = 1.4426950408889634
MOE_ROWS = NTOK * EXPERT_TOP_K + N_EXPERTS * MOE_TM
MOE_TILES = MOE_ROWS // MOE_TM

PA_Q, PA_K, PA_V, PA_R = 0, A_HEADS * A_DK, 2 * A_HEADS * A_DK, 2 * A_HEADS * A_DK + A_HEADS * A_DV
PA_N = PA_R + A_HEADS * A_DV
AB_GG0 = PA_N
AB_D0 = PA_N + 2 * A_GATE_RANK
PD_Q, PD_K, PD_V = 0, B_HEADS * 2 * B_DH, 2 * B_HEADS * 2 * B_DH
PD_GG = PD_V + B_HEADS * B_DV
PD_N = PD_GG + 128


def _params(sem, vmem_mb):
    return pltpu.CompilerParams(dimension_semantics=sem, vmem_limit_bytes=vmem_mb * 2 ** 20)


def _cond_row(i, tm):
    return jnp.where(i * tm < NCTX, 0, 1 + (i * tm - NCTX) // DEC_SEQ)


def _dot(a, b):
    return jnp.dot(a, b, preferred_element_type=F32)


def _dot_nt(a, b):
    return lax.dot_general(a, b, (((1,), (1,)), ((), ())), preferred_element_type=F32)


def _dot_tn(a, b):
    return lax.dot_general(a, b, (((0,), (0,)), ((), ())), preferred_element_type=F32)


def _ada_body(c_ref, w_ref, b_ref, o_ref):
    c = c_ref[...]
    a = (c / (1.0 + jnp.exp(-c))).astype(BF16)
    o_ref[0] = _dot(a, w_ref[0].astype(BF16)) + b_ref[0]


def _ada_mod(cond, ada_w, ada_b):
    tn = 1024
    return pl.pallas_call(
        _ada_body,
        grid=(DEPTH, 6 * D_MODEL // tn),
        in_specs=[pl.BlockSpec((N_COND, D_MODEL), lambda l, j: (0, 0)),
                  pl.BlockSpec((1, D_MODEL, tn), lambda l, j: (l, 0, j)),
                  pl.BlockSpec((1, 1, tn), lambda l, j: (l, 0, j))],
        out_specs=pl.BlockSpec((1, N_COND, tn), lambda l, j: (l, 0, j)),
        out_shape=jax.ShapeDtypeStruct((DEPTH, N_COND, 6 * D_MODEL), F32),
        compiler_params=_params(("arbitrary", "arbitrary"), 40),
        name="ada_mod",
    )(cond, ada_w, ada_b.reshape(DEPTH, 1, 6 * D_MODEL))


def _mod_spec(layer, chunk, tm):
    return pl.BlockSpec((1, 1, 1, D_MODEL), lambda i, *_: (layer, _cond_row(i, tm), 0, chunk))


def _mod_mm_body(x_ref, sc_ref, sh_ref, w_ref, o_ref, h_ref):
    @pl.when(pl.program_id(1) == 0)
    def _():
        h_ref[...] = (x_ref[...] * (1.0 + sc_ref[0, 0]) + sh_ref[0, 0]).astype(BF16)

    o_ref[...] = _dot(h_ref[...], w_ref[...].astype(BF16)).astype(o_ref.dtype)


def _mod_matmul(x, mod4, layer, w, w_layer, n_out, tn, name, tm=1024, out_dtype=F32):
    return pl.pallas_call(
        _mod_mm_body,
        grid=(NTOK // tm, n_out // tn),
        in_specs=[pl.BlockSpec((tm, D_MODEL), lambda i, j: (i, 0)),
                  _mod_spec(layer, 1, tm), _mod_spec(layer, 0, tm),
                  pl.BlockSpec((None, D_MODEL, tn), lambda i, j: (w_layer, 0, j))],
        out_specs=pl.BlockSpec((tm, tn), lambda i, j: (i, j)),
        out_shape=jax.ShapeDtypeStruct((NTOK, n_out), out_dtype),
        scratch_shapes=[pltpu.VMEM((tm, D_MODEL), BF16)],
        compiler_params=_params(("arbitrary", "arbitrary"), 56),
        name=name,
    )(x, mod4, mod4, w)


def _rms_mm_body(x_ref, g_ref, w_ref, *rest, norm, emit_norm):
    if emit_norm:
        o_ref, n_ref, h_ref = rest
    else:
        o_ref, h_ref = rest

    @pl.when(pl.program_id(1) == 0)
    def _():
        x = x_ref[...]
        if norm:
            x = x * lax.rsqrt(jnp.mean(x * x, axis=-1, keepdims=True) + RMS_EPS) * g_ref[...]
        h_ref[...] = x.astype(BF16)
        if emit_norm:
            n_ref[...] = x

    o_ref[...] = _dot(h_ref[...], w_ref[...].astype(BF16)).astype(o_ref.dtype)


def _rms_matmul(x, col_block, k, g, w, col0_blocks, n_out, tn, tm, out_dtype, name, norm=True, emit_norm=False):
    rows = x.shape[0]
    out_shape = [jax.ShapeDtypeStruct((rows, n_out), out_dtype)]
    out_specs = [pl.BlockSpec((tm, tn), lambda i, j: (i, j))]
    if emit_norm:
        out_shape.append(jax.ShapeDtypeStruct((rows, k), F32))
        out_specs.append(pl.BlockSpec((tm, k), lambda i, j: (i, 0)))
    res = pl.pallas_call(
        functools.partial(_rms_mm_body, norm=norm, emit_norm=emit_norm),
        grid=(rows // tm, n_out // tn),
        in_specs=[pl.BlockSpec((tm, k), lambda i, j: (i, col_block)),
                  pl.BlockSpec((1, k), lambda i, j: (0, 0)),
                  pl.BlockSpec((k, tn), lambda i, j: (0, col0_blocks + j))],
        out_specs=out_specs,
        out_shape=out_shape,
        scratch_shapes=[pltpu.VMEM((tm, k), BF16)],
        compiler_params=_params(("arbitrary", "arbitrary"), 40),
        name=name,
    )(x, g, w)
    return res if emit_norm else res[0]


def _layer_norm(z, g, b):
    mu = jnp.mean(z, axis=-1, keepdims=True)
    zc = z - mu
    var = jnp.mean(zc * zc, axis=-1, keepdims=True)
    return zc * lax.rsqrt(var + LN_EPS) * g + b


LN_ROWS = 256


def _proj_ln_body(*refs, n_lhs, n_j, tm, tn):
    lhs = refs[:n_lhs]
    ws = refs[n_lhs:2 * n_lhs]
    x_ref, gate_ref, g_ref, b_ref, o_ref = refs[2 * n_lhs:]
    j = pl.program_id(1)
    y = _dot(lhs[0][...], ws[0][...].astype(BF16))
    for a, w in zip(lhs[1:], ws[1:]):
        y = y + _dot(a[...], w[...].astype(BF16))
    o_ref[:, pl.ds(pl.multiple_of(j * tn, tn), tn)] = y

    @pl.when(j == n_j - 1)
    def _():
        def body(r, carry):
            rows = pl.ds(pl.multiple_of(r * LN_ROWS, LN_ROWS), LN_ROWS)
            z = DEEPNORM_ALPHA * x_ref[rows, :] + gate_ref[0, 0] * o_ref[rows, :]
            o_ref[rows, :] = _layer_norm(z, g_ref[...], b_ref[...])
            return carry
        lax.fori_loop(0, tm // LN_ROWS, body, 0)


def _proj_ln(lhs_list, w, w_layer, x, mod4, layer, gate_chunk, ln_g, ln_b, name, tm=1024, tn=512):
    n_lhs = len(lhs_list)
    kk = lhs_list[0].shape[1]
    n_j = D_MODEL // tn
    in_specs = [pl.BlockSpec((tm, kk), lambda i, j: (i, 0)) for _ in lhs_list]
    in_specs += [pl.BlockSpec((None, kk, tn), functools.partial(lambda i, j, r: (w_layer, r, j), r=r))
                 for r in range(n_lhs)]
    in_specs += [pl.BlockSpec((tm, D_MODEL), lambda i, j: (i, 0)),
                 _mod_spec(layer, gate_chunk, tm),
                 pl.BlockSpec((1, D_MODEL), lambda i, j: (0, 0)),
                 pl.BlockSpec((1, D_MODEL), lambda i, j: (0, 0))]
    return pl.pallas_call(
        functools.partial(_proj_ln_body, n_lhs=n_lhs, n_j=n_j, tm=tm, tn=tn),
        grid=(NTOK // tm, n_j),
        in_specs=in_specs,
        out_specs=pl.BlockSpec((tm, D_MODEL), lambda i, j: (i, 0)),
        out_shape=jax.ShapeDtypeStruct((NTOK, D_MODEL), F32),
        compiler_params=_params(("arbitrary", "arbitrary"), 56),
        name=name,
    )(*lhs_list, *([w] * n_lhs), x, mod4, ln_g, ln_b)


COMBINE_TM = 256


SUB = 8


def _row(ref, i):
    return ref.at[i >> 3, pl.ds(i & (SUB - 1), 1), :]


def _combine_ln_body(dest_ref, x_ref, rt_ref, gate_ref, g_ref, b_ref, rows_hbm, o_ref, buf, sem):
    tm = COMBINE_TM
    i = pl.program_id(0)
    n_i = pl.num_programs(0)

    def fetch(step, slot):
        def body(g, carry):
            for s in range(SUB):
                for k in range(EXPERT_TOP_K):
                    d = dest_ref[k * NTOK + step * tm + g * SUB + s]
                    pltpu.make_async_copy(_row(rows_hbm, d), buf.at[slot, k, g, pl.ds(s, 1), :],
                                          sem.at[slot]).start()
            return carry
        lax.fori_loop(0, tm // SUB, body, 0)

    @pl.when(i == 0)
    def _():
        fetch(0, 0)

    @pl.when(i + 1 < n_i)
    def _():
        fetch(i + 1, (i + 1) % 2)

    slot = i % 2

    def drain(t, carry):
        for k in range(EXPERT_TOP_K):
            pltpu.make_async_copy(_row(rows_hbm, 0), buf.at[slot, k, 0, pl.ds(0, 1), :], sem.at[slot]).wait()
        return carry
    lax.fori_loop(0, tm, drain, 0, unroll=8)

    rt = rt_ref[...]
    y0 = buf[slot, 0].reshape(tm, D_MODEL)
    y1 = buf[slot, 1].reshape(tm, D_MODEL)
    y = rt[:, RT_W0:RT_W0 + 1] * y0 + rt[:, RT_W1:RT_W1 + 1] * y1
    z = DEEPNORM_ALPHA * x_ref[...] + gate_ref[0, 0] * y
    o_ref[...] = _layer_norm(z, g_ref[...], b_ref[...])


def _combine_ln(x, rows, dest, route, mod4, layer, gate_chunk, ln_g, ln_b):
    tm = COMBINE_TM
    row = pl.BlockSpec((tm, D_MODEL), lambda i, d: (i, 0))
    vec = pl.BlockSpec((1, D_MODEL), lambda i, d: (0, 0))
    grid_spec = pltpu.PrefetchScalarGridSpec(
        num_scalar_prefetch=1,
        grid=(NTOK // tm,),
        in_specs=[row, pl.BlockSpec((tm, 128), lambda i, d: (i, 0)),
                  _mod_spec(layer, gate_chunk, tm), vec, vec,
                  pl.BlockSpec(memory_space=pl.ANY)],
        out_specs=row,
        scratch_shapes=[pltpu.VMEM((2, EXPERT_TOP_K, tm // SUB, SUB, D_MODEL), F32),
                        pltpu.SemaphoreType.DMA((2,))],
    )
    return pl.pallas_call(
        _combine_ln_body,
        grid_spec=grid_spec,
        out_shape=jax.ShapeDtypeStruct((NTOK, D_MODEL), F32),
        compiler_params=_params(("arbitrary",), 40),
        name="moe_combine_ln",
    )(dest, x, route, mod4, ln_g, ln_b, rows.reshape(MOE_ROWS // SUB, SUB, D_MODEL))


def _gla_level_table(reverse):
    t = np.arange(GLA_CHUNK)[:, None]
    s = np.arange(GLA_CHUNK)[None, :]
    x = t ^ s
    lev = np.where(x > 0, np.floor(np.log2(np.maximum(x, 1))).astype(np.int32), -1)
    live = (t < s) if reverse else (t > s)
    return np.where(live | (t == s), lev, -2).astype(np.int32)


def _anchor_rows(u, level, reverse):
    c = u.shape[0]
    n = c // 8
    u3 = u.reshape(n, 8, 128)
    half = 1 << level
    if level < 3:
        blk = 2 * half
        sub = lax.broadcasted_iota(jnp.int32, (n, 8, 128), 1)
        out = None
        for j in range(8 // blk):
            r = j * blk + (half if reverse else half - 1)
            piece = jnp.broadcast_to(u3[:, r:r + 1, :], (n, 8, 128))
            out = piece if out is None else jnp.where(sub >= j * blk, piece, out)
        return out.reshape(c, 128)
    m = 1 << (level - 2)
    r = 0 if reverse else 7
    e = jnp.broadcast_to(u3[:, r:r + 1, :], (n, 8, 128)).reshape(n // m, m, 8, 128)
    idx = m // 2 if reverse else m // 2 - 1
    return jnp.broadcast_to(e[:, idx:idx + 1], (n // m, m, 8, 128)).reshape(c, 128)


def _gla_intra(q, k, la, lev, reverse):
    rowk = lax.broadcasted_iota(jnp.int32, (GLA_CHUNK, 128), 0)
    scores = jnp.where(lev == -1, _dot_nt(q.astype(BF16), k.astype(BF16)), 0.0)
    u = la
    for level in range(GLA_LEVELS):
        anchor = _anchor_rows(u, level, reverse)
        qa = (q * jnp.exp2(u)).astype(BF16)
        ka = (k * jnp.exp2(jnp.minimum(anchor - u, 0.0))).astype(BF16)
        scores = jnp.where(lev == level, _dot_nt(qa, ka), scores)
        query_side = ((rowk >> level) & 1) == (0 if reverse else 1)
        u = u + jnp.where(query_side, anchor, 0.0)
    return scores, u


def _gla_body(*refs, t_len, has_s0, emit_state, has_prev):
    it = iter(refs)
    q_ref, k_ref, v_ref, r_ref, gg_ref, wg_ref, bg_ref, g_ref, levf_ref, levb_ref = [next(it) for _ in range(10)]
    s0_ref = next(it) if has_s0 else None
    if has_prev:
        next(it)
    o_ref = next(it)
    st_ref = next(it) if emit_state else None
    acc_ref, sf_ref, sb_ref, la_ref = [next(it) for _ in range(4)]
    c = GLA_CHUNK
    n_ch = t_len // c

    gl = _dot(gg_ref[...].astype(BF16), wg_ref[...].astype(BF16)) + bg_ref[...]
    la_ref[...] = (jnp.minimum(gl, 0.0) - jnp.log1p(jnp.exp(-jnp.abs(gl)))) * (LOG2E / A_TAU)
    if has_s0:
        sf_ref[...] = s0_ref[0]
        sb_ref[...] = s0_ref[1]

    eye = (lax.broadcasted_iota(jnp.int32, (A_DK, A_DK), 0) == lax.broadcasted_iota(jnp.int32, (A_DK, A_DK), 1))

    def chunk_step(ci, reverse, with_state):
        rows = pl.ds(ci * c if isinstance(ci, int) else pl.multiple_of(ci * c, c), c)
        q = q_ref[rows, :] * (A_DK ** -0.5)
        k = k_ref[rows, :]
        v = v_ref[rows, :].astype(BF16)
        la = la_ref[rows, A_DK:2 * A_DK] if reverse else la_ref[rows, 0:A_DK]
        s_ref = sb_ref if reverse else sf_ref
        scores, cum = _gla_intra(q, k, la, (levb_ref if reverse else levf_ref)[...], reverse)
        o = _dot(scores.astype(BF16), v)
        if with_state:
            o = o + _dot((q * jnp.exp2(cum)).astype(BF16), s_ref[...].astype(BF16))
        tot = jnp.broadcast_to(cum[0:1, :] if reverse else cum[c - 1:c, :], (c, A_DK))
        delta = _dot_tn((k * jnp.exp2(tot - cum)).astype(BF16), v)
        if with_state:
            decay = jnp.sum(jnp.where(eye, jnp.exp2(tot[0:A_DK, :]), 0.0), axis=1, keepdims=True)
            s_ref[...] = decay * s_ref[...] + delta
        else:
            s_ref[...] = delta
        if not reverse:
            acc_ref[rows, :] = o
        else:
            y = acc_ref[rows, :] + o
            y = y * lax.rsqrt(jnp.mean(y * y, axis=-1, keepdims=True) + RMS_EPS) * g_ref[...]
            r = r_ref[rows, :]
            o_ref[rows, :] = (y * (r / (1.0 + jnp.exp(-r)))).astype(o_ref.dtype)

    for reverse in (False, True):
        first = (n_ch - 1) if reverse else 0
        chunk_step(first, reverse, has_s0)
        if n_ch > 1:
            def body(i, carry, reverse=reverse):
                chunk_step((n_ch - 1 - i) if reverse else i, reverse, True)
                return carry
            lax.fori_loop(1, n_ch, body, 0)

    if emit_state:
        st_ref[0] = sf_ref[...]
        st_ref[1] = sb_ref[...]


def _alias_prev(in_specs, args, prev):
    if prev is None:
        return {}
    in_specs.append(pl.BlockSpec(memory_space=pl.ANY))
    args.append(prev)
    return {len(args) - 1: 0}


def _gla(pa, pd, wg, bg, gla_g, levf, levb, row0, n_seq, t_len, s0, s0_layer, emit_state, prev=None):
    rb = row0 // t_len
    kb, vb, rbk = PA_K // A_DK, PA_V // A_DV, PA_R // A_DV
    in_specs = [
        pl.BlockSpec((t_len, A_DK), lambda n, h: (rb + n, h)),
        pl.BlockSpec((t_len, A_DK), lambda n, h: (rb + n, kb + h)),
        pl.BlockSpec((t_len, A_DV), lambda n, h: (rb + n, vb + h)),
        pl.BlockSpec((t_len, A_DV), lambda n, h: (rb + n, rbk + h)),
        pl.BlockSpec((t_len, 128), lambda n, h: (rb + n, PD_GG // 128)),
        pl.BlockSpec((None, 128, 2 * A_DK), lambda n, h: (h, 0, 0)),
        pl.BlockSpec((None, 1, 2 * A_DK), lambda n, h: (h, 0, 0)),
        pl.BlockSpec((1, A_DV), lambda n, h: (0, 0)),
        pl.BlockSpec((GLA_CHUNK, GLA_CHUNK), lambda n, h: (0, 0)),
        pl.BlockSpec((GLA_CHUNK, GLA_CHUNK), lambda n, h: (0, 0)),
    ]
    args = [pa, pa, pa, pa, pd, wg, bg, gla_g, levf, levb]
    if s0 is not None:
        in_specs.append(pl.BlockSpec((None, None, 2, None, A_DK, A_DV), lambda n, h: (n, s0_layer, 0, h, 0, 0)))
        args.append(s0)
    aliases = _alias_prev(in_specs, args, prev)
    out_shape = [jax.ShapeDtypeStruct((NTOK, A_HEADS * A_DV), BF16)]
    out_specs = [pl.BlockSpec((t_len, A_DV), lambda n, h: (rb + n, h))]
    if emit_state:
        out_shape.append(jax.ShapeDtypeStruct((n_seq, 2, A_HEADS, A_DK, A_DV), F32))
        out_specs.append(pl.BlockSpec((None, 2, None, A_DK, A_DV), lambda n, h: (n, 0, h, 0, 0)))
    res = pl.pallas_call(
        functools.partial(_gla_body, t_len=t_len, has_s0=s0 is not None, emit_state=emit_state,
                          has_prev=prev is not None),
        grid=(n_seq, A_HEADS),
        in_specs=in_specs,
        out_specs=out_specs,
        out_shape=out_shape,
        input_output_aliases=aliases,
        scratch_shapes=[pltpu.VMEM((t_len, A_DV), F32), pltpu.VMEM((A_DK, A_DV), F32),
                        pltpu.VMEM((A_DK, A_DV), F32), pltpu.VMEM((t_len, 2 * A_DK), F32)],
        compiler_params=_params(("arbitrary", "arbitrary"), 40),
        name="gla_ctx" if s0 is None else "gla_lat",
    )(*args)
    return res


def _rope(x, cos, sin):
    lane = lax.broadcasted_iota(jnp.int32, x.shape, 1)
    partner = jnp.where((lane & 32) == 0, pltpu.roll(x, 96, 1), pltpu.roll(x, 32, 1))
    return x * cos + partner * sin


def _rope_tables():
    n_rows = DEC_SEQ // GRID_W
    rows = jnp.repeat(jnp.arange(n_rows, dtype=F32), GRID_W)
    cols = jnp.tile(jnp.arange(GRID_W, dtype=F32), n_rows)
    quarter = C_ROPE // 4
    freqs = ROPE_THETA ** (-jnp.arange(quarter, dtype=F32) / quarter)
    ang = jnp.concatenate([rows[:, None] * freqs, cols[:, None] * freqs], axis=-1)
    cos, sin = jnp.cos(ang), jnp.sin(ang)
    cos_t = jnp.tile(cos, (1, 4))
    sin_t = jnp.tile(jnp.concatenate([-sin, sin], axis=-1), (1, 2))
    return cos_t, sin_t


def _softmax_parts(scores, scale):
    m = functools.reduce(jnp.maximum, [jnp.max(s, axis=-1, keepdims=True) for s in scores])
    ps = [jnp.exp2((s - m) * (scale * LOG2E)) for s in scores]
    inv = 1.0 / functools.reduce(jnp.add, [jnp.sum(p, axis=-1, keepdims=True) for p in ps])
    return ps, inv


def _pv(ps, values):
    return functools.reduce(jnp.add, [_dot(p.astype(BF16), v) for p, v in zip(ps, values)])


def _diff_body(*refs, latent, lam_init, has_prev, heads):
    it = iter(refs)
    lam_ref, q_ref, k_ref, v_ref, g_ref = [next(it) for _ in range(5)]
    if latent:
        kc_ref, vc_ref, cq_ref, sq_ref, ck_ref, sk_ref = [next(it) for _ in range(6)]
    if has_prev:
        next(it)
    o_ref = next(it)
    k_s, v_s = next(it), next(it)
    lam = lam_ref[0, 0]

    def head(ref, hh):
        return ref[:, hh * 128:(hh + 1) * 128]

    @pl.when(pl.program_id(2) == 0)
    def _():
        for hh in range(heads):
            k = head(k_ref, hh)
            if latent:
                k = _rope(k, ck_ref[...], sk_ref[...])
            k_s[:, hh * 128:(hh + 1) * 128] = k.astype(BF16)
        v_s[...] = v_ref[...].astype(BF16)

    lane = lax.broadcasted_iota(jnp.int32, (q_ref.shape[0], 128), 1)
    scale = B_DH ** -0.5
    scores = []
    for hh in range(heads):
        q = head(q_ref, hh)
        keys = [head(k_s, hh)]
        if latent:
            q = _rope(q, cq_ref[...], sq_ref[...])
            keys.insert(0, head(kc_ref, hh).astype(BF16))
        for half in range(2):
            qh = jnp.where((lane >> 6) == half, q, 0.0).astype(BF16)
            scores.append([_dot_nt(qh, kk) for kk in keys])
    parts = [_softmax_parts(s, scale) for s in scores]
    for hh in range(heads):
        values = [head(v_s, hh)]
        if latent:
            values.insert(0, head(vc_ref, hh).astype(BF16))
        (p1, inv1), (p2, inv2) = parts[2 * hh], parts[2 * hh + 1]
        o = _pv(p1, values) * inv1 - _pv(p2, values) * (lam * inv2)
        o = o * lax.rsqrt(jnp.mean(o * o, axis=-1, keepdims=True) + RMS_EPS) * g_ref[...] * (1.0 - lam_init)
        o_ref[:, hh * B_DV:(hh + 1) * B_DV] = o.astype(o_ref.dtype)


def _diff_attn(pd, lam, diff_g, lam_init, row0, n_seq, t_len, cache=None, rope=None, bq=256, prev=None, heads=2):
    latent = cache is not None
    rb = row0 // t_len
    qb0 = row0 // bq
    nqb = t_len // bq
    gw = heads * 128
    in_specs = [
        pl.BlockSpec(memory_space=pltpu.SMEM),
        pl.BlockSpec((bq, gw), lambda n, h, b: (qb0 + n * nqb + b, PD_Q // gw + h)),
        pl.BlockSpec((t_len, gw), lambda n, h, b: (rb + n, PD_K // gw + h)),
        pl.BlockSpec((t_len, gw), lambda n, h, b: (rb + n, PD_V // gw + h)),
        pl.BlockSpec((1, B_DV), lambda n, h, b: (0, 0)),
    ]
    args = [lam, pd, pd, pd, diff_g]
    if latent:
        kc, vc, layer = cache
        cos_t, sin_t = rope
        in_specs += [
            pl.BlockSpec((None, None, PAST_LEN, gw), lambda n, h, b: (n, layer, 0, h)),
            pl.BlockSpec((None, None, PAST_LEN, gw), lambda n, h, b: (n, layer, 0, h)),
            pl.BlockSpec((bq, 128), lambda n, h, b: (b, 0)),
            pl.BlockSpec((bq, 128), lambda n, h, b: (b, 0)),
            pl.BlockSpec((t_len, 128), lambda n, h, b: (0, 0)),
            pl.BlockSpec((t_len, 128), lambda n, h, b: (0, 0)),
        ]
        args += [kc, vc, cos_t, sin_t, cos_t, sin_t]
    aliases = _alias_prev(in_specs, args, prev)
    return pl.pallas_call(
        functools.partial(_diff_body, latent=latent, lam_init=lam_init, has_prev=prev is not None, heads=heads),
        grid=(n_seq, B_HEADS // heads, nqb),
        in_specs=in_specs,
        out_specs=pl.BlockSpec((bq, gw), lambda n, h, b: (qb0 + n * nqb + b, h)),
        out_shape=jax.ShapeDtypeStruct((NTOK, B_HEADS * B_DV), BF16),
        input_output_aliases=aliases,
        scratch_shapes=[pltpu.VMEM((t_len, gw), BF16), pltpu.VMEM((t_len, gw), BF16)],
        compiler_params=_params(("arbitrary", "arbitrary", "arbitrary"), 40),
        name="diff_lat" if latent else "diff_ctx",
    )(*args)


def _mla_body(*refs, latent, has_prev, heads):
    it = iter(refs)
    qn_ref, qr_ref, kv_ref, kr_ref = [next(it) for _ in range(4)]
    if latent:
        kvc_ref, krc_ref, cq_ref, sq_ref, ck_ref, sk_ref = [next(it) for _ in range(6)]
    if has_prev:
        next(it)
    o_ref = next(it)
    kr_s = next(it)

    @pl.when((pl.program_id(1) == 0) & (pl.program_id(2) == 0))
    def _():
        kr = kr_ref[...]
        if latent:
            kr = _rope(kr, ck_ref[...], sk_ref[...])
        kr_s[...] = kr.astype(BF16)

    scale = (C_NOPE + C_ROPE) ** -0.5
    lane = lax.broadcasted_iota(jnp.int32, (qr_ref.shape[0], 128), 1)
    kr_new = kr_s[...]
    kr_old = krc_ref[...].astype(BF16) if latent else None
    scores = []
    for hh in range(heads):
        if hh % 2 == 0:
            qr = qr_ref[:, (hh // 2) * 128:(hh // 2 + 1) * 128]
            if latent:
                qr = _rope(qr, cq_ref[...], sq_ref[...])
        qcat = jnp.concatenate([qn_ref[:, hh * C_NOPE:(hh + 1) * C_NOPE],
                                jnp.where((lane >> 6) == hh % 2, qr, 0.0).astype(BF16)], axis=1)
        ks = [jnp.concatenate([kv_ref[:, 2 * hh * C_NOPE:(2 * hh + 1) * C_NOPE], kr_new], axis=1)]
        if latent:
            ks.insert(0, jnp.concatenate([kvc_ref[:, 2 * hh * C_NOPE:(2 * hh + 1) * C_NOPE], kr_old], axis=1))
        scores.append([_dot_nt(qcat, kk) for kk in ks])
    parts = [_softmax_parts(s, scale) for s in scores]
    for hh in range(heads):
        vs = [kv_ref[:, (2 * hh + 1) * C_NOPE:(2 * hh + 2) * C_NOPE]]
        if latent:
            vs.insert(0, kvc_ref[:, (2 * hh + 1) * C_NOPE:(2 * hh + 2) * C_NOPE])
        ps, inv = parts[hh]
        o_ref[:, hh * C_DV:(hh + 1) * C_DV] = (_pv(ps, vs) * inv).astype(o_ref.dtype)


def _mla_attn(qn, qr, kv, kr2, row0, n_seq, t_len, cache=None, rope=None, bq=256, prev=None, heads=2):
    latent = cache is not None
    rb = row0 // t_len
    qb0 = row0 // bq
    nqb = t_len // bq
    group_kv = heads * (C_NOPE + C_DV)
    in_specs = [
        pl.BlockSpec((bq, heads * C_NOPE), lambda n, h, b: (qb0 + n * nqb + b, h)),
        pl.BlockSpec((bq, heads * C_ROPE), lambda n, h, b: (qb0 + n * nqb + b, h)),
        pl.BlockSpec((t_len, group_kv), lambda n, h, b: (rb + n, h)),
        pl.BlockSpec((t_len, 128), lambda n, h, b: (rb + n, 0)),
    ]
    args = [qn, qr, kv, kr2]
    if latent:
        kvc, krc = cache
        cos_t, sin_t = rope
        in_specs += [
            pl.BlockSpec((PAST_LEN, group_kv), lambda n, h, b: (n, h)),
            pl.BlockSpec((PAST_LEN, 128), lambda n, h, b: (n, 0)),
            pl.BlockSpec((bq, 128), lambda n, h, b: (b, 0)),
            pl.BlockSpec((bq, 128), lambda n, h, b: (b, 0)),
            pl.BlockSpec((t_len, 128), lambda n, h, b: (0, 0)),
            pl.BlockSpec((t_len, 128), lambda n, h, b: (0, 0)),
        ]
        args += [kvc, krc, cos_t, sin_t, cos_t, sin_t]
    aliases = _alias_prev(in_specs, args, prev)
    return pl.pallas_call(
        functools.partial(_mla_body, latent=latent, has_prev=prev is not None, heads=heads),
        grid=(n_seq, C_HEADS // heads, nqb),
        in_specs=in_specs,
        out_specs=pl.BlockSpec((bq, heads * C_DV), lambda n, h, b: (qb0 + n * nqb + b, h)),
        out_shape=jax.ShapeDtypeStruct((NTOK, C_HEADS * C_DV), BF16),
        input_output_aliases=aliases,
        scratch_shapes=[pltpu.VMEM((t_len, 128), BF16)],
        compiler_params=_params(("arbitrary", "arbitrary", "arbitrary"), 40),
        name="mla_lat" if latent else "mla_ctx",
    )(*args)


RT_E0, RT_E1, RT_R0, RT_R1, RT_W0, RT_W1 = range(6)
ROUTER_TM = 512


def _router_body(x_ref, sc_ref, sh_ref, w_ref, b_ref, tri_ref, rt_ref, cnt_ref, carry_ref):
    h = x_ref[...] * (1.0 + sc_ref[0, 0]) + sh_ref[0, 0]
    w = w_ref[...]
    h_hi = h.astype(BF16)
    w_hi = w.astype(BF16)
    h_lo = (h - h_hi.astype(F32)).astype(BF16)
    w_lo = (w - w_hi.astype(F32)).astype(BF16)
    lg = _dot(h_hi, w_hi) + (_dot(h_lo, w_hi) + _dot(h_hi, w_lo)) + b_ref[...]

    @pl.when(pl.program_id(0) == 0)
    def _():
        carry_ref[...] = jnp.zeros(carry_ref.shape, F32)

    lane = lax.broadcasted_iota(jnp.int32, lg.shape, 1)
    ninf = -jnp.inf
    is_g = lane < N_GROUPS
    gl = jnp.where(is_g, lg, ninf)
    gmax = jnp.max(gl, axis=1, keepdims=True)
    g_sel = jnp.min(jnp.where(gl == gmax, lane, 128), axis=1, keepdims=True)
    g_w = 1.0 / jnp.sum(jnp.where(is_g, jnp.exp(lg - gmax), 0.0), axis=1, keepdims=True)
    el = jnp.where(((lane - N_GROUPS) >> 2) == g_sel, lg, ninf)
    m1 = jnp.max(el, axis=1, keepdims=True)
    i1 = jnp.min(jnp.where(el == m1, lane, 128), axis=1, keepdims=True)
    el2 = jnp.where(lane == i1, ninf, el)
    m2 = jnp.max(el2, axis=1, keepdims=True)
    i2 = jnp.min(jnp.where(el2 == m2, lane, 128), axis=1, keepdims=True)
    p2 = jnp.exp(m2 - m1)
    w0 = g_w / (1.0 + p2)
    w1 = w0 * p2
    e0 = i1 - N_GROUPS
    e1 = i2 - N_GROUPS
    hit0 = lane == e0
    hit1 = lane == e1
    onehot = jnp.where(hit0, 1.0, jnp.where(hit1, 1.0, 0.0))
    before = _dot(tri_ref[...], onehot.astype(BF16)) + carry_ref[...]
    r0 = jnp.sum(jnp.where(hit0, before, 0.0), axis=1, keepdims=True)
    r1 = jnp.sum(jnp.where(hit1, before, 0.0), axis=1, keepdims=True)
    carry = carry_ref[...] + jnp.sum(onehot, axis=0, keepdims=True)
    carry_ref[...] = carry
    cnt_ref[...] = jnp.broadcast_to(carry, cnt_ref.shape)
    rec = [e0.astype(F32), e1.astype(F32), r0, r1, w0, w1]
    out = jnp.zeros(lg.shape, F32)
    for idx, val in enumerate(rec):
        out = jnp.where(lane == idx, val, out)
    rt_ref[...] = out


def _router(x, mod4, layer, w_r, b_r, tri):
    tm = ROUTER_TM
    return pl.pallas_call(
        _router_body,
        grid=(NTOK // tm,),
        in_specs=[pl.BlockSpec((tm, D_MODEL), lambda i: (i, 0)),
                  _mod_spec(layer, 4, tm), _mod_spec(layer, 3, tm),
                  pl.BlockSpec((D_MODEL, 128), lambda i: (0, 0)),
                  pl.BlockSpec((1, 128), lambda i: (0, 0)),
                  pl.BlockSpec((tm, tm), lambda i: (0, 0))],
        out_specs=[pl.BlockSpec((tm, 128), lambda i: (i, 0)),
                   pl.BlockSpec((8, 128), lambda i: (0, 0))],
        out_shape=[jax.ShapeDtypeStruct((NTOK, 128), F32), jax.ShapeDtypeStruct((8, 128), F32)],
        scratch_shapes=[pltpu.VMEM((1, 128), F32)],
        compiler_params=_params(("arbitrary",), 40),
        name="moe_router",
    )(x, mod4, mod4, w_r, b_r, tri)


DISPATCH_TM = 512


def _dispatch_body(dest_ref, pend_ref, padded_ref, x_ref, sc_ref, sh_ref, xs_hbm, hbuf, zbuf, sem, zsem):
    tm = DISPATCH_TM
    i = pl.program_id(0)

    @pl.when(i == 0)
    def _():
        zbuf[...] = jnp.zeros(zbuf.shape, F32)

        def tail(e):
            start = pl.multiple_of((pend_ref[e] - MOE_TM) // SUB, MOE_TM // SUB)
            return pltpu.make_async_copy(zbuf, xs_hbm.at[pl.ds(start, MOE_TM // SUB)], zsem)

        for e in range(N_EXPERTS):
            @pl.when(padded_ref[e] > 0)
            def _():
                tail(e).start()
        for e in range(N_EXPERTS):
            @pl.when(padded_ref[e] > 0)
            def _():
                tail(e).wait()

    h = x_ref[...] * (1.0 + sc_ref[0, 0]) + sh_ref[0, 0]
    hbuf[...] = h.reshape(tm // SUB, SUB, D_MODEL)

    def send(g, carry):
        for s in range(SUB):
            for k in range(EXPERT_TOP_K):
                d = dest_ref[k * NTOK + i * tm + g * SUB + s]
                pltpu.make_async_copy(hbuf.at[g, pl.ds(s, 1), :], _row(xs_hbm, d), sem).start()
        return carry
    lax.fori_loop(0, tm // SUB, send, 0)

    def drain(t, carry):
        for k in range(EXPERT_TOP_K):
            pltpu.make_async_copy(hbuf.at[0, pl.ds(0, 1), :], _row(xs_hbm, 0), sem).wait()
        return carry
    lax.fori_loop(0, tm, drain, 0, unroll=8)


def _dispatch(x, dest, pend, padded, mod4, layer):
    tm = DISPATCH_TM
    grid_spec = pltpu.PrefetchScalarGridSpec(
        num_scalar_prefetch=3,
        grid=(NTOK // tm,),
        in_specs=[pl.BlockSpec((tm, D_MODEL), lambda i, *_: (i, 0)),
                  _mod_spec(layer, 4, tm), _mod_spec(layer, 3, tm)],
        out_specs=pl.BlockSpec(memory_space=pl.ANY),
        scratch_shapes=[pltpu.VMEM((tm // SUB, SUB, D_MODEL), F32), pltpu.VMEM((MOE_TM // SUB, SUB, D_MODEL), F32),
                        pltpu.SemaphoreType.DMA(()), pltpu.SemaphoreType.DMA(())],
    )
    xs = pl.pallas_call(
        _dispatch_body,
        grid_spec=grid_spec,
        out_shape=jax.ShapeDtypeStruct((MOE_ROWS // SUB, SUB, D_MODEL), F32),
        compiler_params=_params(("arbitrary",), 40),
        name="moe_dispatch",
    )(dest, pend, padded, x, mod4, mod4)
    return xs.reshape(MOE_ROWS, D_MODEL)


def _expert_body(te_ref, tv_ref, x_ref, wg_ref, wu_ref, wd_ref, o_ref, wg_s, wu_s, wd_s):
    t = pl.program_id(0)
    prev = te_ref[jnp.maximum(t - 1, 0)]

    @pl.when((t == 0) | (te_ref[t] != prev))
    def _():
        wg_s[...] = wg_ref[...].astype(BF16)
        wu_s[...] = wu_ref[...].astype(BF16)
        wd_s[...] = wd_ref[...].astype(BF16)

    @pl.when(tv_ref[t] > 0)
    def _():
        x = x_ref[...].astype(BF16)
        g = _dot(x, wg_s[...])
        u = _dot(x, wu_s[...])
        act = (g / (1.0 + jnp.exp(-g))) * u
        o_ref[...] = _dot(act.astype(BF16), wd_s[...])

    @pl.when(tv_ref[t] == 0)
    def _():
        o_ref[...] = jnp.zeros(o_ref.shape, o_ref.dtype)


def _experts(xs, tile_e, tile_v, w_gate, w_up, w_down, layer):
    grid_spec = pltpu.PrefetchScalarGridSpec(
        num_scalar_prefetch=2,
        grid=(MOE_TILES,),
        in_specs=[
            pl.BlockSpec((MOE_TM, D_MODEL), lambda t, te, tv: (jnp.where(tv[t] > 0, t, 0), 0)),
            pl.BlockSpec((None, None, D_MODEL, D_EXPERT), lambda t, te, tv: (layer, te[t], 0, 0)),
            pl.BlockSpec((None, None, D_MODEL, D_EXPERT), lambda t, te, tv: (layer, te[t], 0, 0)),
            pl.BlockSpec((None, None, D_EXPERT, D_MODEL), lambda t, te, tv: (layer, te[t], 0, 0)),
        ],
        out_specs=pl.BlockSpec((MOE_TM, D_MODEL), lambda t, te, tv: (t, 0)),
        scratch_shapes=[pltpu.VMEM((D_MODEL, D_EXPERT), BF16), pltpu.VMEM((D_MODEL, D_EXPERT), BF16),
                        pltpu.VMEM((D_EXPERT, D_MODEL), BF16)],
    )
    return pl.pallas_call(
        _expert_body,
        grid_spec=grid_spec,
        out_shape=jax.ShapeDtypeStruct((MOE_ROWS, D_MODEL), F32),
        compiler_params=_params(("arbitrary",), 56),
        name="moe_experts",
    )(tile_e, tile_v, xs, w_gate, w_up, w_down)


def _layout(route, counts):
    counts = counts[0, :N_EXPERTS].astype(jnp.int32)
    padded = ((counts + MOE_TM - 1) // MOE_TM) * MOE_TM
    pend = jnp.cumsum(padded)
    pstart = pend - padded
    experts = jnp.arange(N_EXPERTS, dtype=jnp.int32)[None, :]

    def dest_of(e_lane, r_lane):
        e = route[:, e_lane].astype(jnp.int32)
        base = jnp.sum(jnp.where(e[:, None] == experts, pstart[None, :], 0), axis=1)
        return base + route[:, r_lane].astype(jnp.int32)

    dest = jnp.concatenate([dest_of(RT_E0, RT_R0), dest_of(RT_E1, RT_R1)])
    tile_start = jnp.arange(MOE_TILES, dtype=jnp.int32) * MOE_TM
    tile_e = jnp.minimum(jnp.sum((tile_start[:, None] >= pend[None, :]).astype(jnp.int32), axis=1), N_EXPERTS - 1)
    tile_v = (tile_start < pend[-1]).astype(jnp.int32)
    return dest, pend.astype(jnp.int32), padded, tile_e.astype(jnp.int32), tile_v


def _moe(x, mod4, layer, w_r, b_r, tri, w_gate, w_up, w_down, ln_g, ln_b):
    route, counts = _router(x, mod4, layer, w_r, b_r, tri)
    dest, pend, padded, tile_e, tile_v = _layout(route, counts)
    xs = _dispatch(x, dest, pend, padded, mod4, layer)
    rows = _experts(xs, tile_e, tile_v, w_gate, w_up, w_down, layer)
    return _combine_ln(x, rows, dest, route, mod4, layer, 5, ln_g, ln_b)


def kernel(x_prompt, x_sample, state_gla, cache_diff_k, cache_diff_v, cache_mla_ckv, cache_mla_krope, c, c_ctx, ada_w, ada_b, ln_g, ln_b, ab_w_in, gla_w_gate2, gla_b_gate2, gla_norm_g, diff_lambda, diff_norm_g, ab_w_out, mla_w_in, mla_q_norm_g, mla_w_uq, mla_kv_norm_g, mla_w_ukv, mla_w_out, moe_w_rg, moe_b_rg, moe_w_re, moe_b_re, moe_w_gate, moe_w_up, moe_w_down):
    x = jnp.concatenate([x_prompt.reshape(NCTX, D_MODEL), x_sample.reshape(NLAT, D_MODEL)], axis=0)
    cond = jnp.concatenate([c_ctx[None, :], c, jnp.zeros((N_COND - 1 - DEC_BATCH, D_MODEL), F32)], axis=0)
    mod4 = _ada_mod(cond, ada_w, ada_b).reshape(DEPTH, N_COND, 1, 6 * D_MODEL)
    cos_t, sin_t = _rope_tables()
    levf = jnp.asarray(_gla_level_table(False))
    levb = jnp.asarray(_gla_level_table(True))
    tri = jnp.asarray(np.tril(np.ones((ROUTER_TM, ROUTER_TM), np.float32), -1), BF16)
    cache_k = cache_diff_k.reshape(DEC_BATCH, -1, PAST_LEN, B_HEADS * 2 * B_DH)
    cache_v = cache_diff_v.reshape(DEC_BATCH, -1, PAST_LEN, B_HEADS * B_DV)

    new_gla, new_dk, new_dv, new_ckv, new_kr = [], [], [], [], []
    for layer in range(DEPTH):
        i = layer // 2
        if layer % 2 == 0:
            w_d = jnp.concatenate([ab_w_in[i, :, AB_D0:], ab_w_in[i, :, AB_GG0:AB_D0],
                                   jnp.zeros((D_MODEL, 128 - 2 * A_GATE_RANK), F32)], axis=1)[None]
            pa = _mod_matmul(x, mod4, layer, ab_w_in, i, PA_N, 1024, "ab_in_gla")
            pd = _mod_matmul(x, mod4, layer, w_d, 0, PD_N, 640, "ab_in_diff")
            wg = jnp.zeros((A_HEADS, 128, 2 * A_DK), F32)
            for j in range(2):
                blk = gla_w_gate2[i, j].reshape(A_GATE_RANK, A_HEADS, A_DK).transpose(1, 0, 2)
                wg = wg.at[:, j * A_GATE_RANK:(j + 1) * A_GATE_RANK, j * A_DK:(j + 1) * A_DK].set(blk)
            bg = gla_b_gate2[i].reshape(2, A_HEADS, A_DK).transpose(1, 0, 2).reshape(A_HEADS, 1, 2 * A_DK)
            g_gla = gla_norm_g[i][None, :]
            o_gla_c, st = _gla(pa, pd, wg, bg, g_gla, levf, levb, 0, BATCH, SEQ, None, 0, True)
            (o_gla,) = _gla(pa, pd, wg, bg, g_gla, levf, levb, NCTX, DEC_BATCH, DEC_SEQ, state_gla, i, False,
                            prev=o_gla_c)
            lv = diff_lambda[i]
            lam_init = 0.8 - 0.6 * math.exp(-0.3 * layer)
            lam = (jnp.exp(jnp.sum(lv[0] * lv[1])) - jnp.exp(jnp.sum(lv[2] * lv[3])) + lam_init).reshape(1, 1)
            g_diff = diff_norm_g[i][None, :]
            o_diff_c = _diff_attn(pd, lam, g_diff, lam_init, 0, BATCH, SEQ, heads=8)
            o_diff = _diff_attn(pd, lam, g_diff, lam_init, NCTX, DEC_BATCH, DEC_SEQ,
                                cache=(cache_k, cache_v, i), rope=(cos_t, sin_t), bq=DEC_SEQ, prev=o_diff_c)
            x = _proj_ln([o_gla, o_diff], ab_w_out, i, x, mod4, layer, 2,
                         ln_g[layer, 0][None, :], ln_b[layer, 0][None, :], "ab_out_ln")
            new_gla.append(st)
            new_dk.append(pd[:NCTX, PD_K:PD_V].reshape(BATCH, SEQ, B_HEADS, 2, B_DH))
            new_dv.append(pd[:NCTX, PD_V:PD_GG].reshape(BATCH, SEQ, B_HEADS, B_DV))
        else:
            c_in = C_Q_RANK + C_KV_RANK + C_ROPE
            pm = _mod_matmul(x, mod4, layer, mla_w_in, i, c_in, c_in, "mla_in")
            w_uq = mla_w_uq[i].reshape(C_Q_RANK, C_HEADS, C_NOPE + C_ROPE)
            w_uq = jnp.concatenate([w_uq[:, :, :C_NOPE].reshape(C_Q_RANK, -1),
                                    w_uq[:, :, C_NOPE:].reshape(C_Q_RANK, -1)], axis=1)
            g_q = mla_q_norm_g[i][None, :]
            g_kv = mla_kv_norm_g[i][None, :]
            qn = _rms_matmul(pm, 0, C_Q_RANK, g_q, w_uq, 0, C_HEADS * C_NOPE, 512, 1024, BF16, "mla_q_nope")
            qr = _rms_matmul(pm, 0, C_Q_RANK, g_q, w_uq, C_HEADS * C_NOPE // 512, C_HEADS * C_ROPE, 512, 1024, F32,
                             "mla_q_rope")
            kv, ckv_n = _rms_matmul(pm, C_Q_RANK // C_KV_RANK, C_KV_RANK, g_kv, mla_w_ukv[i], 0,
                                    C_HEADS * (C_NOPE + C_DV), 1024, 1024, BF16, "mla_kv", emit_norm=True)
            kv_c = _rms_matmul(cache_mla_ckv[:, i].reshape(DEC_BATCH * PAST_LEN, C_KV_RANK), 0, C_KV_RANK, g_kv,
                               mla_w_ukv[i], 0, C_HEADS * (C_NOPE + C_DV), 1024, 1024, BF16, "mla_kv_cache",
                               norm=False)
            krope = pm[:, C_Q_RANK + C_KV_RANK:]
            kr2 = jnp.concatenate([krope, krope], axis=1)
            krc = cache_mla_krope[:, i].reshape(DEC_BATCH * PAST_LEN, C_ROPE)
            krc2 = jnp.concatenate([krc, krc], axis=1)
            o_c = _mla_attn(qn, qr, kv, kr2, 0, BATCH, SEQ, heads=8)
            o_mla = _mla_attn(qn, qr, kv, kr2, NCTX, DEC_BATCH, DEC_SEQ, cache=(kv_c, krc2), rope=(cos_t, sin_t),
                              bq=DEC_SEQ, prev=o_c, heads=2)
            x = _proj_ln([o_mla], mla_w_out, i, x, mod4, layer, 2,
                         ln_g[layer, 0][None, :], ln_b[layer, 0][None, :], "mla_out_ln")
            new_ckv.append(ckv_n[:NCTX].reshape(BATCH, SEQ, C_KV_RANK))
            new_kr.append(krope[:NCTX].reshape(BATCH, SEQ, C_ROPE))
        w_r = jnp.concatenate([moe_w_rg[layer], moe_w_re[layer],
                               jnp.zeros((D_MODEL, 128 - N_GROUPS - N_EXPERTS), F32)], axis=1)
        b_r = jnp.concatenate([moe_b_rg[layer], moe_b_re[layer],
                               jnp.zeros((128 - N_GROUPS - N_EXPERTS,), F32)])[None, :]
        x = _moe(x, mod4, layer, w_r, b_r, tri, moe_w_gate, moe_w_up, moe_w_down,
                 ln_g[layer, 1][None, :], ln_b[layer, 1][None, :])

    return (x[:NCTX].reshape(BATCH, SEQ, D_MODEL),
            x[NCTX:].reshape(DEC_BATCH, DEC_SEQ, D_MODEL),
            jnp.stack(new_gla, axis=1), jnp.stack(new_dk, axis=1), jnp.stack(new_dv, axis=1),
            jnp.stack(new_ckv, axis=1), jnp.stack(new_kr, axis=1))
```

```python
import functools
import math

import numpy as np
import jax
import jax.numpy as jnp
from jax import lax
from jax.experimental import pallas as pl
from jax.experimental.pallas import tpu as pltpu

F32 = jnp.float32
BF16 = jnp.bfloat16

D_MODEL = 2048
BATCH = 16
SEQ = 256
DEPTH = 4
DEC_BATCH = 4
DEC_SEQ = 1024
PAST_LEN = 256
GRID_W = 64
A_HEADS = 4
A_DK = 128
A_DV = 256
A_GATE_RANK = 16
A_TAU = 16.0
B_HEADS = 8
B_DH = 64
B_DV = 2 * B_DH
C_HEADS = 16
C_Q_RANK = 512
C_KV_RANK = 256
C_NOPE = 128
C_ROPE = 64
C_DV = 128
N_GROUPS = 4
EXPERTS_PER_GROUP = 4
N_EXPERTS = N_GROUPS * EXPERTS_PER_GROUP
EXPERT_TOP_K = 2
D_EXPERT = 512
ROPE_THETA = 10000.0
LN_EPS = 1e-5
RMS_EPS = 1e-6
DEEPNORM_ALPHA = (2.0 * DEPTH) ** 0.25

NCTX = BATCH * SEQ
NLAT = DEC_BATCH * DEC_SEQ
NTOK = NCTX + NLAT
N_COND = 8
GLA_CHUNK = 256
GLA_LEVELS = 8
MOE_TM = 512
LOG2E = 1.4426950408889634
MOE_ROWS = NTOK * EXPERT_TOP_K + N_EXPERTS * MOE_TM
MOE_TILES = MOE_ROWS // MOE_TM

PA_Q, PA_K, PA_V, PA_R = 0, A_HEADS * A_DK, 2 * A_HEADS * A_DK, 2 * A_HEADS * A_DK + A_HEADS * A_DV
PA_N = PA_R + A_HEADS * A_DV
AB_GG0 = PA_N
AB_D0 = PA_N + 2 * A_GATE_RANK
PD_Q = PA_N
PD_K = PD_Q + B_HEADS * 2 * B_DH
PD_V = PD_K + B_HEADS * 2 * B_DH
PD_GG = PD_V + B_HEADS * B_DV
P_N = PD_GG + 128


def _params(sem, vmem_mb):
    return pltpu.CompilerParams(dimension_semantics=sem, vmem_limit_bytes=vmem_mb * 2 ** 20)


def _cond_row(i, tm):
    return jnp.where(i * tm < NCTX, 0, 1 + (i * tm - NCTX) // DEC_SEQ)


def _dot(a, b):
    return jnp.dot(a, b, preferred_element_type=F32)


def _dot_nt(a, b):
    return lax.dot_general(a, b, (((1,), (1,)), ((), ())), preferred_element_type=F32)


def _dot_tn(a, b):
    return lax.dot_general(a, b, (((0,), (0,)), ((), ())), preferred_element_type=F32)


def _ada_body(c_ref, w_ref, b_ref, o_ref):
    c = c_ref[...]
    a = (c / (1.0 + jnp.exp(-c))).astype(BF16)
    o_ref[0] = _dot(a, w_ref[0].astype(BF16)) + b_ref[0]


def _ada_mod(cond, ada_w, ada_b):
    tn = 1024
    return pl.pallas_call(
        _ada_body,
        grid=(DEPTH, 6 * D_MODEL // tn),
        in_specs=[pl.BlockSpec((N_COND, D_MODEL), lambda l, j: (0, 0)),
                  pl.BlockSpec((1, D_MODEL, tn), lambda l, j: (l, 0, j)),
                  pl.BlockSpec((1, 1, tn), lambda l, j: (l, 0, j))],
        out_specs=pl.BlockSpec((1, N_COND, tn), lambda l, j: (l, 0, j)),
        out_shape=jax.ShapeDtypeStruct((DEPTH, N_COND, 6 * D_MODEL), F32),
        compiler_params=_params(("arbitrary", "arbitrary"), 40),
        name="ada_mod",
    )(cond, ada_w, ada_b.reshape(DEPTH, 1, 6 * D_MODEL))


def _mod_spec(layer, chunk, tm):
    return pl.BlockSpec((1, 1, 1, D_MODEL), lambda i, *_: (layer, _cond_row(i, tm), 0, chunk))


def _stream_specs(x, tm):
    if not isinstance(x, tuple):
        return [pl.BlockSpec((tm, D_MODEL), lambda i, *_: (i, 0))], [x]
    n_c = NCTX // tm
    return ([pl.BlockSpec((tm, D_MODEL), lambda i, *_: (jnp.minimum(i, n_c - 1), 0)),
             pl.BlockSpec((tm, D_MODEL), lambda i, *_: (jnp.maximum(i - n_c, 0), 0))], list(x))


def _for_stream(x_refs, tm, fn):
    if len(x_refs) == 1:
        fn(x_refs[0])
        return
    n_c = NCTX // tm
    pl.when(pl.program_id(0) < n_c)(lambda: fn(x_refs[0]))
    pl.when(pl.program_id(0) >= n_c)(lambda: fn(x_refs[1]))


def _mod_mm_body(*refs, n_x, tm):
    x_refs = refs[:n_x]
    sc_ref, sh_ref, w_ref, o_ref, h_ref = refs[n_x:]

    @pl.when(pl.program_id(1) == 0)
    def _():
        def modulate(x_ref):
            h_ref[...] = (x_ref[...] * (1.0 + sc_ref[0, 0]) + sh_ref[0, 0]).astype(BF16)
        _for_stream(x_refs, tm, modulate)

    o_ref[...] = _dot(h_ref[...], w_ref[...]).astype(o_ref.dtype)


def _mod_matmul(x, mod4, layer, w, w_layer, n_out, tn, name, tm=1024, out_dtype=F32):
    x_specs, x_args = _stream_specs(x, tm)
    return pl.pallas_call(
        functools.partial(_mod_mm_body, n_x=len(x_args), tm=tm),
        grid=(NTOK // tm, n_out // tn),
        in_specs=x_specs + [_mod_spec(layer, 1, tm), _mod_spec(layer, 0, tm),
                            pl.BlockSpec((None, D_MODEL, tn), lambda i, j: (w_layer, 0, j))],
        out_specs=pl.BlockSpec((tm, tn), lambda i, j: (i, j)),
        out_shape=jax.ShapeDtypeStruct((NTOK, n_out), out_dtype),
        scratch_shapes=[pltpu.VMEM((tm, D_MODEL), BF16)],
        compiler_params=_params(("arbitrary", "arbitrary"), 56),
        name=name,
    )(*x_args, mod4, mod4, w)


def _rms_mm_body(x_ref, g_ref, w_ref, *rest, norm, emit_norm):
    if emit_norm:
        o_ref, n_ref, h_ref = rest
    else:
        o_ref, h_ref = rest

    @pl.when(pl.program_id(1) == 0)
    def _():
        x = x_ref[...]
        if norm:
            x = x * lax.rsqrt(jnp.mean(x * x, axis=-1, keepdims=True) + RMS_EPS) * g_ref[...]
        h_ref[...] = x.astype(BF16)
        if emit_norm:
            n_ref[...] = x

    o_ref[...] = _dot(h_ref[...], w_ref[...]).astype(o_ref.dtype)


def _rms_matmul(x, col_block, k, g, w, col0_blocks, n_out, tn, tm, out_dtype, name, norm=True, emit_norm=False):
    rows = x.shape[0]
    out_shape = [jax.ShapeDtypeStruct((rows, n_out), out_dtype)]
    out_specs = [pl.BlockSpec((tm, tn), lambda i, j: (i, j))]
    if emit_norm:
        out_shape.append(jax.ShapeDtypeStruct((rows, k), F32))
        out_specs.append(pl.BlockSpec((tm, k), lambda i, j: (i, 0)))
    res = pl.pallas_call(
        functools.partial(_rms_mm_body, norm=norm, emit_norm=emit_norm),
        grid=(rows // tm, n_out // tn),
        in_specs=[pl.BlockSpec((tm, k), lambda i, j: (i, col_block)),
                  pl.BlockSpec((1, k), lambda i, j: (0, 0)),
                  pl.BlockSpec((k, tn), lambda i, j: (0, col0_blocks + j))],
        out_specs=out_specs,
        out_shape=out_shape,
        scratch_shapes=[pltpu.VMEM((tm, k), BF16)],
        compiler_params=_params(("arbitrary", "arbitrary"), 40),
        name=name,
    )(x, g, w)
    return res if emit_norm else res[0]


def _layer_norm(z, g, b):
    mu = jnp.mean(z, axis=-1, keepdims=True)
    zc = z - mu
    var = jnp.mean(zc * zc, axis=-1, keepdims=True)
    return zc * lax.rsqrt(var + LN_EPS) * g + b


LN_ROWS = 256


def _proj_ln_body(*refs, n_lhs, n_x, n_j, tm, tn):
    lhs = refs[:n_lhs]
    ws = refs[n_lhs:2 * n_lhs]
    x_refs = refs[2 * n_lhs:2 * n_lhs + n_x]
    gate_ref, g_ref, b_ref, o_ref = refs[2 * n_lhs + n_x:]
    j = pl.program_id(1)
    y = _dot(lhs[0][...], ws[0][...])
    for a, w in zip(lhs[1:], ws[1:]):
        y = y + _dot(a[...], w[...])
    o_ref[:, pl.ds(pl.multiple_of(j * tn, tn), tn)] = y

    @pl.when(j == n_j - 1)
    def _():
        def normalise(x_ref):
            def body(r, carry):
                rows = pl.ds(pl.multiple_of(r * LN_ROWS, LN_ROWS), LN_ROWS)
                z = DEEPNORM_ALPHA * x_ref[rows, :] + gate_ref[0, 0] * o_ref[rows, :]
                o_ref[rows, :] = _layer_norm(z, g_ref[...], b_ref[...])
                return carry
            lax.fori_loop(0, tm // LN_ROWS, body, 0)
        _for_stream(x_refs, tm, normalise)


def _proj_ln(lhs_list, w, w_layer, x, mod4, layer, gate_chunk, ln_g, ln_b, name, tm=1024, tn=512):
    n_lhs = len(lhs_list)
    kk = lhs_list[0].shape[1]
    n_j = D_MODEL // tn
    in_specs = [pl.BlockSpec((tm, kk), lambda i, j: (i, 0)) for _ in lhs_list]
    in_specs += [pl.BlockSpec((None, kk, tn), functools.partial(lambda i, j, r: (w_layer, r, j), r=r))
                 for r in range(n_lhs)]
    x_specs, x_args = _stream_specs(x, tm)
    in_specs += x_specs + [_mod_spec(layer, gate_chunk, tm),
                           pl.BlockSpec((1, D_MODEL), lambda i, j: (0, 0)),
                           pl.BlockSpec((1, D_MODEL), lambda i, j: (0, 0))]
    return pl.pallas_call(
        functools.partial(_proj_ln_body, n_lhs=n_lhs, n_x=len(x_args), n_j=n_j, tm=tm, tn=tn),
        grid=(NTOK // tm, n_j),
        in_specs=in_specs,
        out_specs=pl.BlockSpec((tm, D_MODEL), lambda i, j: (i, 0)),
        out_shape=jax.ShapeDtypeStruct((NTOK, D_MODEL), F32),
        compiler_params=_params(("arbitrary", "arbitrary"), 56),
        name=name,
    )(*lhs_list, *([w] * n_lhs), *x_args, mod4, ln_g, ln_b)


COMBINE_TM = 256


SUB = 8


def _row(ref, i):
    return ref.at[i >> 3, pl.ds(i & (SUB - 1), 1), :]


def _combine_ln_body(dest_ref, x_ref, rt_ref, gate_ref, g_ref, b_ref, rows_hbm, o_ref, buf, sem):
    tm = COMBINE_TM
    i = pl.program_id(0)
    n_i = pl.num_programs(0)

    def fetch(step, slot):
        def body(g, carry):
            for s in range(SUB):
                for k in range(EXPERT_TOP_K):
                    d = dest_ref[k * NTOK + step * tm + g * SUB + s]
                    pltpu.make_async_copy(_row(rows_hbm, d), buf.at[slot, k, g, pl.ds(s, 1), :],
                                          sem.at[slot]).start()
            return carry
        lax.fori_loop(0, tm // SUB, body, 0)

    @pl.when(i == 0)
    def _():
        fetch(0, 0)

    @pl.when(i + 1 < n_i)
    def _():
        fetch(i + 1, (i + 1) % 2)

    slot = i % 2

    def drain(t, carry):
        for k in range(EXPERT_TOP_K):
            pltpu.make_async_copy(_row(rows_hbm, 0), buf.at[slot, k, 0, pl.ds(0, 1), :], sem.at[slot]).wait()
        return carry
    lax.fori_loop(0, tm, drain, 0, unroll=8)

    rt = rt_ref[...]
    y0 = buf[slot, 0].reshape(tm, D_MODEL)
    y1 = buf[slot, 1].reshape(tm, D_MODEL)
    y = rt[:, RT_W0:RT_W0 + 1] * y0 + rt[:, RT_W1:RT_W1 + 1] * y1
    z = DEEPNORM_ALPHA * x_ref[...] + gate_ref[0, 0] * y
    o_ref[...] = _layer_norm(z, g_ref[...], b_ref[...])


def _combine_ln(x, rows, dest, route, mod4, layer, gate_chunk, ln_g, ln_b):
    tm = COMBINE_TM
    row = pl.BlockSpec((tm, D_MODEL), lambda i, d: (i, 0))
    vec = pl.BlockSpec((1, D_MODEL), lambda i, d: (0, 0))
    grid_spec = pltpu.PrefetchScalarGridSpec(
        num_scalar_prefetch=1,
        grid=(NTOK // tm,),
        in_specs=[row, pl.BlockSpec((tm, 128), lambda i, d: (i, 0)),
                  _mod_spec(layer, gate_chunk, tm), vec, vec,
                  pl.BlockSpec(memory_space=pl.ANY)],
        out_specs=row,
        scratch_shapes=[pltpu.VMEM((2, EXPERT_TOP_K, tm // SUB, SUB, D_MODEL), F32),
                        pltpu.SemaphoreType.DMA((2,))],
    )
    return pl.pallas_call(
        _combine_ln_body,
        grid_spec=grid_spec,
        out_shape=jax.ShapeDtypeStruct((NTOK, D_MODEL), F32),
        compiler_params=_params(("arbitrary",), 40),
        name="moe_combine_ln",
    )(dest, x, route, mod4, ln_g, ln_b, rows.reshape(MOE_ROWS // SUB, SUB, D_MODEL))


def _gla_level_table(reverse):
    t = np.arange(GLA_CHUNK)[:, None]
    s = np.arange(GLA_CHUNK)[None, :]
    x = t ^ s
    lev = np.where(x > 0, np.floor(np.log2(np.maximum(x, 1))).astype(np.int32), -1)
    live = (t < s) if reverse else (t > s)
    return np.where(live | (t == s), lev, -2).astype(np.int32)


def _anchor_rows(u, level, reverse):
    c = u.shape[0]
    n = c // 8
    u3 = u.reshape(n, 8, 128)
    half = 1 << level
    if level < 3:
        blk = 2 * half
        sub = lax.broadcasted_iota(jnp.int32, (n, 8, 128), 1)
        out = None
        for j in range(8 // blk):
            r = j * blk + (half if reverse else half - 1)
            piece = jnp.broadcast_to(u3[:, r:r + 1, :], (n, 8, 128))
            out = piece if out is None else jnp.where(sub >= j * blk, piece, out)
        return out.reshape(c, 128)
    m = 1 << (level - 2)
    r = 0 if reverse else 7
    e = jnp.broadcast_to(u3[:, r:r + 1, :], (n, 8, 128)).reshape(n // m, m, 8, 128)
    idx = m // 2 if reverse else m // 2 - 1
    return jnp.broadcast_to(e[:, idx:idx + 1], (n // m, m, 8, 128)).reshape(c, 128)


def _gla_intra(q, k, la, lev, reverse):
    rowk = lax.broadcasted_iota(jnp.int32, (GLA_CHUNK, 128), 0)
    scores = jnp.where(lev == -1, _dot_nt(q.astype(BF16), k.astype(BF16)), 0.0)
    u = la
    for level in range(GLA_LEVELS):
        anchor = _anchor_rows(u, level, reverse)
        qa = (q * jnp.exp2(u)).astype(BF16)
        ka = (k * jnp.exp2(jnp.minimum(anchor - u, 0.0))).astype(BF16)
        scores = jnp.where(lev == level, _dot_nt(qa, ka), scores)
        query_side = ((rowk >> level) & 1) == (0 if reverse else 1)
        u = u + jnp.where(query_side, anchor, 0.0)
    return scores, u


def _gla_body(*refs, t_len, has_s0, emit_state, has_prev):
    it = iter(refs)
    q_ref, k_ref, v_ref, r_ref, gg_ref, wg_ref, bg_ref, g_ref, levf_ref, levb_ref = [next(it) for _ in range(10)]
    s0_ref = next(it) if has_s0 else None
    if has_prev:
        next(it)
    o_ref = next(it)
    st_ref = next(it) if emit_state else None
    acc_ref, sf_ref, sb_ref, la_ref = [next(it) for _ in range(4)]
    c = GLA_CHUNK
    n_ch = t_len // c

    gl = _dot(gg_ref[...].astype(BF16), wg_ref[...].astype(BF16)) + bg_ref[...]
    la_ref[...] = (jnp.minimum(gl, 0.0) - jnp.log1p(jnp.exp(-jnp.abs(gl)))) * (LOG2E / A_TAU)
    if has_s0:
        sf_ref[...] = s0_ref[0]
        sb_ref[...] = s0_ref[1]

    eye = (lax.broadcasted_iota(jnp.int32, (A_DK, A_DK), 0) == lax.broadcasted_iota(jnp.int32, (A_DK, A_DK), 1))

    def chunk_step(ci, reverse, with_state):
        rows = pl.ds(ci * c if isinstance(ci, int) else pl.multiple_of(ci * c, c), c)
        q = q_ref[rows, :] * (A_DK ** -0.5)
        k = k_ref[rows, :]
        v = v_ref[rows, :].astype(BF16)
        la = la_ref[rows, A_DK:2 * A_DK] if reverse else la_ref[rows, 0:A_DK]
        s_ref = sb_ref if reverse else sf_ref
        scores, cum = _gla_intra(q, k, la, (levb_ref if reverse else levf_ref)[...], reverse)
        o = _dot(scores.astype(BF16), v)
        if with_state:
            o = o + _dot((q * jnp.exp2(cum)).astype(BF16), s_ref[...].astype(BF16))
        tot = jnp.broadcast_to(cum[0:1, :] if reverse else cum[c - 1:c, :], (c, A_DK))
        delta = _dot_tn((k * jnp.exp2(tot - cum)).astype(BF16), v)
        if with_state:
            decay = jnp.sum(jnp.where(eye, jnp.exp2(tot[0:A_DK, :]), 0.0), axis=1, keepdims=True)
            s_ref[...] = decay * s_ref[...] + delta
        else:
            s_ref[...] = delta
        if not reverse:
            acc_ref[rows, :] = o
        else:
            y = acc_ref[rows, :] + o
            y = y * lax.rsqrt(jnp.mean(y * y, axis=-1, keepdims=True) + RMS_EPS) * g_ref[...]
            r = r_ref[rows, :]
            o_ref[rows, :] = (y * (r / (1.0 + jnp.exp(-r)))).astype(o_ref.dtype)

    for reverse in (False, True):
        first = (n_ch - 1) if reverse else 0
        chunk_step(first, reverse, has_s0)
        if n_ch > 1:
            def body(i, carry, reverse=reverse):
                chunk_step((n_ch - 1 - i) if reverse else i, reverse, True)
                return carry
            lax.fori_loop(1, n_ch, body, 0)

    if emit_state:
        st_ref[0] = sf_ref[...]
        st_ref[1] = sb_ref[...]


def _alias_prev(in_specs, args, prev):
    if prev is None:
        return {}
    in_specs.append(pl.BlockSpec(memory_space=pl.ANY))
    args.append(prev)
    return {len(args) - 1: 0}


def _gla(pa, pd, wg, bg, gla_g, levf, levb, row0, n_seq, t_len, s0, s0_layer, emit_state, prev=None):
    rb = row0 // t_len
    kb, vb, rbk = PA_K // A_DK, PA_V // A_DV, PA_R // A_DV
    in_specs = [
        pl.BlockSpec((t_len, A_DK), lambda n, h: (rb + n, h)),
        pl.BlockSpec((t_len, A_DK), lambda n, h: (rb + n, kb + h)),
        pl.BlockSpec((t_len, A_DV), lambda n, h: (rb + n, vb + h)),
        pl.BlockSpec((t_len, A_DV), lambda n, h: (rb + n, rbk + h)),
        pl.BlockSpec((t_len, 128), lambda n, h: (rb + n, PD_GG // 128)),
        pl.BlockSpec((None, 128, 2 * A_DK), lambda n, h: (h, 0, 0)),
        pl.BlockSpec((None, 1, 2 * A_DK), lambda n, h: (h, 0, 0)),
        pl.BlockSpec((1, A_DV), lambda n, h: (0, 0)),
        pl.BlockSpec((GLA_CHUNK, GLA_CHUNK), lambda n, h: (0, 0)),
        pl.BlockSpec((GLA_CHUNK, GLA_CHUNK), lambda n, h: (0, 0)),
    ]
    args = [pa, pa, pa, pa, pd, wg, bg, gla_g, levf, levb]
    if s0 is not None:
        in_specs.append(pl.BlockSpec((None, None, 2, None, A_DK, A_DV), lambda n, h: (n, s0_layer, 0, h, 0, 0)))
        args.append(s0)
    aliases = _alias_prev(in_specs, args, prev)
    out_shape = [jax.ShapeDtypeStruct((NTOK, A_HEADS * A_DV), BF16)]
    out_specs = [pl.BlockSpec((t_len, A_DV), lambda n, h: (rb + n, h))]
    if emit_state:
        out_shape.append(jax.ShapeDtypeStruct((n_seq, 2, A_HEADS, A_DK, A_DV), F32))
        out_specs.append(pl.BlockSpec((None, 2, None, A_DK, A_DV), lambda n, h: (n, 0, h, 0, 0)))
    res = pl.pallas_call(
        functools.partial(_gla_body, t_len=t_len, has_s0=s0 is not None, emit_state=emit_state,
                          has_prev=prev is not None),
        grid=(n_seq, A_HEADS),
        in_specs=in_specs,
        out_specs=out_specs,
        out_shape=out_shape,
        input_output_aliases=aliases,
        scratch_shapes=[pltpu.VMEM((t_len, A_DV), F32), pltpu.VMEM((A_DK, A_DV), F32),
                        pltpu.VMEM((A_DK, A_DV), F32), pltpu.VMEM((t_len, 2 * A_DK), F32)],
        compiler_params=_params(("arbitrary", "arbitrary"), 40),
        name="gla_ctx" if s0 is None else "gla_lat",
    )(*args)
    return res


def _rope(x, cos, sin):
    lane = lax.broadcasted_iota(jnp.int32, x.shape, 1)
    partner = jnp.where((lane & 32) == 0, pltpu.roll(x, 96, 1), pltpu.roll(x, 32, 1))
    return x * cos + partner * sin


def _rope_tables():
    n_rows = DEC_SEQ // GRID_W
    rows = jnp.repeat(jnp.arange(n_rows, dtype=F32), GRID_W)
    cols = jnp.tile(jnp.arange(GRID_W, dtype=F32), n_rows)
    quarter = C_ROPE // 4
    freqs = ROPE_THETA ** (-jnp.arange(quarter, dtype=F32) / quarter)
    ang = jnp.concatenate([rows[:, None] * freqs, cols[:, None] * freqs], axis=-1)
    cos, sin = jnp.cos(ang), jnp.sin(ang)
    cos_t = jnp.tile(cos, (1, 4))
    sin_t = jnp.tile(jnp.concatenate([-sin, sin], axis=-1), (1, 2))
    return cos_t, sin_t


def _softmax_parts(scores, scale):
    m = functools.reduce(jnp.maximum, [jnp.max(s, axis=-1, keepdims=True) for s in scores])
    ps = [jnp.exp2((s - m) * (scale * LOG2E)) for s in scores]
    inv = 1.0 / functools.reduce(jnp.add, [jnp.sum(p, axis=-1, keepdims=True) for p in ps])
    return ps, inv


def _pv(ps, values):
    return functools.reduce(jnp.add, [_dot(p.astype(BF16), v) for p, v in zip(ps, values)])


def _diff_body(*refs, latent, lam_init, has_prev, heads):
    it = iter(refs)
    lam_ref, q_ref, k_ref, v_ref, g_ref = [next(it) for _ in range(5)]
    if latent:
        kc_ref, vc_ref, cq_ref, sq_ref, ck_ref, sk_ref = [next(it) for _ in range(6)]
    if has_prev:
        next(it)
    o_ref = next(it)
    k_s, v_s = next(it), next(it)
    lam = lam_ref[0, 0]

    def head(ref, hh):
        return ref[:, hh * 128:(hh + 1) * 128]

    @pl.when(pl.program_id(2) == 0)
    def _():
        for hh in range(heads):
            k = head(k_ref, hh)
            if latent:
                k = _rope(k, ck_ref[...], sk_ref[...])
            k_s[:, hh * 128:(hh + 1) * 128] = k.astype(BF16)
        v_s[...] = v_ref[...].astype(BF16)

    lane = lax.broadcasted_iota(jnp.int32, (q_ref.shape[0], 128), 1)
    scale = B_DH ** -0.5
    scores = []
    for hh in range(heads):
        q = head(q_ref, hh)
        keys = [head(k_s, hh)]
        if latent:
            q = _rope(q, cq_ref[...], sq_ref[...])
            keys.insert(0, head(kc_ref, hh).astype(BF16))
        for half in range(2):
            qh = jnp.where((lane >> 6) == half, q, 0.0).astype(BF16)
            scores.append([_dot_nt(qh, kk) for kk in keys])
    parts = [_softmax_parts(s, scale) for s in scores]
    for hh in range(heads):
        values = [head(v_s, hh)]
        if latent:
            values.insert(0, head(vc_ref, hh).astype(BF16))
        (p1, inv1), (p2, inv2) = parts[2 * hh], parts[2 * hh + 1]
        o = _pv(p1, values) * inv1 - _pv(p2, values) * (lam * inv2)
        o = o * lax.rsqrt(jnp.mean(o * o, axis=-1, keepdims=True) + RMS_EPS) * g_ref[...] * (1.0 - lam_init)
        o_ref[:, hh * B_DV:(hh + 1) * B_DV] = o.astype(o_ref.dtype)


def _diff_attn(pd, lam, diff_g, lam_init, row0, n_seq, t_len, cache=None, rope=None, bq=256, prev=None, heads=2):
    latent = cache is not None
    rb = row0 // t_len
    qb0 = row0 // bq
    nqb = t_len // bq
    gw = heads * 128
    in_specs = [
        pl.BlockSpec(memory_space=pltpu.SMEM),
        pl.BlockSpec((bq, gw), lambda n, h, b: (qb0 + n * nqb + b, PD_Q // gw + h)),
        pl.BlockSpec((t_len, gw), lambda n, h, b: (rb + n, PD_K // gw + h)),
        pl.BlockSpec((t_len, gw), lambda n, h, b: (rb + n, PD_V // gw + h)),
        pl.BlockSpec((1, B_DV), lambda n, h, b: (0, 0)),
    ]
    args = [lam, pd, pd, pd, diff_g]
    if latent:
        kc, vc, layer = cache
        cos_t, sin_t = rope
        in_specs += [
            pl.BlockSpec((None, None, PAST_LEN, gw), lambda n, h, b: (n, layer, 0, h)),
            pl.BlockSpec((None, None, PAST_LEN, gw), lambda n, h, b: (n, layer, 0, h)),
            pl.BlockSpec((bq, 128), lambda n, h, b: (b, 0)),
            pl.BlockSpec((bq, 128), lambda n, h, b: (b, 0)),
            pl.BlockSpec((t_len, 128), lambda n, h, b: (0, 0)),
            pl.BlockSpec((t_len, 128), lambda n, h, b: (0, 0)),
        ]
        args += [kc, vc, cos_t, sin_t, cos_t, sin_t]
    aliases = _alias_prev(in_specs, args, prev)
    return pl.pallas_call(
        functools.partial(_diff_body, latent=latent, lam_init=lam_init, has_prev=prev is not None, heads=heads),
        grid=(n_seq, B_HEADS // heads, nqb),
        in_specs=in_specs,
        out_specs=pl.BlockSpec((bq, gw), lambda n, h, b: (qb0 + n * nqb + b, h)),
        out_shape=jax.ShapeDtypeStruct((NTOK, B_HEADS * B_DV), BF16),
        input_output_aliases=aliases,
        scratch_shapes=[pltpu.VMEM((t_len, gw), BF16), pltpu.VMEM((t_len, gw), BF16)],
        compiler_params=_params(("arbitrary", "arbitrary", "arbitrary"), 40),
        name="diff_lat" if latent else "diff_ctx",
    )(*args)


def _mla_body(*refs, latent, has_prev, heads):
    it = iter(refs)
    qn_ref, qr_ref, kv_ref, kr_ref = [next(it) for _ in range(4)]
    if latent:
        kvc_ref, krc_ref, cq_ref, sq_ref, ck_ref, sk_ref = [next(it) for _ in range(6)]
    if has_prev:
        next(it)
    o_ref = next(it)
    kr_s = next(it)

    @pl.when((pl.program_id(1) == 0) & (pl.program_id(2) == 0))
    def _():
        kr = kr_ref[...]
        if latent:
            kr = _rope(kr, ck_ref[...], sk_ref[...])
        kr_s[...] = kr.astype(BF16)

    scale = (C_NOPE + C_ROPE) ** -0.5
    lane = lax.broadcasted_iota(jnp.int32, (qr_ref.shape[0], 128), 1)
    kr_new = kr_s[...]
    kr_old = krc_ref[...].astype(BF16) if latent else None
    scores = []
    for hh in range(heads):
        if hh % 2 == 0:
            qr = qr_ref[:, (hh // 2) * 128:(hh // 2 + 1) * 128]
            if latent:
                qr = _rope(qr, cq_ref[...], sq_ref[...])
        qcat = jnp.concatenate([qn_ref[:, hh * C_NOPE:(hh + 1) * C_NOPE],
                                jnp.where((lane >> 6) == hh % 2, qr, 0.0).astype(BF16)], axis=1)
        ks = [jnp.concatenate([kv_ref[:, 2 * hh * C_NOPE:(2 * hh + 1) * C_NOPE], kr_new], axis=1)]
        if latent:
            ks.insert(0, jnp.concatenate([kvc_ref[:, 2 * hh * C_NOPE:(2 * hh + 1) * C_NOPE], kr_old], axis=1))
        scores.append([_dot_nt(qcat, kk) for kk in ks])
    parts = [_softmax_parts(s, scale) for s in scores]
    for hh in range(heads):
        vs = [kv_ref[:, (2 * hh + 1) * C_NOPE:(2 * hh + 2) * C_NOPE]]
        if latent:
            vs.insert(0, kvc_ref[:, (2 * hh + 1) * C_NOPE:(2 * hh + 2) * C_NOPE])
        ps, inv = parts[hh]
        o_ref[:, hh * C_DV:(hh + 1) * C_DV] = (_pv(ps, vs) * inv).astype(o_ref.dtype)


def _mla_attn(qn, qr, kv, kr2, row0, n_seq, t_len, cache=None, rope=None, bq=256, prev=None, heads=2):
    latent = cache is not None
    rb = row0 // t_len
    qb0 = row0 // bq
    nqb = t_len // bq
    group_kv = heads * (C_NOPE + C_DV)
    in_specs = [
        pl.BlockSpec((bq, heads * C_NOPE), lambda n, h, b: (qb0 + n * nqb + b, h)),
        pl.BlockSpec((bq, heads * C_ROPE), lambda n, h, b: (qb0 + n * nqb + b, h)),
        pl.BlockSpec((t_len, group_kv), lambda n, h, b: (rb + n, h)),
        pl.BlockSpec((t_len, 128), lambda n, h, b: (rb + n, 0)),
    ]
    args = [qn, qr, kv, kr2]
    if latent:
        kvc, krc = cache
        cos_t, sin_t = rope
        in_specs += [
            pl.BlockSpec((PAST_LEN, group_kv), lambda n, h, b: (n, h)),
            pl.BlockSpec((PAST_LEN, 128), lambda n, h, b: (n, 0)),
            pl.BlockSpec((bq, 128), lambda n, h, b: (b, 0)),
            pl.BlockSpec((bq, 128), lambda n, h, b: (b, 0)),
            pl.BlockSpec((t_len, 128), lambda n, h, b: (0, 0)),
            pl.BlockSpec((t_len, 128), lambda n, h, b: (0, 0)),
        ]
        args += [kvc, krc, cos_t, sin_t, cos_t, sin_t]
    aliases = _alias_prev(in_specs, args, prev)
    return pl.pallas_call(
        functools.partial(_mla_body, latent=latent, has_prev=prev is not None, heads=heads),
        grid=(n_seq, C_HEADS // heads, nqb),
        in_specs=in_specs,
        out_specs=pl.BlockSpec((bq, heads * C_DV), lambda n, h, b: (qb0 + n * nqb + b, h)),
        out_shape=jax.ShapeDtypeStruct((NTOK, C_HEADS * C_DV), BF16),
        input_output_aliases=aliases,
        scratch_shapes=[pltpu.VMEM((t_len, 128), BF16)],
        compiler_params=_params(("arbitrary", "arbitrary", "arbitrary"), 40),
        name="mla_lat" if latent else "mla_ctx",
    )(*args)


RT_E0, RT_E1, RT_R0, RT_R1, RT_W0, RT_W1 = range(6)
ROUTER_TM = 512


def _router_body(x_ref, sc_ref, sh_ref, w_ref, b_ref, tri_ref, rt_ref, cnt_ref, carry_ref):
    h = x_ref[...] * (1.0 + sc_ref[0, 0]) + sh_ref[0, 0]
    w = w_ref[...]
    h_hi = h.astype(BF16)
    w_hi = w.astype(BF16)
    h_lo = (h - h_hi.astype(F32)).astype(BF16)
    w_lo = (w - w_hi.astype(F32)).astype(BF16)
    lg = _dot(h_hi, w_hi) + (_dot(h_lo, w_hi) + _dot(h_hi, w_lo)) + b_ref[...]

    @pl.when(pl.program_id(0) == 0)
    def _():
        carry_ref[...] = jnp.zeros(carry_ref.shape, F32)

    lane = lax.broadcasted_iota(jnp.int32, lg.shape, 1)
    ninf = -jnp.inf
    is_g = lane < N_GROUPS
    gl = jnp.where(is_g, lg, ninf)
    gmax = jnp.max(gl, axis=1, keepdims=True)
    g_sel = jnp.min(jnp.where(gl == gmax, lane, 128), axis=1, keepdims=True)
    g_w = 1.0 / jnp.sum(jnp.where(is_g, jnp.exp(lg - gmax), 0.0), axis=1, keepdims=True)
    el = jnp.where(((lane - N_GROUPS) >> 2) == g_sel, lg, ninf)
    m1 = jnp.max(el, axis=1, keepdims=True)
    i1 = jnp.min(jnp.where(el == m1, lane, 128), axis=1, keepdims=True)
    el2 = jnp.where(lane == i1, ninf, el)
    m2 = jnp.max(el2, axis=1, keepdims=True)
    i2 = jnp.min(jnp.where(el2 == m2, lane, 128), axis=1, keepdims=True)
    p2 = jnp.exp(m2 - m1)
    w0 = g_w / (1.0 + p2)
    w1 = w0 * p2
    e0 = i1 - N_GROUPS
    e1 = i2 - N_GROUPS
    hit0 = lane == e0
    hit1 = lane == e1
    onehot = jnp.where(hit0, 1.0, jnp.where(hit1, 1.0, 0.0))
    before = _dot(tri_ref[...], onehot.astype(BF16)) + carry_ref[...]
    r0 = jnp.sum(jnp.where(hit0, before, 0.0), axis=1, keepdims=True)
    r1 = jnp.sum(jnp.where(hit1, before, 0.0), axis=1, keepdims=True)
    carry = carry_ref[...] + jnp.sum(onehot, axis=0, keepdims=True)
    carry_ref[...] = carry
    cnt_ref[...] = jnp.broadcast_to(carry, cnt_ref.shape)
    rec = [e0.astype(F32), e1.astype(F32), r0, r1, w0, w1]
    out = jnp.zeros(lg.shape, F32)
    for idx, val in enumerate(rec):
        out = jnp.where(lane == idx, val, out)
    rt_ref[...] = out


def _router(x, mod4, layer, w_r, b_r, tri):
    tm = ROUTER_TM
    return pl.pallas_call(
        _router_body,
        grid=(NTOK // tm,),
        in_specs=[pl.BlockSpec((tm, D_MODEL), lambda i: (i, 0)),
                  _mod_spec(layer, 4, tm), _mod_spec(layer, 3, tm),
                  pl.BlockSpec((D_MODEL, 128), lambda i: (0, 0)),
                  pl.BlockSpec((1, 128), lambda i: (0, 0)),
                  pl.BlockSpec((tm, tm), lambda i: (0, 0))],
        out_specs=[pl.BlockSpec((tm, 128), lambda i: (i, 0)),
                   pl.BlockSpec((8, 128), lambda i: (0, 0))],
        out_shape=[jax.ShapeDtypeStruct((NTOK, 128), F32), jax.ShapeDtypeStruct((8, 128), F32)],
        scratch_shapes=[pltpu.VMEM((1, 128), F32)],
        compiler_params=_params(("arbitrary",), 40),
        name="moe_router",
    )(x, mod4, mod4, w_r, b_r, tri)


DISPATCH_TM = 512


def _dispatch_body(dest_ref, pend_ref, padded_ref, x_ref, sc_ref, sh_ref, xs_hbm, hbuf, zbuf, sem, zsem):
    tm = DISPATCH_TM
    i = pl.program_id(0)

    @pl.when(i == 0)
    def _():
        zbuf[...] = jnp.zeros(zbuf.shape, F32)

        def tail(e):
            start = pl.multiple_of((pend_ref[e] - MOE_TM) // SUB, MOE_TM // SUB)
            return pltpu.make_async_copy(zbuf, xs_hbm.at[pl.ds(start, MOE_TM // SUB)], zsem)

        for e in range(N_EXPERTS):
            @pl.when(padded_ref[e] > 0)
            def _():
                tail(e).start()
        for e in range(N_EXPERTS):
            @pl.when(padded_ref[e] > 0)
            def _():
                tail(e).wait()

    h = x_ref[...] * (1.0 + sc_ref[0, 0]) + sh_ref[0, 0]
    hbuf[...] = h.reshape(tm // SUB, SUB, D_MODEL)

    def send(g, carry):
        for s in range(SUB):
            for k in range(EXPERT_TOP_K):
                d = dest_ref[k * NTOK + i * tm + g * SUB + s]
                pltpu.make_async_copy(hbuf.at[g, pl.ds(s, 1), :], _row(xs_hbm, d), sem).start()
        return carry
    lax.fori_loop(0, tm // SUB, send, 0)

    def drain(t, carry):
        for k in range(EXPERT_TOP_K):
            pltpu.make_async_copy(hbuf.at[0, pl.ds(0, 1), :], _row(xs_hbm, 0), sem).wait()
        return carry
    lax.fori_loop(0, tm, drain, 0, unroll=8)


def _dispatch(x, dest, pend, padded, mod4, layer):
    tm = DISPATCH_TM
    grid_spec = pltpu.PrefetchScalarGridSpec(
        num_scalar_prefetch=3,
        grid=(NTOK // tm,),
        in_specs=[pl.BlockSpec((tm, D_MODEL), lambda i, *_: (i, 0)),
                  _mod_spec(layer, 4, tm), _mod_spec(layer, 3, tm)],
        out_specs=pl.BlockSpec(memory_space=pl.ANY),
        scratch_shapes=[pltpu.VMEM((tm // SUB, SUB, D_MODEL), F32), pltpu.VMEM((MOE_TM // SUB, SUB, D_MODEL), F32),
                        pltpu.SemaphoreType.DMA(()), pltpu.SemaphoreType.DMA(())],
    )
    xs = pl.pallas_call(
        _dispatch_body,
        grid_spec=grid_spec,
        out_shape=jax.ShapeDtypeStruct((MOE_ROWS // SUB, SUB, D_MODEL), F32),
        compiler_params=_params(("arbitrary",), 40),
        name="moe_dispatch",
    )(dest, pend, padded, x, mod4, mod4)
    return xs.reshape(MOE_ROWS, D_MODEL)


def _expert_body(te_ref, tv_ref, x_ref, wg_ref, wu_ref, wd_ref, o_ref, wg_s, wu_s, wd_s):
    t = pl.program_id(0)
    prev = te_ref[jnp.maximum(t - 1, 0)]

    @pl.when((t == 0) | (te_ref[t] != prev))
    def _():
        wg_s[...] = wg_ref[...].astype(BF16)
        wu_s[...] = wu_ref[...].astype(BF16)
        wd_s[...] = wd_ref[...].astype(BF16)

    @pl.when(tv_ref[t] > 0)
    def _():
        x = x_ref[...].astype(BF16)
        g = _dot(x, wg_s[...])
        u = _dot(x, wu_s[...])
        act = (g / (1.0 + jnp.exp(-g))) * u
        o_ref[...] = _dot(act.astype(BF16), wd_s[...])

    @pl.when(tv_ref[t] == 0)
    def _():
        o_ref[...] = jnp.zeros(o_ref.shape, o_ref.dtype)


def _experts(xs, tile_e, tile_v, w_gate, w_up, w_down, layer):
    grid_spec = pltpu.PrefetchScalarGridSpec(
        num_scalar_prefetch=2,
        grid=(MOE_TILES,),
        in_specs=[
            pl.BlockSpec((MOE_TM, D_MODEL), lambda t, te, tv: (jnp.where(tv[t] > 0, t, 0), 0)),
            pl.BlockSpec((None, None, D_MODEL, D_EXPERT), lambda t, te, tv: (layer, te[t], 0, 0)),
            pl.BlockSpec((None, None, D_MODEL, D_EXPERT), lambda t, te, tv: (layer, te[t], 0, 0)),
            pl.BlockSpec((None, None, D_EXPERT, D_MODEL), lambda t, te, tv: (layer, te[t], 0, 0)),
        ],
        out_specs=pl.BlockSpec((MOE_TM, D_MODEL), lambda t, te, tv: (t, 0)),
        scratch_shapes=[pltpu.VMEM((D_MODEL, D_EXPERT), BF16), pltpu.VMEM((D_MODEL, D_EXPERT), BF16),
                        pltpu.VMEM((D_EXPERT, D_MODEL), BF16)],
    )
    return pl.pallas_call(
        _expert_body,
        grid_spec=grid_spec,
        out_shape=jax.ShapeDtypeStruct((MOE_ROWS, D_MODEL), F32),
        compiler_params=_params(("arbitrary",), 56),
        name="moe_experts",
    )(tile_e, tile_v, xs, w_gate, w_up, w_down)


def _layout(route, counts):
    counts = counts[0, :N_EXPERTS].astype(jnp.int32)
    padded = ((counts + MOE_TM - 1) // MOE_TM) * MOE_TM
    pend = jnp.cumsum(padded)
    pstart = pend - padded
    experts = jnp.arange(N_EXPERTS, dtype=jnp.int32)[None, :]

    def dest_of(e_lane, r_lane):
        e = route[:, e_lane].astype(jnp.int32)
        base = jnp.sum(jnp.where(e[:, None] == experts, pstart[None, :], 0), axis=1)
        return base + route[:, r_lane].astype(jnp.int32)

    dest = jnp.concatenate([dest_of(RT_E0, RT_R0), dest_of(RT_E1, RT_R1)])
    tile_start = jnp.arange(MOE_TILES, dtype=jnp.int32) * MOE_TM
    tile_e = jnp.minimum(jnp.sum((tile_start[:, None] >= pend[None, :]).astype(jnp.int32), axis=1), N_EXPERTS - 1)
    tile_v = (tile_start < pend[-1]).astype(jnp.int32)
    return dest, pend.astype(jnp.int32), padded, tile_e.astype(jnp.int32), tile_v


def _moe(x, mod4, layer, w_r, b_r, tri, w_gate, w_up, w_down, ln_g, ln_b):
    route, counts = _router(x, mod4, layer, w_r, b_r, tri)
    dest, pend, padded, tile_e, tile_v = _layout(route, counts)
    xs = _dispatch(x, dest, pend, padded, mod4, layer)
    rows = _experts(xs, tile_e, tile_v, w_gate, w_up, w_down, layer)
    return _combine_ln(x, rows, dest, route, mod4, layer, 5, ln_g, ln_b)


def kernel(x_prompt, x_sample, state_gla, cache_diff_k, cache_diff_v, cache_mla_ckv, cache_mla_krope, c, c_ctx, ada_w, ada_b, ln_g, ln_b, ab_w_in, gla_w_gate2, gla_b_gate2, gla_norm_g, diff_lambda, diff_norm_g, ab_w_out, mla_w_in, mla_q_norm_g, mla_w_uq, mla_kv_norm_g, mla_w_ukv, mla_w_out, moe_w_rg, moe_b_rg, moe_w_re, moe_b_re, moe_w_gate, moe_w_up, moe_w_down):
    x = (x_prompt.reshape(NCTX, D_MODEL), x_sample.reshape(NLAT, D_MODEL))
    cond = jnp.concatenate([c_ctx[None, :], c, jnp.zeros((N_COND - 1 - DEC_BATCH, D_MODEL), F32)], axis=0)
    mod4 = _ada_mod(cond, ada_w, ada_b).reshape(DEPTH, N_COND, 1, 6 * D_MODEL)
    cos_t, sin_t = _rope_tables()
    levf = jnp.asarray(_gla_level_table(False))
    levb = jnp.asarray(_gla_level_table(True))
    tri = jnp.asarray(np.tril(np.ones((ROUTER_TM, ROUTER_TM), np.float32), -1), BF16)
    cache_k = cache_diff_k.reshape(DEC_BATCH, -1, PAST_LEN, B_HEADS * 2 * B_DH)
    cache_v = cache_diff_v.reshape(DEC_BATCH, -1, PAST_LEN, B_HEADS * B_DV)

    new_gla, new_dk, new_dv, new_ckv, new_kr = [], [], [], [], []
    for layer in range(DEPTH):
        i = layer // 2
        if layer % 2 == 0:
            w_ab = jnp.concatenate([ab_w_in[i, :, :PA_N], ab_w_in[i, :, AB_D0:], ab_w_in[i, :, AB_GG0:AB_D0],
                                    jnp.zeros((D_MODEL, 128 - 2 * A_GATE_RANK), F32)], axis=1).astype(BF16)[None]
            pa = pd = _mod_matmul(x, mod4, layer, w_ab, 0, P_N, 896, "ab_in")
            wg = jnp.zeros((A_HEADS, 128, 2 * A_DK), F32)
            for j in range(2):
                blk = gla_w_gate2[i, j].reshape(A_GATE_RANK, A_HEADS, A_DK).transpose(1, 0, 2)
                wg = wg.at[:, j * A_GATE_RANK:(j + 1) * A_GATE_RANK, j * A_DK:(j + 1) * A_DK].set(blk)
            bg = gla_b_gate2[i].reshape(2, A_HEADS, A_DK).transpose(1, 0, 2).reshape(A_HEADS, 1, 2 * A_DK)
            g_gla = gla_norm_g[i][None, :]
            o_gla_c, st = _gla(pa, pd, wg, bg, g_gla, levf, levb, 0, BATCH, SEQ, None, 0, True)
            (o_gla,) = _gla(pa, pd, wg, bg, g_gla, levf, levb, NCTX, DEC_BATCH, DEC_SEQ, state_gla, i, False,
                            prev=o_gla_c)
            lv = diff_lambda[i]
            lam_init = 0.8 - 0.6 * math.exp(-0.3 * layer)
            lam = (jnp.exp(jnp.sum(lv[0] * lv[1])) - jnp.exp(jnp.sum(lv[2] * lv[3])) + lam_init).reshape(1, 1)
            g_diff = diff_norm_g[i][None, :]
            o_diff_c = _diff_attn(pd, lam, g_diff, lam_init, 0, BATCH, SEQ, heads=8)
            o_diff = _diff_attn(pd, lam, g_diff, lam_init, NCTX, DEC_BATCH, DEC_SEQ,
                                cache=(cache_k, cache_v, i), rope=(cos_t, sin_t), bq=DEC_SEQ, prev=o_diff_c)
            x = _proj_ln([o_gla, o_diff], ab_w_out[i].astype(BF16)[None], 0, x, mod4, layer, 2,
                         ln_g[layer, 0][None, :], ln_b[layer, 0][None, :], "ab_out_ln",
                         tm=512 if isinstance(x, tuple) else 1024)
            new_gla.append(st)
            new_dk.append(pd[:NCTX, PD_K:PD_V].reshape(BATCH, SEQ, B_HEADS, 2, B_DH))
            new_dv.append(pd[:NCTX, PD_V:PD_GG].reshape(BATCH, SEQ, B_HEADS, B_DV))
        else:
            c_in = C_Q_RANK + C_KV_RANK + C_ROPE
            pm = _mod_matmul(x, mod4, layer, mla_w_in[i].astype(BF16)[None], 0, c_in, c_in, "mla_in")
            w_uq = mla_w_uq[i].reshape(C_Q_RANK, C_HEADS, C_NOPE + C_ROPE)
            w_uq = jnp.concatenate([w_uq[:, :, :C_NOPE].reshape(C_Q_RANK, -1),
                                    w_uq[:, :, C_NOPE:].reshape(C_Q_RANK, -1)], axis=1).astype(BF16)
            w_ukv = mla_w_ukv[i].astype(BF16)
            g_q = mla_q_norm_g[i][None, :]
            g_kv = mla_kv_norm_g[i][None, :]
            qn = _rms_matmul(pm, 0, C_Q_RANK, g_q, w_uq, 0, C_HEADS * C_NOPE, 512, 1024, BF16, "mla_q_nope")
            qr = _rms_matmul(pm, 0, C_Q_RANK, g_q, w_uq, C_HEADS * C_NOPE // 512, C_HEADS * C_ROPE, 512, 1024, F32,
                             "mla_q_rope")
            kv, ckv_n = _rms_matmul(pm, C_Q_RANK // C_KV_RANK, C_KV_RANK, g_kv, w_ukv, 0,
                                    C_HEADS * (C_NOPE + C_DV), 1024, 1024, BF16, "mla_kv", emit_norm=True)
            kv_c = _rms_matmul(cache_mla_ckv[:, i].reshape(DEC_BATCH * PAST_LEN, C_KV_RANK), 0, C_KV_RANK, g_kv,
                               w_ukv, 0, C_HEADS * (C_NOPE + C_DV), 1024, 1024, BF16, "mla_kv_cache",
                               norm=False)
            krope = pm[:, C_Q_RANK + C_KV_RANK:]
            kr2 = jnp.concatenate([krope, krope], axis=1)
            krc = cache_mla_krope[:, i].reshape(DEC_BATCH * PAST_LEN, C_ROPE)
            krc2 = jnp.concatenate([krc, krc], axis=1)
            o_c = _mla_attn(qn, qr, kv, kr2, 0, BATCH, SEQ, heads=8)
            o_mla = _mla_attn(qn, qr, kv, kr2, NCTX, DEC_BATCH, DEC_SEQ, cache=(kv_c, krc2), rope=(cos_t, sin_t),
                              bq=DEC_SEQ, prev=o_c, heads=2)
            x = _proj_ln([o_mla], mla_w_out[i].astype(BF16)[None], 0, x, mod4, layer, 2,
                         ln_g[layer, 0][None, :], ln_b[layer, 0][None, :], "mla_out_ln")
            new_ckv.append(ckv_n[:NCTX].reshape(BATCH, SEQ, C_KV_RANK))
            new_kr.append(krope[:NCTX].reshape(BATCH, SEQ, C_ROPE))
        w_r = jnp.concatenate([moe_w_rg[layer], moe_w_re[layer],
                               jnp.zeros((D_MODEL, 128 - N_GROUPS - N_EXPERTS), F32)], axis=1)
        b_r = jnp.concatenate([moe_b_rg[layer], moe_b_re[layer],
                               jnp.zeros((128 - N_GROUPS - N_EXPERTS,), F32)])[None, :]
        x = _moe(x, mod4, layer, w_r, b_r, tri, moe_w_gate, moe_w_up, moe_w_down,
                 ln_g[layer, 1][None, :], ln_b[layer, 1][None, :])

    return (x[:NCTX].reshape(BATCH, SEQ, D_MODEL),
            x[NCTX:].reshape(DEC_BATCH, DEC_SEQ, D_MODEL),
            jnp.stack(new_gla, axis=1), jnp.stack(new_dk, axis=1), jnp.stack(new_dv, axis=1),
            jnp.stack(new_ckv, axis=1), jnp.stack(new_kr, axis=1))
```

```python
import functools
import math

import numpy as np
import jax
import jax.numpy as jnp
from jax import lax
from jax.experimental import pallas as pl
from jax.experimental.pallas import tpu as pltpu

F32 = jnp.float32
BF16 = jnp.bfloat16

D_MODEL = 2048
BATCH = 16
SEQ = 256
DEPTH = 4
DEC_BATCH = 4
DEC_SEQ = 1024
PAST_LEN = 256
GRID_W = 64
A_HEADS = 4
A_DK = 128
A_DV = 256
A_GATE_RANK = 16
A_TAU = 16.0
B_HEADS = 8
B_DH = 64
B_DV = 2 * B_DH
C_HEADS = 16
C_Q_RANK = 512
C_KV_RANK = 256
C_NOPE = 128
C_ROPE = 64
C_DV = 128
N_GROUPS = 4
EXPERTS_PER_GROUP = 4
N_EXPERTS = N_GROUPS * EXPERTS_PER_GROUP
EXPERT_TOP_K = 2
D_EXPERT = 512
ROPE_THETA = 10000.0
LN_EPS = 1e-5
RMS_EPS = 1e-6
DEEPNORM_ALPHA = (2.0 * DEPTH) ** 0.25

NCTX = BATCH * SEQ
NLAT = DEC_BATCH * DEC_SEQ
NTOK = NCTX + NLAT
N_COND = 8
GLA_CHUNK = 256
GLA_LEVELS = 8
MOE_TM = 512
LOG2E = 1.4426950408889634
MOE_ROWS = NTOK * EXPERT_TOP_K + N_EXPERTS * MOE_TM
MOE_TILES = MOE_ROWS // MOE_TM

PA_Q, PA_K, PA_V, PA_R = 0, A_HEADS * A_DK, 2 * A_HEADS * A_DK, 2 * A_HEADS * A_DK + A_HEADS * A_DV
PA_N = PA_R + A_HEADS * A_DV
AB_GG0 = PA_N
AB_D0 = PA_N + 2 * A_GATE_RANK
PD_Q = PA_N
PD_K = PD_Q + B_HEADS * 2 * B_DH
PD_V = PD_K + B_HEADS * 2 * B_DH
PD_GG = PD_V + B_HEADS * B_DV
P_TN = 1280
P_N = -(-(PD_GG + 2 * A_GATE_RANK) // P_TN) * P_TN


def _params(sem, vmem_mb):
    return pltpu.CompilerParams(dimension_semantics=sem, vmem_limit_bytes=vmem_mb * 2 ** 20)


def _cond_row(i, tm):
    return jnp.where(i * tm < NCTX, 0, 1 + (i * tm - NCTX) // DEC_SEQ)


def _dot(a, b):
    return jnp.dot(a, b, preferred_element_type=F32)


def _dot_nt(a, b):
    return lax.dot_general(a, b, (((1,), (1,)), ((), ())), preferred_element_type=F32)


def _dot_tn(a, b):
    return lax.dot_general(a, b, (((0,), (0,)), ((), ())), preferred_element_type=F32)


def _ada_body(c_ref, w_ref, b_ref, o_ref):
    c = c_ref[...]
    a = (c / (1.0 + jnp.exp(-c))).astype(BF16)
    o_ref[0] = _dot(a, w_ref[0].astype(BF16)) + b_ref[0]


def _ada_mod(cond, ada_w, ada_b):
    tn = 1024
    return pl.pallas_call(
        _ada_body,
        grid=(DEPTH, 6 * D_MODEL // tn),
        in_specs=[pl.BlockSpec((N_COND, D_MODEL), lambda l, j: (0, 0)),
                  pl.BlockSpec((1, D_MODEL, tn), lambda l, j: (l, 0, j)),
                  pl.BlockSpec((1, 1, tn), lambda l, j: (l, 0, j))],
        out_specs=pl.BlockSpec((1, N_COND, tn), lambda l, j: (l, 0, j)),
        out_shape=jax.ShapeDtypeStruct((DEPTH, N_COND, 6 * D_MODEL), F32),
        compiler_params=_params(("arbitrary", "arbitrary"), 40),
        name="ada_mod",
    )(cond, ada_w, ada_b.reshape(DEPTH, 1, 6 * D_MODEL))


def _mod_spec(layer, chunk, tm):
    return pl.BlockSpec((1, 1, 1, D_MODEL), lambda i, *_: (layer, _cond_row(i, tm), 0, chunk))


def _stream_specs(x, tm, single_buffer=False):
    if not isinstance(x, tuple):
        return [pl.BlockSpec((tm, D_MODEL), lambda i, *_: (i, 0))], [x]
    n_c = NCTX // tm
    mode = dict(pipeline_mode=pl.Buffered(1)) if single_buffer else {}
    return ([pl.BlockSpec((tm, D_MODEL), lambda i, *_: (jnp.minimum(i, n_c - 1), 0), **mode),
             pl.BlockSpec((tm, D_MODEL), lambda i, *_: (jnp.maximum(i - n_c, 0), 0), **mode)], list(x))


def _for_stream(x_refs, tm, fn):
    if len(x_refs) == 1:
        fn(x_refs[0])
        return
    n_c = NCTX // tm
    pl.when(pl.program_id(0) < n_c)(lambda: fn(x_refs[0]))
    pl.when(pl.program_id(0) >= n_c)(lambda: fn(x_refs[1]))


def _mod_mm_body(*refs, n_x, tm):
    x_refs = refs[:n_x]
    sc_ref, sh_ref, w_ref, o_ref, h_ref = refs[n_x:]

    @pl.when(pl.program_id(1) == 0)
    def _():
        def modulate(x_ref):
            h_ref[...] = (x_ref[...] * (1.0 + sc_ref[0, 0]) + sh_ref[0, 0]).astype(BF16)
        _for_stream(x_refs, tm, modulate)

    o_ref[...] = _dot(h_ref[...], w_ref[...]).astype(o_ref.dtype)


def _mod_matmul(x, mod4, layer, w, w_layer, n_out, tn, name, tm=1024, out_dtype=F32):
    x_specs, x_args = _stream_specs(x, tm, single_buffer=True)
    return pl.pallas_call(
        functools.partial(_mod_mm_body, n_x=len(x_args), tm=tm),
        grid=(NTOK // tm, n_out // tn),
        in_specs=x_specs + [_mod_spec(layer, 1, tm), _mod_spec(layer, 0, tm),
                            pl.BlockSpec((None, D_MODEL, tn), lambda i, j: (w_layer, 0, j))],
        out_specs=pl.BlockSpec((tm, tn), lambda i, j: (i, j)),
        out_shape=jax.ShapeDtypeStruct((NTOK, n_out), out_dtype),
        scratch_shapes=[pltpu.VMEM((tm, D_MODEL), BF16)],
        compiler_params=_params(("arbitrary", "arbitrary"), 56),
        name=name,
    )(*x_args, mod4, mod4, w)


def _rms_mm_body(x_ref, g_ref, w_ref, *rest, norm, emit_norm):
    if emit_norm:
        o_ref, n_ref, h_ref = rest
    else:
        o_ref, h_ref = rest

    @pl.when(pl.program_id(1) == 0)
    def _():
        x = x_ref[...]
        if norm:
            x = x * lax.rsqrt(jnp.mean(x * x, axis=-1, keepdims=True) + RMS_EPS) * g_ref[...]
        h_ref[...] = x.astype(BF16)
        if emit_norm:
            n_ref[...] = x

    o_ref[...] = _dot(h_ref[...], w_ref[...]).astype(o_ref.dtype)


def _rms_matmul(x, col_block, k, g, w, col0_blocks, n_out, tn, tm, out_dtype, name, norm=True, emit_norm=False):
    rows = x.shape[0]
    out_shape = [jax.ShapeDtypeStruct((rows, n_out), out_dtype)]
    out_specs = [pl.BlockSpec((tm, tn), lambda i, j: (i, j))]
    if emit_norm:
        out_shape.append(jax.ShapeDtypeStruct((rows, k), F32))
        out_specs.append(pl.BlockSpec((tm, k), lambda i, j: (i, 0)))
    res = pl.pallas_call(
        functools.partial(_rms_mm_body, norm=norm, emit_norm=emit_norm),
        grid=(rows // tm, n_out // tn),
        in_specs=[pl.BlockSpec((tm, k), lambda i, j: (i, col_block)),
                  pl.BlockSpec((1, k), lambda i, j: (0, 0)),
                  pl.BlockSpec((k, tn), lambda i, j: (0, col0_blocks + j))],
        out_specs=out_specs,
        out_shape=out_shape,
        scratch_shapes=[pltpu.VMEM((tm, k), BF16)],
        compiler_params=_params(("arbitrary", "arbitrary"), 40),
        name=name,
    )(x, g, w)
    return res if emit_norm else res[0]


def _layer_norm(z, g, b):
    mu = jnp.mean(z, axis=-1, keepdims=True)
    zc = z - mu
    var = jnp.mean(zc * zc, axis=-1, keepdims=True)
    return zc * lax.rsqrt(var + LN_EPS) * g + b


LN_ROWS = 256


def _proj_ln_body(*refs, n_lhs, n_x, n_j, tm, tn):
    lhs = refs[:n_lhs]
    ws = refs[n_lhs:2 * n_lhs]
    x_refs = refs[2 * n_lhs:2 * n_lhs + n_x]
    gate_ref, g_ref, b_ref, o_ref = refs[2 * n_lhs + n_x:]
    j = pl.program_id(1)
    y = _dot(lhs[0][...], ws[0][...])
    for a, w in zip(lhs[1:], ws[1:]):
        y = y + _dot(a[...], w[...])
    o_ref[:, pl.ds(pl.multiple_of(j * tn, tn), tn)] = y

    @pl.when(j == n_j - 1)
    def _():
        def normalise(x_ref):
            def body(r, carry):
                rows = pl.ds(pl.multiple_of(r * LN_ROWS, LN_ROWS), LN_ROWS)
                z = DEEPNORM_ALPHA * x_ref[rows, :] + gate_ref[0, 0] * o_ref[rows, :]
                o_ref[rows, :] = _layer_norm(z, g_ref[...], b_ref[...])
                return carry
            lax.fori_loop(0, tm // LN_ROWS, body, 0)
        _for_stream(x_refs, tm, normalise)


def _proj_ln(lhs_list, w, w_layer, x, mod4, layer, gate_chunk, ln_g, ln_b, name, tm=1024, tn=1024):
    n_lhs = len(lhs_list)
    kk = lhs_list[0].shape[1]
    n_j = D_MODEL // tn
    in_specs = [pl.BlockSpec((tm, kk), lambda i, j: (i, 0)) for _ in lhs_list]
    in_specs += [pl.BlockSpec((None, kk, tn), functools.partial(lambda i, j, r: (w_layer, r, j), r=r))
                 for r in range(n_lhs)]
    x_specs, x_args = _stream_specs(x, tm, single_buffer=True)
    in_specs += x_specs + [_mod_spec(layer, gate_chunk, tm),
                           pl.BlockSpec((1, D_MODEL), lambda i, j: (0, 0)),
                           pl.BlockSpec((1, D_MODEL), lambda i, j: (0, 0))]
    return pl.pallas_call(
        functools.partial(_proj_ln_body, n_lhs=n_lhs, n_x=len(x_args), n_j=n_j, tm=tm, tn=tn),
        grid=(NTOK // tm, n_j),
        in_specs=in_specs,
        out_specs=pl.BlockSpec((tm, D_MODEL), lambda i, j: (i, 0)),
        out_shape=jax.ShapeDtypeStruct((NTOK, D_MODEL), F32),
        compiler_params=_params(("arbitrary", "arbitrary"), 56),
        name=name,
    )(*lhs_list, *([w] * n_lhs), *x_args, mod4, ln_g, ln_b)


COMBINE_TM = 256


SUB = 8


def _row(ref, i):
    return ref.at[i >> 3, pl.ds(i & (SUB - 1), 1), :]


def _combine_ln_body(dest_ref, x_ref, rt_ref, gate_ref, g_ref, b_ref, rows_hbm, o_ref, buf, sem):
    tm = COMBINE_TM
    i = pl.program_id(0)
    n_i = pl.num_programs(0)

    def fetch(step, slot):
        def body(g, carry):
            for s in range(SUB):
                for k in range(EXPERT_TOP_K):
                    d = dest_ref[k * NTOK + step * tm + g * SUB + s]
                    pltpu.make_async_copy(_row(rows_hbm, d), buf.at[slot, k, g, pl.ds(s, 1), :],
                                          sem.at[slot]).start()
            return carry
        lax.fori_loop(0, tm // SUB, body, 0)

    @pl.when(i == 0)
    def _():
        fetch(0, 0)

    @pl.when(i + 1 < n_i)
    def _():
        fetch(i + 1, (i + 1) % 2)

    slot = i % 2

    def drain(t, carry):
        for k in range(EXPERT_TOP_K):
            pltpu.make_async_copy(_row(rows_hbm, 0), buf.at[slot, k, 0, pl.ds(0, 1), :], sem.at[slot]).wait()
        return carry
    lax.fori_loop(0, tm, drain, 0, unroll=8)

    rt = rt_ref[...]
    y0 = buf[slot, 0].reshape(tm, D_MODEL)
    y1 = buf[slot, 1].reshape(tm, D_MODEL)
    y = rt[:, RT_W0:RT_W0 + 1] * y0 + rt[:, RT_W1:RT_W1 + 1] * y1
    z = DEEPNORM_ALPHA * x_ref[...] + gate_ref[0, 0] * y
    o_ref[...] = _layer_norm(z, g_ref[...], b_ref[...])


def _combine_ln(x, rows, dest, route, mod4, layer, gate_chunk, ln_g, ln_b):
    tm = COMBINE_TM
    row = pl.BlockSpec((tm, D_MODEL), lambda i, d: (i, 0))
    vec = pl.BlockSpec((1, D_MODEL), lambda i, d: (0, 0))
    grid_spec = pltpu.PrefetchScalarGridSpec(
        num_scalar_prefetch=1,
        grid=(NTOK // tm,),
        in_specs=[row, pl.BlockSpec((tm, 128), lambda i, d: (i, 0)),
                  _mod_spec(layer, gate_chunk, tm), vec, vec,
                  pl.BlockSpec(memory_space=pl.ANY)],
        out_specs=row,
        scratch_shapes=[pltpu.VMEM((2, EXPERT_TOP_K, tm // SUB, SUB, D_MODEL), F32),
                        pltpu.SemaphoreType.DMA((2,))],
    )
    return pl.pallas_call(
        _combine_ln_body,
        grid_spec=grid_spec,
        out_shape=jax.ShapeDtypeStruct((NTOK, D_MODEL), F32),
        compiler_params=_params(("arbitrary",), 40),
        name="moe_combine_ln",
    )(dest, x, route, mod4, ln_g, ln_b, rows.reshape(MOE_ROWS // SUB, SUB, D_MODEL))


def _gla_level_table(reverse):
    t = np.arange(GLA_CHUNK)[:, None]
    s = np.arange(GLA_CHUNK)[None, :]
    x = t ^ s
    lev = np.where(x > 0, np.floor(np.log2(np.maximum(x, 1))).astype(np.int32), -1)
    live = (t < s) if reverse else (t > s)
    return np.where(live | (t == s), lev, -2).astype(np.int32)


def _anchor_rows(u, level, reverse):
    c = u.shape[0]
    n = c // 8
    u3 = u.reshape(n, 8, 128)
    half = 1 << level
    if level < 3:
        blk = 2 * half
        sub = lax.broadcasted_iota(jnp.int32, (n, 8, 128), 1)
        out = None
        for j in range(8 // blk):
            r = j * blk + (half if reverse else half - 1)
            piece = jnp.broadcast_to(u3[:, r:r + 1, :], (n, 8, 128))
            out = piece if out is None else jnp.where(sub >= j * blk, piece, out)
        return out.reshape(c, 128)
    m = 1 << (level - 2)
    r = 0 if reverse else 7
    e = jnp.broadcast_to(u3[:, r:r + 1, :], (n, 8, 128)).reshape(n // m, m, 8, 128)
    idx = m // 2 if reverse else m // 2 - 1
    return jnp.broadcast_to(e[:, idx:idx + 1], (n // m, m, 8, 128)).reshape(c, 128)


def _gla_intra(q, k, la, lev, reverse):
    rowk = lax.broadcasted_iota(jnp.int32, (GLA_CHUNK, 128), 0)
    scores = jnp.where(lev == -1, _dot_nt(q.astype(BF16), k.astype(BF16)), 0.0)
    u = la
    for level in range(GLA_LEVELS):
        anchor = _anchor_rows(u, level, reverse)
        qa = (q * jnp.exp2(u)).astype(BF16)
        ka = (k * jnp.exp2(jnp.minimum(anchor - u, 0.0))).astype(BF16)
        scores = jnp.where(lev == level, _dot_nt(qa, ka), scores)
        query_side = ((rowk >> level) & 1) == (0 if reverse else 1)
        u = u + jnp.where(query_side, anchor, 0.0)
    return scores, u


def _gla_body(*refs, t_len, has_s0, emit_state, has_prev):
    it = iter(refs)
    q_ref, k_ref, v_ref, r_ref, gg_ref, wg_ref, bg_ref, g_ref, levf_ref, levb_ref = [next(it) for _ in range(10)]
    s0_ref = next(it) if has_s0 else None
    if has_prev:
        next(it)
    o_ref = next(it)
    st_ref = next(it) if emit_state else None
    acc_ref, sf_ref, sb_ref, la_ref = [next(it) for _ in range(4)]
    c = GLA_CHUNK
    n_ch = t_len // c

    gl = _dot(gg_ref[...].astype(BF16), wg_ref[...].astype(BF16)) + bg_ref[...]
    la_ref[...] = (jnp.minimum(gl, 0.0) - jnp.log1p(jnp.exp(-jnp.abs(gl)))) * (LOG2E / A_TAU)
    if has_s0:
        sf_ref[...] = s0_ref[0]
        sb_ref[...] = s0_ref[1]

    eye = (lax.broadcasted_iota(jnp.int32, (A_DK, A_DK), 0) == lax.broadcasted_iota(jnp.int32, (A_DK, A_DK), 1))

    def chunk_step(ci, reverse, with_state):
        rows = pl.ds(ci * c if isinstance(ci, int) else pl.multiple_of(ci * c, c), c)
        q = q_ref[rows, :] * (A_DK ** -0.5)
        k = k_ref[rows, :]
        v = v_ref[rows, :].astype(BF16)
        la = la_ref[rows, A_DK:2 * A_DK] if reverse else la_ref[rows, 0:A_DK]
        s_ref = sb_ref if reverse else sf_ref
        scores, cum = _gla_intra(q, k, la, (levb_ref if reverse else levf_ref)[...], reverse)
        o = _dot(scores.astype(BF16), v)
        if with_state:
            o = o + _dot((q * jnp.exp2(cum)).astype(BF16), s_ref[...].astype(BF16))
        tot = jnp.broadcast_to(cum[0:1, :] if reverse else cum[c - 1:c, :], (c, A_DK))
        delta = _dot_tn((k * jnp.exp2(tot - cum)).astype(BF16), v)
        if with_state:
            decay = jnp.sum(jnp.where(eye, jnp.exp2(tot[0:A_DK, :]), 0.0), axis=1, keepdims=True)
            s_ref[...] = decay * s_ref[...] + delta
        else:
            s_ref[...] = delta
        if not reverse:
            acc_ref[rows, :] = o
        else:
            y = acc_ref[rows, :] + o
            y = y * lax.rsqrt(jnp.mean(y * y, axis=-1, keepdims=True) + RMS_EPS) * g_ref[...]
            r = r_ref[rows, :]
            o_ref[rows, :] = (y * (r / (1.0 + jnp.exp(-r)))).astype(o_ref.dtype)

    for reverse in (False, True):
        first = (n_ch - 1) if reverse else 0
        chunk_step(first, reverse, has_s0)
        if n_ch > 1:
            def body(i, carry, reverse=reverse):
                chunk_step((n_ch - 1 - i) if reverse else i, reverse, True)
                return carry
            lax.fori_loop(1, n_ch, body, 0)

    if emit_state:
        st_ref[0] = sf_ref[...]
        st_ref[1] = sb_ref[...]


def _alias_prev(in_specs, args, prev):
    if prev is None:
        return {}
    in_specs.append(pl.BlockSpec(memory_space=pl.ANY))
    args.append(prev)
    return {len(args) - 1: 0}


def _gla(pa, pd, wg, bg, gla_g, levf, levb, row0, n_seq, t_len, s0, s0_layer, emit_state, prev=None):
    rb = row0 // t_len
    kb, vb, rbk = PA_K // A_DK, PA_V // A_DV, PA_R // A_DV
    in_specs = [
        pl.BlockSpec((t_len, A_DK), lambda n, h: (rb + n, h)),
        pl.BlockSpec((t_len, A_DK), lambda n, h: (rb + n, kb + h)),
        pl.BlockSpec((t_len, A_DV), lambda n, h: (rb + n, vb + h)),
        pl.BlockSpec((t_len, A_DV), lambda n, h: (rb + n, rbk + h)),
        pl.BlockSpec((t_len, 128), lambda n, h: (rb + n, PD_GG // 128)),
        pl.BlockSpec((None, 128, 2 * A_DK), lambda n, h: (h, 0, 0)),
        pl.BlockSpec((None, 1, 2 * A_DK), lambda n, h: (h, 0, 0)),
        pl.BlockSpec((1, A_DV), lambda n, h: (0, 0)),
        pl.BlockSpec((GLA_CHUNK, GLA_CHUNK), lambda n, h: (0, 0)),
        pl.BlockSpec((GLA_CHUNK, GLA_CHUNK), lambda n, h: (0, 0)),
    ]
    args = [pa, pa, pa, pa, pd, wg, bg, gla_g, levf, levb]
    if s0 is not None:
        in_specs.append(pl.BlockSpec((None, None, 2, None, A_DK, A_DV), lambda n, h: (n, s0_layer, 0, h, 0, 0)))
        args.append(s0)
    aliases = _alias_prev(in_specs, args, prev)
    out_shape = [jax.ShapeDtypeStruct((NTOK, A_HEADS * A_DV), BF16)]
    out_specs = [pl.BlockSpec((t_len, A_DV), lambda n, h: (rb + n, h))]
    if emit_state:
        out_shape.append(jax.ShapeDtypeStruct((n_seq, 2, A_HEADS, A_DK, A_DV), F32))
        out_specs.append(pl.BlockSpec((None, 2, None, A_DK, A_DV), lambda n, h: (n, 0, h, 0, 0)))
    res = pl.pallas_call(
        functools.partial(_gla_body, t_len=t_len, has_s0=s0 is not None, emit_state=emit_state,
                          has_prev=prev is not None),
        grid=(n_seq, A_HEADS),
        in_specs=in_specs,
        out_specs=out_specs,
        out_shape=out_shape,
        input_output_aliases=aliases,
        scratch_shapes=[pltpu.VMEM((t_len, A_DV), F32), pltpu.VMEM((A_DK, A_DV), F32),
                        pltpu.VMEM((A_DK, A_DV), F32), pltpu.VMEM((t_len, 2 * A_DK), F32)],
        compiler_params=_params(("arbitrary", "arbitrary"), 40),
        name="gla_ctx" if s0 is None else "gla_lat",
    )(*args)
    return res


def _rope(x, cos, sin):
    lane = lax.broadcasted_iota(jnp.int32, x.shape, 1)
    partner = jnp.where((lane & 32) == 0, pltpu.roll(x, 96, 1), pltpu.roll(x, 32, 1))
    return x * cos + partner * sin


def _rope_tables():
    n_rows = DEC_SEQ // GRID_W
    rows = jnp.repeat(jnp.arange(n_rows, dtype=F32), GRID_W)
    cols = jnp.tile(jnp.arange(GRID_W, dtype=F32), n_rows)
    quarter = C_ROPE // 4
    freqs = ROPE_THETA ** (-jnp.arange(quarter, dtype=F32) / quarter)
    ang = jnp.concatenate([rows[:, None] * freqs, cols[:, None] * freqs], axis=-1)
    cos, sin = jnp.cos(ang), jnp.sin(ang)
    cos_t = jnp.tile(cos, (1, 4))
    sin_t = jnp.tile(jnp.concatenate([-sin, sin], axis=-1), (1, 2))
    return cos_t, sin_t


def _softmax_parts(scores, scale):
    m = functools.reduce(jnp.maximum, [jnp.max(s, axis=-1, keepdims=True) for s in scores])
    ps = [jnp.exp2((s - m) * (scale * LOG2E)) for s in scores]
    inv = 1.0 / functools.reduce(jnp.add, [jnp.sum(p, axis=-1, keepdims=True) for p in ps])
    return ps, inv


def _pv(ps, values):
    return functools.reduce(jnp.add, [_dot(p.astype(BF16), v) for p, v in zip(ps, values)])


def _diff_body(*refs, latent, lam_init, has_prev, heads):
    it = iter(refs)
    lam_ref, q_ref, k_ref, v_ref, g_ref = [next(it) for _ in range(5)]
    if latent:
        kc_ref, vc_ref, cq_ref, sq_ref, ck_ref, sk_ref = [next(it) for _ in range(6)]
    if has_prev:
        next(it)
    o_ref = next(it)
    k_s, v_s = next(it), next(it)
    lam = lam_ref[0, 0]

    def head(ref, hh):
        return ref[:, hh * 128:(hh + 1) * 128]

    @pl.when(pl.program_id(2) == 0)
    def _():
        for hh in range(heads):
            k = head(k_ref, hh)
            if latent:
                k = _rope(k, ck_ref[...], sk_ref[...])
            k_s[:, hh * 128:(hh + 1) * 128] = k.astype(BF16)
        v_s[...] = v_ref[...].astype(BF16)

    lane = lax.broadcasted_iota(jnp.int32, (q_ref.shape[0], 128), 1)
    scale = B_DH ** -0.5
    scores = []
    for hh in range(heads):
        q = head(q_ref, hh)
        keys = [head(k_s, hh)]
        if latent:
            q = _rope(q, cq_ref[...], sq_ref[...])
            keys.insert(0, head(kc_ref, hh).astype(BF16))
        for half in range(2):
            qh = jnp.where((lane >> 6) == half, q, 0.0).astype(BF16)
            scores.append([_dot_nt(qh, kk) for kk in keys])
    parts = [_softmax_parts(s, scale) for s in scores]
    for hh in range(heads):
        values = [head(v_s, hh)]
        if latent:
            values.insert(0, head(vc_ref, hh).astype(BF16))
        (p1, inv1), (p2, inv2) = parts[2 * hh], parts[2 * hh + 1]
        o = _pv(p1, values) * inv1 - _pv(p2, values) * (lam * inv2)
        o = o * lax.rsqrt(jnp.mean(o * o, axis=-1, keepdims=True) + RMS_EPS) * g_ref[...] * (1.0 - lam_init)
        o_ref[:, hh * B_DV:(hh + 1) * B_DV] = o.astype(o_ref.dtype)


def _diff_attn(pd, lam, diff_g, lam_init, row0, n_seq, t_len, cache=None, rope=None, bq=256, prev=None, heads=2):
    latent = cache is not None
    rb = row0 // t_len
    qb0 = row0 // bq
    nqb = t_len // bq
    gw = heads * 128
    in_specs = [
        pl.BlockSpec(memory_space=pltpu.SMEM),
        pl.BlockSpec((bq, gw), lambda n, h, b: (qb0 + n * nqb + b, PD_Q // gw + h)),
        pl.BlockSpec((t_len, gw), lambda n, h, b: (rb + n, PD_K // gw + h)),
        pl.BlockSpec((t_len, gw), lambda n, h, b: (rb + n, PD_V // gw + h)),
        pl.BlockSpec((1, B_DV), lambda n, h, b: (0, 0)),
    ]
    args = [lam, pd, pd, pd, diff_g]
    if latent:
        kc, vc, layer = cache
        cos_t, sin_t = rope
        in_specs += [
            pl.BlockSpec((None, None, PAST_LEN, gw), lambda n, h, b: (n, layer, 0, h)),
            pl.BlockSpec((None, None, PAST_LEN, gw), lambda n, h, b: (n, layer, 0, h)),
            pl.BlockSpec((bq, 128), lambda n, h, b: (b, 0)),
            pl.BlockSpec((bq, 128), lambda n, h, b: (b, 0)),
            pl.BlockSpec((t_len, 128), lambda n, h, b: (0, 0)),
            pl.BlockSpec((t_len, 128), lambda n, h, b: (0, 0)),
        ]
        args += [kc, vc, cos_t, sin_t, cos_t, sin_t]
    aliases = _alias_prev(in_specs, args, prev)
    return pl.pallas_call(
        functools.partial(_diff_body, latent=latent, lam_init=lam_init, has_prev=prev is not None, heads=heads),
        grid=(n_seq, B_HEADS // heads, nqb),
        in_specs=in_specs,
        out_specs=pl.BlockSpec((bq, gw), lambda n, h, b: (qb0 + n * nqb + b, h)),
        out_shape=jax.ShapeDtypeStruct((NTOK, B_HEADS * B_DV), BF16),
        input_output_aliases=aliases,
        scratch_shapes=[pltpu.VMEM((t_len, gw), BF16), pltpu.VMEM((t_len, gw), BF16)],
        compiler_params=_params(("arbitrary", "arbitrary", "arbitrary"), 40),
        name="diff_lat" if latent else "diff_ctx",
    )(*args)


def _mla_body(*refs, latent, has_prev, heads):
    it = iter(refs)
    qn_ref, qr_ref, kv_ref, kr_ref = [next(it) for _ in range(4)]
    if latent:
        kvc_ref, krc_ref, cq_ref, sq_ref, ck_ref, sk_ref = [next(it) for _ in range(6)]
    if has_prev:
        next(it)
    o_ref = next(it)
    kr_s = next(it)

    @pl.when((pl.program_id(1) == 0) & (pl.program_id(2) == 0))
    def _():
        kr = kr_ref[...]
        if latent:
            kr = _rope(kr, ck_ref[...], sk_ref[...])
        kr_s[...] = kr.astype(BF16)

    scale = (C_NOPE + C_ROPE) ** -0.5
    lane = lax.broadcasted_iota(jnp.int32, (qr_ref.shape[0], 128), 1)
    kr_new = kr_s[...]
    kr_old = krc_ref[...].astype(BF16) if latent else None
    scores = []
    for hh in range(heads):
        if hh % 2 == 0:
            qr = qr_ref[:, (hh // 2) * 128:(hh // 2 + 1) * 128]
            if latent:
                qr = _rope(qr, cq_ref[...], sq_ref[...])
        qcat = jnp.concatenate([qn_ref[:, hh * C_NOPE:(hh + 1) * C_NOPE],
                                jnp.where((lane >> 6) == hh % 2, qr, 0.0).astype(BF16)], axis=1)
        ks = [jnp.concatenate([kv_ref[:, 2 * hh * C_NOPE:(2 * hh + 1) * C_NOPE], kr_new], axis=1)]
        if latent:
            ks.insert(0, jnp.concatenate([kvc_ref[:, 2 * hh * C_NOPE:(2 * hh + 1) * C_NOPE], kr_old], axis=1))
        scores.append([_dot_nt(qcat, kk) for kk in ks])
    parts = [_softmax_parts(s, scale) for s in scores]
    for hh in range(heads):
        vs = [kv_ref[:, (2 * hh + 1) * C_NOPE:(2 * hh + 2) * C_NOPE]]
        if latent:
            vs.insert(0, kvc_ref[:, (2 * hh + 1) * C_NOPE:(2 * hh + 2) * C_NOPE])
        ps, inv = parts[hh]
        o_ref[:, hh * C_DV:(hh + 1) * C_DV] = (_pv(ps, vs) * inv).astype(o_ref.dtype)


def _mla_attn(qn, qr, kv, kr2, row0, n_seq, t_len, cache=None, rope=None, bq=256, prev=None, heads=2):
    latent = cache is not None
    rb = row0 // t_len
    qb0 = row0 // bq
    nqb = t_len // bq
    group_kv = heads * (C_NOPE + C_DV)
    in_specs = [
        pl.BlockSpec((bq, heads * C_NOPE), lambda n, h, b: (qb0 + n * nqb + b, h)),
        pl.BlockSpec((bq, heads * C_ROPE), lambda n, h, b: (qb0 + n * nqb + b, h)),
        pl.BlockSpec((t_len, group_kv), lambda n, h, b: (rb + n, h)),
        pl.BlockSpec((t_len, 128), lambda n, h, b: (rb + n, 0)),
    ]
    args = [qn, qr, kv, kr2]
    if latent:
        kvc, krc = cache
        cos_t, sin_t = rope
        in_specs += [
            pl.BlockSpec((PAST_LEN, group_kv), lambda n, h, b: (n, h)),
            pl.BlockSpec((PAST_LEN, 128), lambda n, h, b: (n, 0)),
            pl.BlockSpec((bq, 128), lambda n, h, b: (b, 0)),
            pl.BlockSpec((bq, 128), lambda n, h, b: (b, 0)),
            pl.BlockSpec((t_len, 128), lambda n, h, b: (0, 0)),
            pl.BlockSpec((t_len, 128), lambda n, h, b: (0, 0)),
        ]
        args += [kvc, krc, cos_t, sin_t, cos_t, sin_t]
    aliases = _alias_prev(in_specs, args, prev)
    return pl.pallas_call(
        functools.partial(_mla_body, latent=latent, has_prev=prev is not None, heads=heads),
        grid=(n_seq, C_HEADS // heads, nqb),
        in_specs=in_specs,
        out_specs=pl.BlockSpec((bq, heads * C_DV), lambda n, h, b: (qb0 + n * nqb + b, h)),
        out_shape=jax.ShapeDtypeStruct((NTOK, C_HEADS * C_DV), BF16),
        input_output_aliases=aliases,
        scratch_shapes=[pltpu.VMEM((t_len, 128), BF16)],
        compiler_params=_params(("arbitrary", "arbitrary", "arbitrary"), 40),
        name="mla_lat" if latent else "mla_ctx",
    )(*args)


RT_E0, RT_E1, RT_R0, RT_R1, RT_W0, RT_W1 = range(6)
ROUTER_TM = 512


def _router_body(x_ref, sc_ref, sh_ref, w_ref, b_ref, tri_ref, rt_ref, cnt_ref, carry_ref):
    h = x_ref[...] * (1.0 + sc_ref[0, 0]) + sh_ref[0, 0]
    w = w_ref[...]
    h_hi = h.astype(BF16)
    w_hi = w.astype(BF16)
    h_lo = (h - h_hi.astype(F32)).astype(BF16)
    w_lo = (w - w_hi.astype(F32)).astype(BF16)
    lg = _dot(h_hi, w_hi) + (_dot(h_lo, w_hi) + _dot(h_hi, w_lo)) + b_ref[...]

    @pl.when(pl.program_id(0) == 0)
    def _():
        carry_ref[...] = jnp.zeros(carry_ref.shape, F32)

    lane = lax.broadcasted_iota(jnp.int32, lg.shape, 1)
    ninf = -jnp.inf
    is_g = lane < N_GROUPS
    gl = jnp.where(is_g, lg, ninf)
    gmax = jnp.max(gl, axis=1, keepdims=True)
    g_sel = jnp.min(jnp.where(gl == gmax, lane, 128), axis=1, keepdims=True)
    g_w = 1.0 / jnp.sum(jnp.where(is_g, jnp.exp(lg - gmax), 0.0), axis=1, keepdims=True)
    el = jnp.where(((lane - N_GROUPS) >> 2) == g_sel, lg, ninf)
    m1 = jnp.max(el, axis=1, keepdims=True)
    i1 = jnp.min(jnp.where(el == m1, lane, 128), axis=1, keepdims=True)
    el2 = jnp.where(lane == i1, ninf, el)
    m2 = jnp.max(el2, axis=1, keepdims=True)
    i2 = jnp.min(jnp.where(el2 == m2, lane, 128), axis=1, keepdims=True)
    p2 = jnp.exp(m2 - m1)
    w0 = g_w / (1.0 + p2)
    w1 = w0 * p2
    e0 = i1 - N_GROUPS
    e1 = i2 - N_GROUPS
    hit0 = lane == e0
    hit1 = lane == e1
    onehot = jnp.where(hit0, 1.0, jnp.where(hit1, 1.0, 0.0))
    before = _dot(tri_ref[...], onehot.astype(BF16)) + carry_ref[...]
    r0 = jnp.sum(jnp.where(hit0, before, 0.0), axis=1, keepdims=True)
    r1 = jnp.sum(jnp.where(hit1, before, 0.0), axis=1, keepdims=True)
    carry = carry_ref[...] + jnp.sum(onehot, axis=0, keepdims=True)
    carry_ref[...] = carry
    cnt_ref[...] = jnp.broadcast_to(carry, cnt_ref.shape)
    rec = [e0.astype(F32), e1.astype(F32), r0, r1, w0, w1]
    out = jnp.zeros(lg.shape, F32)
    for idx, val in enumerate(rec):
        out = jnp.where(lane == idx, val, out)
    rt_ref[...] = out


def _router(x, mod4, layer, w_r, b_r, tri):
    tm = ROUTER_TM
    return pl.pallas_call(
        _router_body,
        grid=(NTOK // tm,),
        in_specs=[pl.BlockSpec((tm, D_MODEL), lambda i: (i, 0)),
                  _mod_spec(layer, 4, tm), _mod_spec(layer, 3, tm),
                  pl.BlockSpec((D_MODEL, 128), lambda i: (0, 0)),
                  pl.BlockSpec((1, 128), lambda i: (0, 0)),
                  pl.BlockSpec((tm, tm), lambda i: (0, 0))],
        out_specs=[pl.BlockSpec((tm, 128), lambda i: (i, 0)),
                   pl.BlockSpec((8, 128), lambda i: (0, 0))],
        out_shape=[jax.ShapeDtypeStruct((NTOK, 128), F32), jax.ShapeDtypeStruct((8, 128), F32)],
        scratch_shapes=[pltpu.VMEM((1, 128), F32)],
        compiler_params=_params(("arbitrary",), 40),
        name="moe_router",
    )(x, mod4, mod4, w_r, b_r, tri)


DISPATCH_TM = 512


def _dispatch_body(dest_ref, pend_ref, padded_ref, x_ref, sc_ref, sh_ref, xs_hbm, hbuf, zbuf, sem, zsem):
    tm = DISPATCH_TM
    i = pl.program_id(0)

    @pl.when(i == 0)
    def _():
        zbuf[...] = jnp.zeros(zbuf.shape, F32)

        def tail(e):
            start = pl.multiple_of((pend_ref[e] - MOE_TM) // SUB, MOE_TM // SUB)
            return pltpu.make_async_copy(zbuf, xs_hbm.at[pl.ds(start, MOE_TM // SUB)], zsem)

        for e in range(N_EXPERTS):
            @pl.when(padded_ref[e] > 0)
            def _():
                tail(e).start()
        for e in range(N_EXPERTS):
            @pl.when(padded_ref[e] > 0)
            def _():
                tail(e).wait()

    h = x_ref[...] * (1.0 + sc_ref[0, 0]) + sh_ref[0, 0]
    hbuf[...] = h.reshape(tm // SUB, SUB, D_MODEL)

    def send(g, carry):
        for s in range(SUB):
            for k in range(EXPERT_TOP_K):
                d = dest_ref[k * NTOK + i * tm + g * SUB + s]
                pltpu.make_async_copy(hbuf.at[g, pl.ds(s, 1), :], _row(xs_hbm, d), sem).start()
        return carry
    lax.fori_loop(0, tm // SUB, send, 0)

    def drain(t, carry):
        for k in range(EXPERT_TOP_K):
            pltpu.make_async_copy(hbuf.at[0, pl.ds(0, 1), :], _row(xs_hbm, 0), sem).wait()
        return carry
    lax.fori_loop(0, tm, drain, 0, unroll=8)


def _dispatch(x, dest, pend, padded, mod4, layer):
    tm = DISPATCH_TM
    grid_spec = pltpu.PrefetchScalarGridSpec(
        num_scalar_prefetch=3,
        grid=(NTOK // tm,),
        in_specs=[pl.BlockSpec((tm, D_MODEL), lambda i, *_: (i, 0)),
                  _mod_spec(layer, 4, tm), _mod_spec(layer, 3, tm)],
        out_specs=pl.BlockSpec(memory_space=pl.ANY),
        scratch_shapes=[pltpu.VMEM((tm // SUB, SUB, D_MODEL), F32), pltpu.VMEM((MOE_TM // SUB, SUB, D_MODEL), F32),
                        pltpu.SemaphoreType.DMA(()), pltpu.SemaphoreType.DMA(())],
    )
    xs = pl.pallas_call(
        _dispatch_body,
        grid_spec=grid_spec,
        out_shape=jax.ShapeDtypeStruct((MOE_ROWS // SUB, SUB, D_MODEL), F32),
        compiler_params=_params(("arbitrary",), 40),
        name="moe_dispatch",
    )(dest, pend, padded, x, mod4, mod4)
    return xs.reshape(MOE_ROWS, D_MODEL)


def _expert_body(te_ref, tv_ref, x_ref, wg_ref, wu_ref, wd_ref, o_ref, wg_s, wu_s, wd_s):
    t = pl.program_id(0)
    prev = te_ref[jnp.maximum(t - 1, 0)]

    @pl.when((t == 0) | (te_ref[t] != prev))
    def _():
        wg_s[...] = wg_ref[...].astype(BF16)
        wu_s[...] = wu_ref[...].astype(BF16)
        wd_s[...] = wd_ref[...].astype(BF16)

    @pl.when(tv_ref[t] > 0)
    def _():
        x = x_ref[...].astype(BF16)
        g = _dot(x, wg_s[...])
        u = _dot(x, wu_s[...])
        act = (g / (1.0 + jnp.exp(-g))) * u
        o_ref[...] = _dot(act.astype(BF16), wd_s[...])

    @pl.when(tv_ref[t] == 0)
    def _():
        o_ref[...] = jnp.zeros(o_ref.shape, o_ref.dtype)


def _experts(xs, tile_e, tile_v, w_gate, w_up, w_down, layer):
    grid_spec = pltpu.PrefetchScalarGridSpec(
        num_scalar_prefetch=2,
        grid=(MOE_TILES,),
        in_specs=[
            pl.BlockSpec((MOE_TM, D_MODEL), lambda t, te, tv: (jnp.where(tv[t] > 0, t, 0), 0)),
            pl.BlockSpec((None, None, D_MODEL, D_EXPERT), lambda t, te, tv: (layer, te[t], 0, 0)),
            pl.BlockSpec((None, None, D_MODEL, D_EXPERT), lambda t, te, tv: (layer, te[t], 0, 0)),
            pl.BlockSpec((None, None, D_EXPERT, D_MODEL), lambda t, te, tv: (layer, te[t], 0, 0)),
        ],
        out_specs=pl.BlockSpec((MOE_TM, D_MODEL), lambda t, te, tv: (t, 0)),
        scratch_shapes=[pltpu.VMEM((D_MODEL, D_EXPERT), BF16), pltpu.VMEM((D_MODEL, D_EXPERT), BF16),
                        pltpu.VMEM((D_EXPERT, D_MODEL), BF16)],
    )
    return pl.pallas_call(
        _expert_body,
        grid_spec=grid_spec,
        out_shape=jax.ShapeDtypeStruct((MOE_ROWS, D_MODEL), F32),
        compiler_params=_params(("arbitrary",), 56),
        name="moe_experts",
    )(tile_e, tile_v, xs, w_gate, w_up, w_down)


def _layout(route, counts):
    counts = counts[0, :N_EXPERTS].astype(jnp.int32)
    padded = ((counts + MOE_TM - 1) // MOE_TM) * MOE_TM
    pend = jnp.cumsum(padded)
    pstart = pend - padded
    experts = jnp.arange(N_EXPERTS, dtype=jnp.int32)[None, :]

    def dest_of(e_lane, r_lane):
        e = route[:, e_lane].astype(jnp.int32)
        base = jnp.sum(jnp.where(e[:, None] == experts, pstart[None, :], 0), axis=1)
        return base + route[:, r_lane].astype(jnp.int32)

    dest = jnp.concatenate([dest_of(RT_E0, RT_R0), dest_of(RT_E1, RT_R1)])
    tile_start = jnp.arange(MOE_TILES, dtype=jnp.int32) * MOE_TM
    tile_e = jnp.minimum(jnp.sum((tile_start[:, None] >= pend[None, :]).astype(jnp.int32), axis=1), N_EXPERTS - 1)
    tile_v = (tile_start < pend[-1]).astype(jnp.int32)
    return dest, pend.astype(jnp.int32), padded, tile_e.astype(jnp.int32), tile_v


def _moe(x, mod4, layer, w_r, b_r, tri, w_gate, w_up, w_down, ln_g, ln_b):
    route, counts = _router(x, mod4, layer, w_r, b_r, tri)
    dest, pend, padded, tile_e, tile_v = _layout(route, counts)
    xs = _dispatch(x, dest, pend, padded, mod4, layer)
    rows = _experts(xs, tile_e, tile_v, w_gate, w_up, w_down, layer)
    return _combine_ln(x, rows, dest, route, mod4, layer, 5, ln_g, ln_b)


def kernel(x_prompt, x_sample, state_gla, cache_diff_k, cache_diff_v, cache_mla_ckv, cache_mla_krope, c, c_ctx, ada_w, ada_b, ln_g, ln_b, ab_w_in, gla_w_gate2, gla_b_gate2, gla_norm_g, diff_lambda, diff_norm_g, ab_w_out, mla_w_in, mla_q_norm_g, mla_w_uq, mla_kv_norm_g, mla_w_ukv, mla_w_out, moe_w_rg, moe_b_rg, moe_w_re, moe_b_re, moe_w_gate, moe_w_up, moe_w_down):
    x = (x_prompt.reshape(NCTX, D_MODEL), x_sample.reshape(NLAT, D_MODEL))
    cond = jnp.concatenate([c_ctx[None, :], c, jnp.zeros((N_COND - 1 - DEC_BATCH, D_MODEL), F32)], axis=0)
    mod4 = _ada_mod(cond, ada_w, ada_b).reshape(DEPTH, N_COND, 1, 6 * D_MODEL)
    cos_t, sin_t = _rope_tables()
    levf = jnp.asarray(_gla_level_table(False))
    levb = jnp.asarray(_gla_level_table(True))
    tri = jnp.asarray(np.tril(np.ones((ROUTER_TM, ROUTER_TM), np.float32), -1), BF16)
    cache_k = cache_diff_k.reshape(DEC_BATCH, -1, PAST_LEN, B_HEADS * 2 * B_DH)
    cache_v = cache_diff_v.reshape(DEC_BATCH, -1, PAST_LEN, B_HEADS * B_DV)

    new_gla, new_dk, new_dv, new_ckv, new_kr = [], [], [], [], []
    for layer in range(DEPTH):
        i = layer // 2
        if layer % 2 == 0:
            w_ab = jnp.concatenate([ab_w_in[i, :, :PA_N], ab_w_in[i, :, AB_D0:], ab_w_in[i, :, AB_GG0:AB_D0],
                                    jnp.zeros((D_MODEL, P_N - PD_GG - 2 * A_GATE_RANK), F32)],
                                   axis=1).astype(BF16)[None]
            pa = pd = _mod_matmul(x, mod4, layer, w_ab, 0, P_N, P_TN, "ab_in")
            wg = jnp.zeros((A_HEADS, 128, 2 * A_DK), F32)
            for j in range(2):
                blk = gla_w_gate2[i, j].reshape(A_GATE_RANK, A_HEADS, A_DK).transpose(1, 0, 2)
                wg = wg.at[:, j * A_GATE_RANK:(j + 1) * A_GATE_RANK, j * A_DK:(j + 1) * A_DK].set(blk)
            bg = gla_b_gate2[i].reshape(2, A_HEADS, A_DK).transpose(1, 0, 2).reshape(A_HEADS, 1, 2 * A_DK)
            g_gla = gla_norm_g[i][None, :]
            o_gla_c, st = _gla(pa, pd, wg, bg, g_gla, levf, levb, 0, BATCH, SEQ, None, 0, True)
            (o_gla,) = _gla(pa, pd, wg, bg, g_gla, levf, levb, NCTX, DEC_BATCH, DEC_SEQ, state_gla, i, False,
                            prev=o_gla_c)
            lv = diff_lambda[i]
            lam_init = 0.8 - 0.6 * math.exp(-0.3 * layer)
            lam = (jnp.exp(jnp.sum(lv[0] * lv[1])) - jnp.exp(jnp.sum(lv[2] * lv[3])) + lam_init).reshape(1, 1)
            g_diff = diff_norm_g[i][None, :]
            o_diff_c = _diff_attn(pd, lam, g_diff, lam_init, 0, BATCH, SEQ, heads=8)
            o_diff = _diff_attn(pd, lam, g_diff, lam_init, NCTX, DEC_BATCH, DEC_SEQ,
                                cache=(cache_k, cache_v, i), rope=(cos_t, sin_t), bq=DEC_SEQ, prev=o_diff_c)
            x = _proj_ln([o_gla, o_diff], ab_w_out[i].astype(BF16)[None], 0, x, mod4, layer, 2,
                         ln_g[layer, 0][None, :], ln_b[layer, 0][None, :], "ab_out_ln")
            new_gla.append(st)
            new_dk.append(pd[:NCTX, PD_K:PD_V].reshape(BATCH, SEQ, B_HEADS, 2, B_DH))
            new_dv.append(pd[:NCTX, PD_V:PD_GG].reshape(BATCH, SEQ, B_HEADS, B_DV))
        else:
            c_in = C_Q_RANK + C_KV_RANK + C_ROPE
            pm = _mod_matmul(x, mod4, layer, mla_w_in[i].astype(BF16)[None], 0, c_in, c_in, "mla_in")
            w_uq = mla_w_uq[i].reshape(C_Q_RANK, C_HEADS, C_NOPE + C_ROPE)
            w_uq = jnp.concatenate([w_uq[:, :, :C_NOPE].reshape(C_Q_RANK, -1),
                                    w_uq[:, :, C_NOPE:].reshape(C_Q_RANK, -1)], axis=1).astype(BF16)
            w_ukv = mla_w_ukv[i].astype(BF16)
            g_q = mla_q_norm_g[i][None, :]
            g_kv = mla_kv_norm_g[i][None, :]
            qn = _rms_matmul(pm, 0, C_Q_RANK, g_q, w_uq, 0, C_HEADS * C_NOPE, 512, 1024, BF16, "mla_q_nope")
            qr = _rms_matmul(pm, 0, C_Q_RANK, g_q, w_uq, C_HEADS * C_NOPE // 512, C_HEADS * C_ROPE, 512, 1024, F32,
                             "mla_q_rope")
            kv, ckv_n = _rms_matmul(pm, C_Q_RANK // C_KV_RANK, C_KV_RANK, g_kv, w_ukv, 0,
                                    C_HEADS * (C_NOPE + C_DV), 1024, 1024, BF16, "mla_kv", emit_norm=True)
            kv_c = _rms_matmul(cache_mla_ckv[:, i].reshape(DEC_BATCH * PAST_LEN, C_KV_RANK), 0, C_KV_RANK, g_kv,
                               w_ukv, 0, C_HEADS * (C_NOPE + C_DV), 1024, 1024, BF16, "mla_kv_cache",
                               norm=False)
            krope = pm[:, C_Q_RANK + C_KV_RANK:]
            kr2 = jnp.concatenate([krope, krope], axis=1)
            krc = cache_mla_krope[:, i].reshape(DEC_BATCH * PAST_LEN, C_ROPE)
            krc2 = jnp.concatenate([krc, krc], axis=1)
            o_c = _mla_attn(qn, qr, kv, kr2, 0, BATCH, SEQ, heads=8)
            o_mla = _mla_attn(qn, qr, kv, kr2, NCTX, DEC_BATCH, DEC_SEQ, cache=(kv_c, krc2), rope=(cos_t, sin_t),
                              bq=DEC_SEQ, prev=o_c, heads=2)
            x = _proj_ln([o_mla], mla_w_out[i].astype(BF16)[None], 0, x, mod4, layer, 2,
                         ln_g[layer, 0][None, :], ln_b[layer, 0][None, :], "mla_out_ln")
            new_ckv.append(ckv_n[:NCTX].reshape(BATCH, SEQ, C_KV_RANK))
            new_kr.append(krope[:NCTX].reshape(BATCH, SEQ, C_ROPE))
        w_r = jnp.concatenate([moe_w_rg[layer], moe_w_re[layer],
                               jnp.zeros((D_MODEL, 128 - N_GROUPS - N_EXPERTS), F32)], axis=1)
        b_r = jnp.concatenate([moe_b_rg[layer], moe_b_re[layer],
                               jnp.zeros((128 - N_GROUPS - N_EXPERTS,), F32)])[None, :]
        x = _moe(x, mod4, layer, w_r, b_r, tri, moe_w_gate, moe_w_up, moe_w_down,
                 ln_g[layer, 1][None, :], ln_b[layer, 1][None, :])

    return (x[:NCTX].reshape(BATCH, SEQ, D_MODEL),
            x[NCTX:].reshape(DEC_BATCH, DEC_SEQ, D_MODEL),
            jnp.stack(new_gla, axis=1), jnp.stack(new_dk, axis=1), jnp.stack(new_dv, axis=1),
            jnp.stack(new_ckv, axis=1), jnp.stack(new_kr, axis=1))
```

```python
import functools
import math

import numpy as np
import jax
import jax.numpy as jnp
from jax import lax
from jax.experimental import pallas as pl
from jax.experimental.pallas import tpu as pltpu

F32 = jnp.float32
BF16 = jnp.bfloat16

D_MODEL = 2048
BATCH = 16
SEQ = 256
DEPTH = 4
DEC_BATCH = 4
DEC_SEQ = 1024
PAST_LEN = 256
GRID_W = 64
A_HEADS = 4
A_DK = 128
A_DV = 256
A_GATE_RANK = 16
A_TAU = 16.0
B_HEADS = 8
B_DH = 64
B_DV = 2 * B_DH
C_HEADS = 16
C_Q_RANK = 512
C_KV_RANK = 256
C_NOPE = 128
C_ROPE = 64
C_DV = 128
N_GROUPS = 4
EXPERTS_PER_GROUP = 4
N_EXPERTS = N_GROUPS * EXPERTS_PER_GROUP
EXPERT_TOP_K = 2
D_EXPERT = 512
ROPE_THETA = 10000.0
LN_EPS = 1e-5
RMS_EPS = 1e-6
DEEPNORM_ALPHA = (2.0 * DEPTH) ** 0.25

NCTX = BATCH * SEQ
NLAT = DEC_BATCH * DEC_SEQ
NTOK = NCTX + NLAT
N_COND = 8
GLA_CHUNK = 256
GLA_LEVELS = 8
MOE_TM = 512
LOG2E = 1.4426950408889634
MOE_ROWS = NTOK * EXPERT_TOP_K + N_EXPERTS * MOE_TM
MOE_TILES = MOE_ROWS // MOE_TM

PA_Q, PA_K, PA_V, PA_R = 0, A_HEADS * A_DK, 2 * A_HEADS * A_DK, 2 * A_HEADS * A_DK + A_HEADS * A_DV
PA_N = PA_R + A_HEADS * A_DV
AB_GG0 = PA_N
AB_D0 = PA_N + 2 * A_GATE_RANK
PD_Q = PA_N
PD_K = PD_Q + B_HEADS * 2 * B_DH
PD_V = PD_K + B_HEADS * 2 * B_DH
PD_GG = PD_V + B_HEADS * B_DV
P_TN = 1280
P_N = -(-(PD_GG + 2 * A_GATE_RANK) // P_TN) * P_TN


def _params(sem, vmem_mb):
    return pltpu.CompilerParams(dimension_semantics=sem, vmem_limit_bytes=vmem_mb * 2 ** 20)


def _cond_row(i, tm):
    return jnp.where(i * tm < NCTX, 0, 1 + (i * tm - NCTX) // DEC_SEQ)


def _dot(a, b):
    return jnp.dot(a, b, preferred_element_type=F32)


def _dot_nt(a, b):
    return lax.dot_general(a, b, (((1,), (1,)), ((), ())), preferred_element_type=F32)


def _dot_tn(a, b):
    return lax.dot_general(a, b, (((0,), (0,)), ((), ())), preferred_element_type=F32)


def _ada_body(c_ref, w_ref, b_ref, o_ref):
    c = c_ref[...]
    a = (c / (1.0 + jnp.exp(-c))).astype(BF16)
    o_ref[0] = _dot(a, w_ref[0].astype(BF16)) + b_ref[0]


def _ada_mod(cond, ada_w, ada_b):
    tn = 1024
    return pl.pallas_call(
        _ada_body,
        grid=(DEPTH, 6 * D_MODEL // tn),
        in_specs=[pl.BlockSpec((N_COND, D_MODEL), lambda l, j: (0, 0)),
                  pl.BlockSpec((1, D_MODEL, tn), lambda l, j: (l, 0, j)),
                  pl.BlockSpec((1, 1, tn), lambda l, j: (l, 0, j))],
        out_specs=pl.BlockSpec((1, N_COND, tn), lambda l, j: (l, 0, j)),
        out_shape=jax.ShapeDtypeStruct((DEPTH, N_COND, 6 * D_MODEL), F32),
        compiler_params=_params(("arbitrary", "arbitrary"), 40),
        name="ada_mod",
    )(cond, ada_w, ada_b.reshape(DEPTH, 1, 6 * D_MODEL))


def _mod_spec(layer, chunk, tm):
    return pl.BlockSpec((1, 1, 1, D_MODEL), lambda i, *_: (layer, _cond_row(i, tm), 0, chunk))


def _stream_specs(x, tm, single_buffer=False):
    if not isinstance(x, tuple):
        return [pl.BlockSpec((tm, D_MODEL), lambda i, *_: (i, 0))], [x]
    n_c = NCTX // tm
    mode = dict(pipeline_mode=pl.Buffered(1)) if single_buffer else {}
    return ([pl.BlockSpec((tm, D_MODEL), lambda i, *_: (jnp.minimum(i, n_c - 1), 0), **mode),
             pl.BlockSpec((tm, D_MODEL), lambda i, *_: (jnp.maximum(i - n_c, 0), 0), **mode)], list(x))


def _for_stream(x_refs, tm, fn):
    if len(x_refs) == 1:
        fn(x_refs[0])
        return
    n_c = NCTX // tm
    pl.when(pl.program_id(0) < n_c)(lambda: fn(x_refs[0]))
    pl.when(pl.program_id(0) >= n_c)(lambda: fn(x_refs[1]))


def _mod_mm_body(*refs, n_x, tm):
    x_refs = refs[:n_x]
    sc_ref, sh_ref, w_ref, o_ref, h_ref = refs[n_x:]

    @pl.when(pl.program_id(1) == 0)
    def _():
        def modulate(x_ref):
            h_ref[...] = (x_ref[...] * (1.0 + sc_ref[0, 0]) + sh_ref[0, 0]).astype(BF16)
        _for_stream(x_refs, tm, modulate)

    o_ref[...] = _dot(h_ref[...], w_ref[...]).astype(o_ref.dtype)


def _mod_matmul(x, mod4, layer, w, w_layer, n_out, tn, name, tm=1024, out_dtype=F32):
    x_specs, x_args = _stream_specs(x, tm, single_buffer=True)
    return pl.pallas_call(
        functools.partial(_mod_mm_body, n_x=len(x_args), tm=tm),
        grid=(NTOK // tm, n_out // tn),
        in_specs=x_specs + [_mod_spec(layer, 1, tm), _mod_spec(layer, 0, tm),
                            pl.BlockSpec((None, D_MODEL, tn), lambda i, j: (w_layer, 0, j))],
        out_specs=pl.BlockSpec((tm, tn), lambda i, j: (i, j)),
        out_shape=jax.ShapeDtypeStruct((NTOK, n_out), out_dtype),
        scratch_shapes=[pltpu.VMEM((tm, D_MODEL), BF16)],
        compiler_params=_params(("arbitrary", "arbitrary"), 56),
        name=name,
    )(*x_args, mod4, mod4, w)


def _rms_mm_body(x_ref, g_ref, w_ref, *rest, norm, emit_norm):
    if emit_norm:
        o_ref, n_ref, h_ref = rest
    else:
        o_ref, h_ref = rest

    @pl.when(pl.program_id(1) == 0)
    def _():
        x = x_ref[...]
        if norm:
            x = x * lax.rsqrt(jnp.mean(x * x, axis=-1, keepdims=True) + RMS_EPS) * g_ref[...]
        h_ref[...] = x.astype(BF16)
        if emit_norm:
            n_ref[...] = x

    o_ref[...] = _dot(h_ref[...], w_ref[...]).astype(o_ref.dtype)


def _rms_matmul(x, col_block, k, g, w, col0_blocks, n_out, tn, tm, out_dtype, name, norm=True, emit_norm=False):
    rows = x.shape[0]
    out_shape = [jax.ShapeDtypeStruct((rows, n_out), out_dtype)]
    out_specs = [pl.BlockSpec((tm, tn), lambda i, j: (i, j))]
    if emit_norm:
        out_shape.append(jax.ShapeDtypeStruct((rows, k), F32))
        out_specs.append(pl.BlockSpec((tm, k), lambda i, j: (i, 0)))
    res = pl.pallas_call(
        functools.partial(_rms_mm_body, norm=norm, emit_norm=emit_norm),
        grid=(rows // tm, n_out // tn),
        in_specs=[pl.BlockSpec((tm, k), lambda i, j: (i, col_block)),
                  pl.BlockSpec((1, k), lambda i, j: (0, 0)),
                  pl.BlockSpec((k, tn), lambda i, j: (0, col0_blocks + j))],
        out_specs=out_specs,
        out_shape=out_shape,
        scratch_shapes=[pltpu.VMEM((tm, k), BF16)],
        compiler_params=_params(("arbitrary", "arbitrary"), 40),
        name=name,
    )(x, g, w)
    return res if emit_norm else res[0]


def _layer_norm(z, g, b):
    mu = jnp.mean(z, axis=-1, keepdims=True)
    zc = z - mu
    var = jnp.mean(zc * zc, axis=-1, keepdims=True)
    return zc * lax.rsqrt(var + LN_EPS) * g + b


LN_ROWS = 256


def _proj_ln_body(*refs, n_lhs, n_x, n_j, tm, tn):
    lhs = refs[:n_lhs]
    ws = refs[n_lhs:2 * n_lhs]
    x_refs = refs[2 * n_lhs:2 * n_lhs + n_x]
    gate_ref, g_ref, b_ref, o_ref = refs[2 * n_lhs + n_x:]
    j = pl.program_id(1)
    y = _dot(lhs[0][...], ws[0][...])
    for a, w in zip(lhs[1:], ws[1:]):
        y = y + _dot(a[...], w[...])
    o_ref[:, pl.ds(pl.multiple_of(j * tn, tn), tn)] = y

    @pl.when(j == n_j - 1)
    def _():
        def normalise(x_ref):
            def body(r, carry):
                rows = pl.ds(pl.multiple_of(r * LN_ROWS, LN_ROWS), LN_ROWS)
                z = DEEPNORM_ALPHA * x_ref[rows, :] + gate_ref[0, 0] * o_ref[rows, :]
                o_ref[rows, :] = _layer_norm(z, g_ref[...], b_ref[...])
                return carry
            lax.fori_loop(0, tm // LN_ROWS, body, 0)
        _for_stream(x_refs, tm, normalise)


def _proj_ln(lhs_list, w, w_layer, x, mod4, layer, gate_chunk, ln_g, ln_b, name, tm=1024, tn=1024):
    n_lhs = len(lhs_list)
    kk = lhs_list[0].shape[1]
    n_j = D_MODEL // tn
    in_specs = [pl.BlockSpec((tm, kk), lambda i, j: (i, 0)) for _ in lhs_list]
    in_specs += [pl.BlockSpec((None, kk, tn), functools.partial(lambda i, j, r: (w_layer, r, j), r=r))
                 for r in range(n_lhs)]
    x_specs, x_args = _stream_specs(x, tm, single_buffer=True)
    in_specs += x_specs + [_mod_spec(layer, gate_chunk, tm),
                           pl.BlockSpec((1, D_MODEL), lambda i, j: (0, 0)),
                           pl.BlockSpec((1, D_MODEL), lambda i, j: (0, 0))]
    return pl.pallas_call(
        functools.partial(_proj_ln_body, n_lhs=n_lhs, n_x=len(x_args), n_j=n_j, tm=tm, tn=tn),
        grid=(NTOK // tm, n_j),
        in_specs=in_specs,
        out_specs=pl.BlockSpec((tm, D_MODEL), lambda i, j: (i, 0)),
        out_shape=jax.ShapeDtypeStruct((NTOK, D_MODEL), F32),
        compiler_params=_params(("arbitrary", "arbitrary"), 56),
        name=name,
    )(*lhs_list, *([w] * n_lhs), *x_args, mod4, ln_g, ln_b)


COMBINE_TM = 256


SUB = 8


def _row(ref, i):
    return ref.at[i >> 3, pl.ds(i & (SUB - 1), 1), :]


def _combine_ln_body(dest_ref, x_ref, rt_ref, gate_ref, g_ref, b_ref, rows_hbm, o_ref, buf, sem):
    tm = COMBINE_TM
    i = pl.program_id(0)
    n_i = pl.num_programs(0)

    def fetch(step, slot):
        def body(g, carry):
            for s in range(SUB):
                for k in range(EXPERT_TOP_K):
                    d = dest_ref[k * NTOK + step * tm + g * SUB + s]
                    pltpu.make_async_copy(_row(rows_hbm, d), buf.at[slot, k, g, pl.ds(s, 1), :],
                                          sem.at[slot]).start()
            return carry
        lax.fori_loop(0, tm // SUB, body, 0)

    @pl.when(i == 0)
    def _():
        fetch(0, 0)

    @pl.when(i + 1 < n_i)
    def _():
        fetch(i + 1, (i + 1) % 2)

    slot = i % 2

    def drain(t, carry):
        for k in range(EXPERT_TOP_K):
            pltpu.make_async_copy(_row(rows_hbm, 0), buf.at[slot, k, 0, pl.ds(0, 1), :], sem.at[slot]).wait()
        return carry
    lax.fori_loop(0, tm, drain, 0, unroll=8)

    rt = rt_ref[...]
    y0 = buf[slot, 0].reshape(tm, D_MODEL)
    y1 = buf[slot, 1].reshape(tm, D_MODEL)
    y = rt[:, RT_W0:RT_W0 + 1] * y0 + rt[:, RT_W1:RT_W1 + 1] * y1
    z = DEEPNORM_ALPHA * x_ref[...] + gate_ref[0, 0] * y
    o_ref[...] = _layer_norm(z, g_ref[...], b_ref[...])


def _combine_ln(x, rows, dest, route, mod4, layer, gate_chunk, ln_g, ln_b):
    tm = COMBINE_TM
    row = pl.BlockSpec((tm, D_MODEL), lambda i, d: (i, 0))
    vec = pl.BlockSpec((1, D_MODEL), lambda i, d: (0, 0))
    grid_spec = pltpu.PrefetchScalarGridSpec(
        num_scalar_prefetch=1,
        grid=(NTOK // tm,),
        in_specs=[row, pl.BlockSpec((tm, 128), lambda i, d: (i, 0)),
                  _mod_spec(layer, gate_chunk, tm), vec, vec,
                  pl.BlockSpec(memory_space=pl.ANY)],
        out_specs=row,
        scratch_shapes=[pltpu.VMEM((2, EXPERT_TOP_K, tm // SUB, SUB, D_MODEL), F32),
                        pltpu.SemaphoreType.DMA((2,))],
    )
    return pl.pallas_call(
        _combine_ln_body,
        grid_spec=grid_spec,
        out_shape=jax.ShapeDtypeStruct((NTOK, D_MODEL), F32),
        compiler_params=_params(("arbitrary",), 40),
        name="moe_combine_ln",
    )(dest, x, route, mod4, ln_g, ln_b, rows.reshape(MOE_ROWS // SUB, SUB, D_MODEL))


def _gla_level_table(reverse):
    t = np.arange(GLA_CHUNK)[:, None]
    s = np.arange(GLA_CHUNK)[None, :]
    x = t ^ s
    lev = np.where(x > 0, np.floor(np.log2(np.maximum(x, 1))).astype(np.int32), -1)
    live = (t < s) if reverse else (t > s)
    return np.where(live | (t == s), lev, -2).astype(np.int32)


def _anchor_rows(u, level, reverse):
    c = u.shape[0]
    n = c // 8
    u3 = u.reshape(n, 8, 128)
    half = 1 << level
    if level < 3:
        blk = 2 * half
        sub = lax.broadcasted_iota(jnp.int32, (n, 8, 128), 1)
        out = None
        for j in range(8 // blk):
            r = j * blk + (half if reverse else half - 1)
            piece = jnp.broadcast_to(u3[:, r:r + 1, :], (n, 8, 128))
            out = piece if out is None else jnp.where(sub >= j * blk, piece, out)
        return out.reshape(c, 128)
    m = 1 << (level - 2)
    r = 0 if reverse else 7
    e = jnp.broadcast_to(u3[:, r:r + 1, :], (n, 8, 128)).reshape(n // m, m, 8, 128)
    idx = m // 2 if reverse else m // 2 - 1
    return jnp.broadcast_to(e[:, idx:idx + 1], (n // m, m, 8, 128)).reshape(c, 128)


def _gla_intra(q, k, la, lev, reverse):
    rowk = lax.broadcasted_iota(jnp.int32, (GLA_CHUNK, 128), 0)
    scores = jnp.where(lev == -1, _dot_nt(q.astype(BF16), k.astype(BF16)), 0.0)
    u = la
    for level in range(GLA_LEVELS):
        anchor = _anchor_rows(u, level, reverse)
        qa = (q * jnp.exp2(u)).astype(BF16)
        ka = (k * jnp.exp2(jnp.minimum(anchor - u, 0.0))).astype(BF16)
        scores = jnp.where(lev == level, _dot_nt(qa, ka), scores)
        query_side = ((rowk >> level) & 1) == (0 if reverse else 1)
        u = u + jnp.where(query_side, anchor, 0.0)
    return scores, u


def _gla_body(*refs, t_len, has_s0, emit_state, has_prev):
    it = iter(refs)
    q_ref, k_ref, v_ref, r_ref, gg_ref, wg_ref, bg_ref, g_ref, levf_ref, levb_ref = [next(it) for _ in range(10)]
    s0_ref = next(it) if has_s0 else None
    if has_prev:
        next(it)
    o_ref = next(it)
    st_ref = next(it) if emit_state else None
    acc_ref, sf_ref, sb_ref, la_ref = [next(it) for _ in range(4)]
    c = GLA_CHUNK
    n_ch = t_len // c

    gl = _dot(gg_ref[...].astype(BF16), wg_ref[...].astype(BF16)) + bg_ref[...]
    la_ref[...] = (jnp.minimum(gl, 0.0) - jnp.log1p(jnp.exp(-jnp.abs(gl)))) * (LOG2E / A_TAU)
    if has_s0:
        sf_ref[...] = s0_ref[0]
        sb_ref[...] = s0_ref[1]

    eye = (lax.broadcasted_iota(jnp.int32, (A_DK, A_DK), 0) == lax.broadcasted_iota(jnp.int32, (A_DK, A_DK), 1))

    def chunk_step(ci, reverse, with_state):
        rows = pl.ds(ci * c if isinstance(ci, int) else pl.multiple_of(ci * c, c), c)
        q = q_ref[rows, :] * (A_DK ** -0.5)
        k = k_ref[rows, :]
        v = v_ref[rows, :].astype(BF16)
        la = la_ref[rows, A_DK:2 * A_DK] if reverse else la_ref[rows, 0:A_DK]
        s_ref = sb_ref if reverse else sf_ref
        scores, cum = _gla_intra(q, k, la, (levb_ref if reverse else levf_ref)[...], reverse)
        o = _dot(scores.astype(BF16), v)
        if with_state:
            o = o + _dot((q * jnp.exp2(cum)).astype(BF16), s_ref[...].astype(BF16))
        tot = jnp.broadcast_to(cum[0:1, :] if reverse else cum[c - 1:c, :], (c, A_DK))
        delta = _dot_tn((k * jnp.exp2(tot - cum)).astype(BF16), v)
        if with_state:
            decay = jnp.sum(jnp.where(eye, jnp.exp2(tot[0:A_DK, :]), 0.0), axis=1, keepdims=True)
            s_ref[...] = decay * s_ref[...] + delta
        else:
            s_ref[...] = delta
        if not reverse:
            acc_ref[rows, :] = o
        else:
            y = acc_ref[rows, :] + o
            y = y * lax.rsqrt(jnp.mean(y * y, axis=-1, keepdims=True) + RMS_EPS) * g_ref[...]
            r = r_ref[rows, :]
            o_ref[rows, :] = (y * (r / (1.0 + jnp.exp(-r)))).astype(o_ref.dtype)

    for reverse in (False, True):
        first = (n_ch - 1) if reverse else 0
        chunk_step(first, reverse, has_s0)
        if n_ch > 1:
            def body(i, carry, reverse=reverse):
                chunk_step((n_ch - 1 - i) if reverse else i, reverse, True)
                return carry
            lax.fori_loop(1, n_ch, body, 0)

    if emit_state:
        st_ref[0] = sf_ref[...]
        st_ref[1] = sb_ref[...]


def _alias_prev(in_specs, args, prev):
    if prev is None:
        return {}
    in_specs.append(pl.BlockSpec(memory_space=pl.ANY))
    args.append(prev)
    return {len(args) - 1: 0}


def _gla(pa, pd, wg, bg, gla_g, levf, levb, row0, n_seq, t_len, s0, s0_layer, emit_state, prev=None):
    rb = row0 // t_len
    kb, vb, rbk = PA_K // A_DK, PA_V // A_DV, PA_R // A_DV
    in_specs = [
        pl.BlockSpec((t_len, A_DK), lambda n, h: (rb + n, h)),
        pl.BlockSpec((t_len, A_DK), lambda n, h: (rb + n, kb + h)),
        pl.BlockSpec((t_len, A_DV), lambda n, h: (rb + n, vb + h)),
        pl.BlockSpec((t_len, A_DV), lambda n, h: (rb + n, rbk + h)),
        pl.BlockSpec((t_len, 128), lambda n, h: (rb + n, PD_GG // 128)),
        pl.BlockSpec((None, 128, 2 * A_DK), lambda n, h: (h, 0, 0)),
        pl.BlockSpec((None, 1, 2 * A_DK), lambda n, h: (h, 0, 0)),
        pl.BlockSpec((1, A_DV), lambda n, h: (0, 0)),
        pl.BlockSpec((GLA_CHUNK, GLA_CHUNK), lambda n, h: (0, 0)),
        pl.BlockSpec((GLA_CHUNK, GLA_CHUNK), lambda n, h: (0, 0)),
    ]
    args = [pa, pa, pa, pa, pd, wg, bg, gla_g, levf, levb]
    if s0 is not None:
        in_specs.append(pl.BlockSpec((None, None, 2, None, A_DK, A_DV), lambda n, h: (n, s0_layer, 0, h, 0, 0)))
        args.append(s0)
    aliases = _alias_prev(in_specs, args, prev)
    out_shape = [jax.ShapeDtypeStruct((NTOK, A_HEADS * A_DV), BF16)]
    out_specs = [pl.BlockSpec((t_len, A_DV), lambda n, h: (rb + n, h))]
    if emit_state:
        out_shape.append(jax.ShapeDtypeStruct((n_seq, 2, A_HEADS, A_DK, A_DV), F32))
        out_specs.append(pl.BlockSpec((None, 2, None, A_DK, A_DV), lambda n, h: (n, 0, h, 0, 0)))
    res = pl.pallas_call(
        functools.partial(_gla_body, t_len=t_len, has_s0=s0 is not None, emit_state=emit_state,
                          has_prev=prev is not None),
        grid=(n_seq, A_HEADS),
        in_specs=in_specs,
        out_specs=out_specs,
        out_shape=out_shape,
        input_output_aliases=aliases,
        scratch_shapes=[pltpu.VMEM((t_len, A_DV), F32), pltpu.VMEM((A_DK, A_DV), F32),
                        pltpu.VMEM((A_DK, A_DV), F32), pltpu.VMEM((t_len, 2 * A_DK), F32)],
        compiler_params=_params(("arbitrary", "arbitrary"), 40),
        name="gla_ctx" if s0 is None else "gla_lat",
    )(*args)
    return res


def _rope(x, cos, sin):
    lane = lax.broadcasted_iota(jnp.int32, x.shape, 1)
    partner = jnp.where((lane & 32) == 0, pltpu.roll(x, 96, 1), pltpu.roll(x, 32, 1))
    return x * cos + partner * sin


def _rope_tables():
    n_rows = DEC_SEQ // GRID_W
    rows = jnp.repeat(jnp.arange(n_rows, dtype=F32), GRID_W)
    cols = jnp.tile(jnp.arange(GRID_W, dtype=F32), n_rows)
    quarter = C_ROPE // 4
    freqs = ROPE_THETA ** (-jnp.arange(quarter, dtype=F32) / quarter)
    ang = jnp.concatenate([rows[:, None] * freqs, cols[:, None] * freqs], axis=-1)
    cos, sin = jnp.cos(ang), jnp.sin(ang)
    cos_t = jnp.tile(cos, (1, 4))
    sin_t = jnp.tile(jnp.concatenate([-sin, sin], axis=-1), (1, 2))
    return cos_t, sin_t


def _softmax_parts(scores, scale):
    m = functools.reduce(jnp.maximum, [jnp.max(s, axis=-1, keepdims=True) for s in scores])
    ps = [jnp.exp2((s - m) * (scale * LOG2E)) for s in scores]
    inv = 1.0 / functools.reduce(jnp.add, [jnp.sum(p, axis=-1, keepdims=True) for p in ps])
    return ps, inv


def _pv(ps, values):
    return functools.reduce(jnp.add, [_dot(p.astype(BF16), v) for p, v in zip(ps, values)])


def _diff_body(*refs, latent, lam_init, has_prev, heads):
    it = iter(refs)
    lam_ref, q_ref, k_ref, v_ref, g_ref = [next(it) for _ in range(5)]
    if latent:
        kc_ref, vc_ref, cq_ref, sq_ref, ck_ref, sk_ref = [next(it) for _ in range(6)]
    if has_prev:
        next(it)
    o_ref = next(it)
    k_s, v_s = next(it), next(it)
    lam = lam_ref[0, 0]

    def head(ref, hh):
        return ref[:, hh * 128:(hh + 1) * 128]

    @pl.when(pl.program_id(2) == 0)
    def _():
        for hh in range(heads):
            k = head(k_ref, hh)
            if latent:
                k = _rope(k, ck_ref[...], sk_ref[...])
            k_s[:, hh * 128:(hh + 1) * 128] = k.astype(BF16)
        v_s[...] = v_ref[...].astype(BF16)

    lane = lax.broadcasted_iota(jnp.int32, (q_ref.shape[0], 128), 1)
    scale = B_DH ** -0.5
    scores = []
    for hh in range(heads):
        q = head(q_ref, hh)
        keys = [head(k_s, hh)]
        if latent:
            q = _rope(q, cq_ref[...], sq_ref[...])
            keys.insert(0, head(kc_ref, hh).astype(BF16))
        for half in range(2):
            qh = jnp.where((lane >> 6) == half, q, 0.0).astype(BF16)
            scores.append([_dot_nt(qh, kk) for kk in keys])
    parts = [_softmax_parts(s, scale) for s in scores]
    for hh in range(heads):
        values = [head(v_s, hh)]
        if latent:
            values.insert(0, head(vc_ref, hh).astype(BF16))
        (p1, inv1), (p2, inv2) = parts[2 * hh], parts[2 * hh + 1]
        o = _pv(p1, values) * inv1 - _pv(p2, values) * (lam * inv2)
        o = o * lax.rsqrt(jnp.mean(o * o, axis=-1, keepdims=True) + RMS_EPS) * g_ref[...] * (1.0 - lam_init)
        o_ref[:, hh * B_DV:(hh + 1) * B_DV] = o.astype(o_ref.dtype)


def _diff_attn(pd, lam, diff_g, lam_init, row0, n_seq, t_len, cache=None, rope=None, bq=256, prev=None, heads=2):
    latent = cache is not None
    rb = row0 // t_len
    qb0 = row0 // bq
    nqb = t_len // bq
    gw = heads * 128
    in_specs = [
        pl.BlockSpec(memory_space=pltpu.SMEM),
        pl.BlockSpec((bq, gw), lambda n, h, b: (qb0 + n * nqb + b, PD_Q // gw + h)),
        pl.BlockSpec((t_len, gw), lambda n, h, b: (rb + n, PD_K // gw + h)),
        pl.BlockSpec((t_len, gw), lambda n, h, b: (rb + n, PD_V // gw + h)),
        pl.BlockSpec((1, B_DV), lambda n, h, b: (0, 0)),
    ]
    args = [lam, pd, pd, pd, diff_g]
    if latent:
        kc, vc, layer = cache
        cos_t, sin_t = rope
        in_specs += [
            pl.BlockSpec((None, None, PAST_LEN, gw), lambda n, h, b: (n, layer, 0, h)),
            pl.BlockSpec((None, None, PAST_LEN, gw), lambda n, h, b: (n, layer, 0, h)),
            pl.BlockSpec((bq, 128), lambda n, h, b: (b, 0)),
            pl.BlockSpec((bq, 128), lambda n, h, b: (b, 0)),
            pl.BlockSpec((t_len, 128), lambda n, h, b: (0, 0)),
            pl.BlockSpec((t_len, 128), lambda n, h, b: (0, 0)),
        ]
        args += [kc, vc, cos_t, sin_t, cos_t, sin_t]
    aliases = _alias_prev(in_specs, args, prev)
    return pl.pallas_call(
        functools.partial(_diff_body, latent=latent, lam_init=lam_init, has_prev=prev is not None, heads=heads),
        grid=(n_seq, B_HEADS // heads, nqb),
        in_specs=in_specs,
        out_specs=pl.BlockSpec((bq, gw), lambda n, h, b: (qb0 + n * nqb + b, h)),
        out_shape=jax.ShapeDtypeStruct((NTOK, B_HEADS * B_DV), BF16),
        input_output_aliases=aliases,
        scratch_shapes=[pltpu.VMEM((t_len, gw), BF16), pltpu.VMEM((t_len, gw), BF16)],
        compiler_params=_params(("arbitrary", "arbitrary", "arbitrary"), 40),
        name="diff_lat" if latent else "diff_ctx",
    )(*args)


def _mla_body(*refs, latent, has_prev, heads):
    it = iter(refs)
    qn_ref, qr_ref, kv_ref, kr_ref = [next(it) for _ in range(4)]
    if latent:
        kvc_ref, krc_ref, cq_ref, sq_ref, ck_ref, sk_ref = [next(it) for _ in range(6)]
    if has_prev:
        next(it)
    o_ref = next(it)
    kr_s = next(it)

    @pl.when((pl.program_id(1) == 0) & (pl.program_id(2) == 0))
    def _():
        kr = kr_ref[...]
        if latent:
            kr = _rope(kr, ck_ref[...], sk_ref[...])
        kr_s[...] = kr.astype(BF16)

    scale = (C_NOPE + C_ROPE) ** -0.5
    lane = lax.broadcasted_iota(jnp.int32, (qr_ref.shape[0], 128), 1)
    kr_new = kr_s[...]
    kr_old = krc_ref[...].astype(BF16) if latent else None
    scores = []
    for hh in range(heads):
        if hh % 2 == 0:
            qr = qr_ref[:, (hh // 2) * 128:(hh // 2 + 1) * 128]
            if latent:
                qr = _rope(qr, cq_ref[...], sq_ref[...])
        qcat = jnp.concatenate([qn_ref[:, hh * C_NOPE:(hh + 1) * C_NOPE],
                                jnp.where((lane >> 6) == hh % 2, qr, 0.0).astype(BF16)], axis=1)
        ks = [jnp.concatenate([kv_ref[:, 2 * hh * C_NOPE:(2 * hh + 1) * C_NOPE], kr_new], axis=1)]
        if latent:
            ks.insert(0, jnp.concatenate([kvc_ref[:, 2 * hh * C_NOPE:(2 * hh + 1) * C_NOPE], kr_old], axis=1))
        scores.append([_dot_nt(qcat, kk) for kk in ks])
    parts = [_softmax_parts(s, scale) for s in scores]
    for hh in range(heads):
        vs = [kv_ref[:, (2 * hh + 1) * C_NOPE:(2 * hh + 2) * C_NOPE]]
        if latent:
            vs.insert(0, kvc_ref[:, (2 * hh + 1) * C_NOPE:(2 * hh + 2) * C_NOPE])
        ps, inv = parts[hh]
        o_ref[:, hh * C_DV:(hh + 1) * C_DV] = (_pv(ps, vs) * inv).astype(o_ref.dtype)


def _mla_attn(qn, qr, kv, kr2, row0, n_seq, t_len, cache=None, rope=None, bq=256, prev=None, heads=2):
    latent = cache is not None
    rb = row0 // t_len
    qb0 = row0 // bq
    nqb = t_len // bq
    group_kv = heads * (C_NOPE + C_DV)
    in_specs = [
        pl.BlockSpec((bq, heads * C_NOPE), lambda n, h, b: (qb0 + n * nqb + b, h)),
        pl.BlockSpec((bq, heads * C_ROPE), lambda n, h, b: (qb0 + n * nqb + b, h)),
        pl.BlockSpec((t_len, group_kv), lambda n, h, b: (rb + n, h)),
        pl.BlockSpec((t_len, 128), lambda n, h, b: (rb + n, 0)),
    ]
    args = [qn, qr, kv, kr2]
    if latent:
        kvc, krc = cache
        cos_t, sin_t = rope
        in_specs += [
            pl.BlockSpec((PAST_LEN, group_kv), lambda n, h, b: (n, h)),
            pl.BlockSpec((PAST_LEN, 128), lambda n, h, b: (n, 0)),
            pl.BlockSpec((bq, 128), lambda n, h, b: (b, 0)),
            pl.BlockSpec((bq, 128), lambda n, h, b: (b, 0)),
            pl.BlockSpec((t_len, 128), lambda n, h, b: (0, 0)),
            pl.BlockSpec((t_len, 128), lambda n, h, b: (0, 0)),
        ]
        args += [kvc, krc, cos_t, sin_t, cos_t, sin_t]
    aliases = _alias_prev(in_specs, args, prev)
    return pl.pallas_call(
        functools.partial(_mla_body, latent=latent, has_prev=prev is not None, heads=heads),
        grid=(n_seq, C_HEADS // heads, nqb),
        in_specs=in_specs,
        out_specs=pl.BlockSpec((bq, heads * C_DV), lambda n, h, b: (qb0 + n * nqb + b, h)),
        out_shape=jax.ShapeDtypeStruct((NTOK, C_HEADS * C_DV), BF16),
        input_output_aliases=aliases,
        scratch_shapes=[pltpu.VMEM((t_len, 128), BF16)],
        compiler_params=_params(("arbitrary", "arbitrary", "arbitrary"), 40),
        name="mla_lat" if latent else "mla_ctx",
    )(*args)


RT_E0, RT_E1, RT_R0, RT_R1, RT_W0, RT_W1 = range(6)
ROUTER_TM = 512


def _router_body(x_ref, sc_ref, sh_ref, w_ref, b_ref, tri_ref, rt_ref, cnt_ref, carry_ref):
    h = x_ref[...] * (1.0 + sc_ref[0, 0]) + sh_ref[0, 0]
    w = w_ref[...]
    h_hi = h.astype(BF16)
    w_hi = w.astype(BF16)
    h_lo = (h - h_hi.astype(F32)).astype(BF16)
    w_lo = (w - w_hi.astype(F32)).astype(BF16)
    lg = _dot(h_hi, w_hi) + (_dot(h_lo, w_hi) + _dot(h_hi, w_lo)) + b_ref[...]

    @pl.when(pl.program_id(0) == 0)
    def _():
        carry_ref[...] = jnp.zeros(carry_ref.shape, F32)

    lane = lax.broadcasted_iota(jnp.int32, lg.shape, 1)
    ninf = -jnp.inf
    is_g = lane < N_GROUPS
    gl = jnp.where(is_g, lg, ninf)
    gmax = jnp.max(gl, axis=1, keepdims=True)
    g_sel = jnp.min(jnp.where(gl == gmax, lane, 128), axis=1, keepdims=True)
    g_w = 1.0 / jnp.sum(jnp.where(is_g, jnp.exp(lg - gmax), 0.0), axis=1, keepdims=True)
    el = jnp.where(((lane - N_GROUPS) >> 2) == g_sel, lg, ninf)
    m1 = jnp.max(el, axis=1, keepdims=True)
    i1 = jnp.min(jnp.where(el == m1, lane, 128), axis=1, keepdims=True)
    el2 = jnp.where(lane == i1, ninf, el)
    m2 = jnp.max(el2, axis=1, keepdims=True)
    i2 = jnp.min(jnp.where(el2 == m2, lane, 128), axis=1, keepdims=True)
    p2 = jnp.exp(m2 - m1)
    w0 = g_w / (1.0 + p2)
    w1 = w0 * p2
    e0 = i1 - N_GROUPS
    e1 = i2 - N_GROUPS
    hit0 = lane == e0
    hit1 = lane == e1
    onehot = jnp.where(hit0, 1.0, jnp.where(hit1, 1.0, 0.0))
    before = _dot(tri_ref[...], onehot.astype(BF16)) + carry_ref[...]
    r0 = jnp.sum(jnp.where(hit0, before, 0.0), axis=1, keepdims=True)
    r1 = jnp.sum(jnp.where(hit1, before, 0.0), axis=1, keepdims=True)
    carry = carry_ref[...] + jnp.sum(onehot, axis=0, keepdims=True)
    carry_ref[...] = carry
    cnt_ref[...] = jnp.broadcast_to(carry, cnt_ref.shape)
    rec = [e0.astype(F32), e1.astype(F32), r0, r1, w0, w1]
    out = jnp.zeros(lg.shape, F32)
    for idx, val in enumerate(rec):
        out = jnp.where(lane == idx, val, out)
    rt_ref[...] = out


def _router(x, mod4, layer, w_r, b_r, tri):
    tm = ROUTER_TM
    return pl.pallas_call(
        _router_body,
        grid=(NTOK // tm,),
        in_specs=[pl.BlockSpec((tm, D_MODEL), lambda i: (i, 0)),
                  _mod_spec(layer, 4, tm), _mod_spec(layer, 3, tm),
                  pl.BlockSpec((D_MODEL, 128), lambda i: (0, 0)),
                  pl.BlockSpec((1, 128), lambda i: (0, 0)),
                  pl.BlockSpec((tm, tm), lambda i: (0, 0))],
        out_specs=[pl.BlockSpec((tm, 128), lambda i: (i, 0)),
                   pl.BlockSpec((8, 128), lambda i: (0, 0))],
        out_shape=[jax.ShapeDtypeStruct((NTOK, 128), F32), jax.ShapeDtypeStruct((8, 128), F32)],
        scratch_shapes=[pltpu.VMEM((1, 128), F32)],
        compiler_params=_params(("arbitrary",), 40),
        name="moe_router",
    )(x, mod4, mod4, w_r, b_r, tri)


DISPATCH_TM = 512


def _dispatch_body(dest_ref, pend_ref, padded_ref, x_ref, sc_ref, sh_ref, xs_hbm, hbuf, zbuf, sem, zsem):
    tm = DISPATCH_TM
    i = pl.program_id(0)

    @pl.when(i == 0)
    def _():
        zbuf[...] = jnp.zeros(zbuf.shape, F32)

        def tail(e):
            start = pl.multiple_of((pend_ref[e] - MOE_TM) // SUB, MOE_TM // SUB)
            return pltpu.make_async_copy(zbuf, xs_hbm.at[pl.ds(start, MOE_TM // SUB)], zsem)

        for e in range(N_EXPERTS):
            @pl.when(padded_ref[e] > 0)
            def _():
                tail(e).start()
        for e in range(N_EXPERTS):
            @pl.when(padded_ref[e] > 0)
            def _():
                tail(e).wait()

    h = x_ref[...] * (1.0 + sc_ref[0, 0]) + sh_ref[0, 0]
    hbuf[...] = h.reshape(tm // SUB, SUB, D_MODEL)

    def send(g, carry):
        for s in range(SUB):
            for k in range(EXPERT_TOP_K):
                d = dest_ref[k * NTOK + i * tm + g * SUB + s]
                pltpu.make_async_copy(hbuf.at[g, pl.ds(s, 1), :], _row(xs_hbm, d), sem).start()
        return carry
    lax.fori_loop(0, tm // SUB, send, 0)

    def drain(t, carry):
        for k in range(EXPERT_TOP_K):
            pltpu.make_async_copy(hbuf.at[0, pl.ds(0, 1), :], _row(xs_hbm, 0), sem).wait()
        return carry
    lax.fori_loop(0, tm, drain, 0, unroll=8)


def _dispatch(x, dest, pend, padded, mod4, layer):
    tm = DISPATCH_TM
    grid_spec = pltpu.PrefetchScalarGridSpec(
        num_scalar_prefetch=3,
        grid=(NTOK // tm,),
        in_specs=[pl.BlockSpec((tm, D_MODEL), lambda i, *_: (i, 0)),
                  _mod_spec(layer, 4, tm), _mod_spec(layer, 3, tm)],
        out_specs=pl.BlockSpec(memory_space=pl.ANY),
        scratch_shapes=[pltpu.VMEM((tm // SUB, SUB, D_MODEL), F32), pltpu.VMEM((MOE_TM // SUB, SUB, D_MODEL), F32),
                        pltpu.SemaphoreType.DMA(()), pltpu.SemaphoreType.DMA(())],
    )
    xs = pl.pallas_call(
        _dispatch_body,
        grid_spec=grid_spec,
        out_shape=jax.ShapeDtypeStruct((MOE_ROWS // SUB, SUB, D_MODEL), F32),
        compiler_params=_params(("arbitrary",), 40),
        name="moe_dispatch",
    )(dest, pend, padded, x, mod4, mod4)
    return xs.reshape(MOE_ROWS, D_MODEL)


def _expert_body(te_ref, tv_ref, x_ref, wg_ref, wu_ref, wd_ref, o_ref):
    t = pl.program_id(0)

    @pl.when(tv_ref[t] > 0)
    def _():
        x = x_ref[...].astype(BF16)
        g = _dot(x, wg_ref[...].astype(BF16))
        u = _dot(x, wu_ref[...].astype(BF16))
        act = (g / (1.0 + jnp.exp(-g))) * u
        o_ref[...] = _dot(act.astype(BF16), wd_ref[...].astype(BF16))

    @pl.when(tv_ref[t] == 0)
    def _():
        o_ref[...] = jnp.zeros(o_ref.shape, o_ref.dtype)


def _experts(xs, tile_e, tile_v, w_gate, w_up, w_down, layer):
    grid_spec = pltpu.PrefetchScalarGridSpec(
        num_scalar_prefetch=2,
        grid=(MOE_TILES,),
        in_specs=[
            pl.BlockSpec((MOE_TM, D_MODEL), lambda t, te, tv: (jnp.where(tv[t] > 0, t, 0), 0)),
            pl.BlockSpec((None, None, D_MODEL, D_EXPERT), lambda t, te, tv: (layer, te[t], 0, 0)),
            pl.BlockSpec((None, None, D_MODEL, D_EXPERT), lambda t, te, tv: (layer, te[t], 0, 0)),
            pl.BlockSpec((None, None, D_EXPERT, D_MODEL), lambda t, te, tv: (layer, te[t], 0, 0)),
        ],
        out_specs=pl.BlockSpec((MOE_TM, D_MODEL), lambda t, te, tv: (t, 0)),
    )
    return pl.pallas_call(
        _expert_body,
        grid_spec=grid_spec,
        out_shape=jax.ShapeDtypeStruct((MOE_ROWS, D_MODEL), F32),
        compiler_params=_params(("arbitrary",), 56),
        name="moe_experts",
    )(tile_e, tile_v, xs, w_gate, w_up, w_down)


def _layout(route, counts):
    counts = counts[0, :N_EXPERTS].astype(jnp.int32)
    padded = ((counts + MOE_TM - 1) // MOE_TM) * MOE_TM
    pend = jnp.cumsum(padded)
    pstart = pend - padded
    experts = jnp.arange(N_EXPERTS, dtype=jnp.int32)[None, :]

    def dest_of(e_lane, r_lane):
        e = route[:, e_lane].astype(jnp.int32)
        base = jnp.sum(jnp.where(e[:, None] == experts, pstart[None, :], 0), axis=1)
        return base + route[:, r_lane].astype(jnp.int32)

    dest = jnp.concatenate([dest_of(RT_E0, RT_R0), dest_of(RT_E1, RT_R1)])
    tile_start = jnp.arange(MOE_TILES, dtype=jnp.int32) * MOE_TM
    tile_e = jnp.minimum(jnp.sum((tile_start[:, None] >= pend[None, :]).astype(jnp.int32), axis=1), N_EXPERTS - 1)
    tile_v = (tile_start < pend[-1]).astype(jnp.int32)
    return dest, pend.astype(jnp.int32), padded, tile_e.astype(jnp.int32), tile_v


def _moe(x, mod4, layer, w_r, b_r, tri, w_gate, w_up, w_down, ln_g, ln_b):
    route, counts = _router(x, mod4, layer, w_r, b_r, tri)
    dest, pend, padded, tile_e, tile_v = _layout(route, counts)
    xs = _dispatch(x, dest, pend, padded, mod4, layer)
    rows = _experts(xs, tile_e, tile_v, w_gate, w_up, w_down, layer)
    return _combine_ln(x, rows, dest, route, mod4, layer, 5, ln_g, ln_b)


def kernel(x_prompt, x_sample, state_gla, cache_diff_k, cache_diff_v, cache_mla_ckv, cache_mla_krope, c, c_ctx, ada_w, ada_b, ln_g, ln_b, ab_w_in, gla_w_gate2, gla_b_gate2, gla_norm_g, diff_lambda, diff_norm_g, ab_w_out, mla_w_in, mla_q_norm_g, mla_w_uq, mla_kv_norm_g, mla_w_ukv, mla_w_out, moe_w_rg, moe_b_rg, moe_w_re, moe_b_re, moe_w_gate, moe_w_up, moe_w_down):
    x = (x_prompt.reshape(NCTX, D_MODEL), x_sample.reshape(NLAT, D_MODEL))
    cond = jnp.concatenate([c_ctx[None, :], c, jnp.zeros((N_COND - 1 - DEC_BATCH, D_MODEL), F32)], axis=0)
    mod4 = _ada_mod(cond, ada_w, ada_b).reshape(DEPTH, N_COND, 1, 6 * D_MODEL)
    cos_t, sin_t = _rope_tables()
    levf = jnp.asarray(_gla_level_table(False))
    levb = jnp.asarray(_gla_level_table(True))
    tri = jnp.asarray(np.tril(np.ones((ROUTER_TM, ROUTER_TM), np.float32), -1), BF16)
    cache_k = cache_diff_k.reshape(DEC_BATCH, -1, PAST_LEN, B_HEADS * 2 * B_DH)
    cache_v = cache_diff_v.reshape(DEC_BATCH, -1, PAST_LEN, B_HEADS * B_DV)

    new_gla, new_dk, new_dv, new_ckv, new_kr = [], [], [], [], []
    for layer in range(DEPTH):
        i = layer // 2
        if layer % 2 == 0:
            w_ab = jnp.concatenate([ab_w_in[i, :, :PA_N], ab_w_in[i, :, AB_D0:], ab_w_in[i, :, AB_GG0:AB_D0],
                                    jnp.zeros((D_MODEL, P_N - PD_GG - 2 * A_GATE_RANK), F32)],
                                   axis=1).astype(BF16)[None]
            pa = pd = _mod_matmul(x, mod4, layer, w_ab, 0, P_N, P_TN, "ab_in")
            wg = jnp.zeros((A_HEADS, 128, 2 * A_DK), F32)
            for j in range(2):
                blk = gla_w_gate2[i, j].reshape(A_GATE_RANK, A_HEADS, A_DK).transpose(1, 0, 2)
                wg = wg.at[:, j * A_GATE_RANK:(j + 1) * A_GATE_RANK, j * A_DK:(j + 1) * A_DK].set(blk)
            bg = gla_b_gate2[i].reshape(2, A_HEADS, A_DK).transpose(1, 0, 2).reshape(A_HEADS, 1, 2 * A_DK)
            g_gla = gla_norm_g[i][None, :]
            o_gla_c, st = _gla(pa, pd, wg, bg, g_gla, levf, levb, 0, BATCH, SEQ, None, 0, True)
            (o_gla,) = _gla(pa, pd, wg, bg, g_gla, levf, levb, NCTX, DEC_BATCH, DEC_SEQ, state_gla, i, False,
                            prev=o_gla_c)
            lv = diff_lambda[i]
            lam_init = 0.8 - 0.6 * math.exp(-0.3 * layer)
            lam = (jnp.exp(jnp.sum(lv[0] * lv[1])) - jnp.exp(jnp.sum(lv[2] * lv[3])) + lam_init).reshape(1, 1)
            g_diff = diff_norm_g[i][None, :]
            o_diff_c = _diff_attn(pd, lam, g_diff, lam_init, 0, BATCH, SEQ, heads=8)
            o_diff = _diff_attn(pd, lam, g_diff, lam_init, NCTX, DEC_BATCH, DEC_SEQ,
                                cache=(cache_k, cache_v, i), rope=(cos_t, sin_t), bq=DEC_SEQ, prev=o_diff_c)
            x = _proj_ln([o_gla, o_diff], ab_w_out[i].astype(BF16)[None], 0, x, mod4, layer, 2,
                         ln_g[layer, 0][None, :], ln_b[layer, 0][None, :], "ab_out_ln")
            new_gla.append(st)
            new_dk.append(pd[:NCTX, PD_K:PD_V].reshape(BATCH, SEQ, B_HEADS, 2, B_DH))
            new_dv.append(pd[:NCTX, PD_V:PD_GG].reshape(BATCH, SEQ, B_HEADS, B_DV))
        else:
            c_in = C_Q_RANK + C_KV_RANK + C_ROPE
            pm = _mod_matmul(x, mod4, layer, mla_w_in[i].astype(BF16)[None], 0, c_in, c_in, "mla_in")
            w_uq = mla_w_uq[i].reshape(C_Q_RANK, C_HEADS, C_NOPE + C_ROPE)
            w_uq = jnp.concatenate([w_uq[:, :, :C_NOPE].reshape(C_Q_RANK, -1),
                                    w_uq[:, :, C_NOPE:].reshape(C_Q_RANK, -1)], axis=1).astype(BF16)
            w_ukv = mla_w_ukv[i].astype(BF16)
            g_q = mla_q_norm_g[i][None, :]
            g_kv = mla_kv_norm_g[i][None, :]
            qn = _rms_matmul(pm, 0, C_Q_RANK, g_q, w_uq, 0, C_HEADS * C_NOPE, 512, 1024, BF16, "mla_q_nope")
            qr = _rms_matmul(pm, 0, C_Q_RANK, g_q, w_uq, C_HEADS * C_NOPE // 512, C_HEADS * C_ROPE, 512, 1024, F32,
                             "mla_q_rope")
            kv, ckv_n = _rms_matmul(pm, C_Q_RANK // C_KV_RANK, C_KV_RANK, g_kv, w_ukv, 0,
                                    C_HEADS * (C_NOPE + C_DV), 1024, 1024, BF16, "mla_kv", emit_norm=True)
            kv_c = _rms_matmul(cache_mla_ckv[:, i].reshape(DEC_BATCH * PAST_LEN, C_KV_RANK), 0, C_KV_RANK, g_kv,
                               w_ukv, 0, C_HEADS * (C_NOPE + C_DV), 1024, 1024, BF16, "mla_kv_cache",
                               norm=False)
            krope = pm[:, C_Q_RANK + C_KV_RANK:]
            kr2 = jnp.concatenate([krope, krope], axis=1)
            krc = cache_mla_krope[:, i].reshape(DEC_BATCH * PAST_LEN, C_ROPE)
            krc2 = jnp.concatenate([krc, krc], axis=1)
            o_c = _mla_attn(qn, qr, kv, kr2, 0, BATCH, SEQ, heads=8)
            o_mla = _mla_attn(qn, qr, kv, kr2, NCTX, DEC_BATCH, DEC_SEQ, cache=(kv_c, krc2), rope=(cos_t, sin_t),
                              bq=DEC_SEQ, prev=o_c, heads=2)
            x = _proj_ln([o_mla], mla_w_out[i].astype(BF16)[None], 0, x, mod4, layer, 2,
                         ln_g[layer, 0][None, :], ln_b[layer, 0][None, :], "mla_out_ln")
            new_ckv.append(ckv_n[:NCTX].reshape(BATCH, SEQ, C_KV_RANK))
            new_kr.append(krope[:NCTX].reshape(BATCH, SEQ, C_ROPE))
        w_r = jnp.concatenate([moe_w_rg[layer], moe_w_re[layer],
                               jnp.zeros((D_MODEL, 128 - N_GROUPS - N_EXPERTS), F32)], axis=1)
        b_r = jnp.concatenate([moe_b_rg[layer], moe_b_re[layer],
                               jnp.zeros((128 - N_GROUPS - N_EXPERTS,), F32)])[None, :]
        x = _moe(x, mod4, layer, w_r, b_r, tri, moe_w_gate, moe_w_up, moe_w_down,
                 ln_g[layer, 1][None, :], ln_b[layer, 1][None, :])

    return (x[:NCTX].reshape(BATCH, SEQ, D_MODEL),
            x[NCTX:].reshape(DEC_BATCH, DEC_SEQ, D_MODEL),
            jnp.stack(new_gla, axis=1), jnp.stack(new_dk, axis=1), jnp.stack(new_dv, axis=1),
            jnp.stack(new_ckv, axis=1), jnp.stack(new_kr, axis=1))
```

```python
import functools
import math

import numpy as np
import jax
import jax.numpy as jnp
from jax import lax
from jax.experimental import pallas as pl
from jax.experimental.pallas import tpu as pltpu

F32 = jnp.float32
BF16 = jnp.bfloat16

D_MODEL = 2048
BATCH = 16
SEQ = 256
DEPTH = 4
DEC_BATCH = 4
DEC_SEQ = 1024
PAST_LEN = 256
GRID_W = 64
A_HEADS = 4
A_DK = 128
A_DV = 256
A_GATE_RANK = 16
A_TAU = 16.0
B_HEADS = 8
B_DH = 64
B_DV = 2 * B_DH
C_HEADS = 16
C_Q_RANK = 512
C_KV_RANK = 256
C_NOPE = 128
C_ROPE = 64
C_DV = 128
N_GROUPS = 4
EXPERTS_PER_GROUP = 4
N_EXPERTS = N_GROUPS * EXPERTS_PER_GROUP
EXPERT_TOP_K = 2
D_EXPERT = 512
ROPE_THETA = 10000.0
LN_EPS = 1e-5
RMS_EPS = 1e-6
DEEPNORM_ALPHA = (2.0 * DEPTH) ** 0.25

NCTX = BATCH * SEQ
NLAT = DEC_BATCH * DEC_SEQ
NTOK = NCTX + NLAT
N_COND = 8
GLA_CHUNK = 256
GLA_LEVELS = 8
MOE_TM = 384
LOG2E = 1.4426950408889634
MOE_TILES = -(-(NTOK * EXPERT_TOP_K + N_EXPERTS * (MOE_TM - 1)) // MOE_TM)
MOE_ROWS = MOE_TILES * MOE_TM

PA_Q, PA_K, PA_V, PA_R = 0, A_HEADS * A_DK, 2 * A_HEADS * A_DK, 2 * A_HEADS * A_DK + A_HEADS * A_DV
PA_N = PA_R + A_HEADS * A_DV
AB_GG0 = PA_N
AB_D0 = PA_N + 2 * A_GATE_RANK
PD_Q = PA_N
PD_K = PD_Q + B_HEADS * 2 * B_DH
PD_V = PD_K + B_HEADS * 2 * B_DH
PD_GG = PD_V + B_HEADS * B_DV
P_TN = 1280
P_N = -(-(PD_GG + 2 * A_GATE_RANK) // P_TN) * P_TN


def _params(sem, vmem_mb):
    return pltpu.CompilerParams(dimension_semantics=sem, vmem_limit_bytes=vmem_mb * 2 ** 20)


def _cond_row(i, tm):
    return jnp.where(i * tm < NCTX, 0, 1 + (i * tm - NCTX) // DEC_SEQ)


def _dot(a, b):
    return jnp.dot(a, b, preferred_element_type=F32)


def _dot_nt(a, b):
    return lax.dot_general(a, b, (((1,), (1,)), ((), ())), preferred_element_type=F32)


def _dot_tn(a, b):
    return lax.dot_general(a, b, (((0,), (0,)), ((), ())), preferred_element_type=F32)


def _ada_body(c_ref, w_ref, b_ref, o_ref):
    c = c_ref[...]
    a = (c / (1.0 + jnp.exp(-c))).astype(BF16)
    o_ref[0] = _dot(a, w_ref[0].astype(BF16)) + b_ref[0]


def _ada_mod(cond, ada_w, ada_b):
    tn = 1024
    return pl.pallas_call(
        _ada_body,
        grid=(DEPTH, 6 * D_MODEL // tn),
        in_specs=[pl.BlockSpec((N_COND, D_MODEL), lambda l, j: (0, 0)),
                  pl.BlockSpec((1, D_MODEL, tn), lambda l, j: (l, 0, j)),
                  pl.BlockSpec((1, 1, tn), lambda l, j: (l, 0, j))],
        out_specs=pl.BlockSpec((1, N_COND, tn), lambda l, j: (l, 0, j)),
        out_shape=jax.ShapeDtypeStruct((DEPTH, N_COND, 6 * D_MODEL), F32),
        compiler_params=_params(("arbitrary", "arbitrary"), 40),
        name="ada_mod",
    )(cond, ada_w, ada_b.reshape(DEPTH, 1, 6 * D_MODEL))


def _mod_spec(layer, chunk, tm):
    return pl.BlockSpec((1, 1, 1, D_MODEL), lambda i, *_: (layer, _cond_row(i, tm), 0, chunk))


def _stream_specs(x, tm, single_buffer=False):
    if not isinstance(x, tuple):
        return [pl.BlockSpec((tm, D_MODEL), lambda i, *_: (i, 0))], [x]
    n_c = NCTX // tm
    mode = dict(pipeline_mode=pl.Buffered(1)) if single_buffer else {}
    return ([pl.BlockSpec((tm, D_MODEL), lambda i, *_: (jnp.minimum(i, n_c - 1), 0), **mode),
             pl.BlockSpec((tm, D_MODEL), lambda i, *_: (jnp.maximum(i - n_c, 0), 0), **mode)], list(x))


def _for_stream(x_refs, tm, fn):
    if len(x_refs) == 1:
        fn(x_refs[0])
        return
    n_c = NCTX // tm
    pl.when(pl.program_id(0) < n_c)(lambda: fn(x_refs[0]))
    pl.when(pl.program_id(0) >= n_c)(lambda: fn(x_refs[1]))


def _mod_mm_body(*refs, n_x, tm):
    x_refs = refs[:n_x]
    sc_ref, sh_ref, w_ref, o_ref, h_ref = refs[n_x:]

    @pl.when(pl.program_id(1) == 0)
    def _():
        def modulate(x_ref):
            h_ref[...] = (x_ref[...] * (1.0 + sc_ref[0, 0]) + sh_ref[0, 0]).astype(BF16)
        _for_stream(x_refs, tm, modulate)

    o_ref[...] = _dot(h_ref[...], w_ref[...]).astype(o_ref.dtype)


def _mod_matmul(x, mod4, layer, w, w_layer, n_out, tn, name, tm=1024, out_dtype=F32):
    x_specs, x_args = _stream_specs(x, tm, single_buffer=True)
    return pl.pallas_call(
        functools.partial(_mod_mm_body, n_x=len(x_args), tm=tm),
        grid=(NTOK // tm, n_out // tn),
        in_specs=x_specs + [_mod_spec(layer, 1, tm), _mod_spec(layer, 0, tm),
                            pl.BlockSpec((None, D_MODEL, tn), lambda i, j: (w_layer, 0, j))],
        out_specs=pl.BlockSpec((tm, tn), lambda i, j: (i, j)),
        out_shape=jax.ShapeDtypeStruct((NTOK, n_out), out_dtype),
        scratch_shapes=[pltpu.VMEM((tm, D_MODEL), BF16)],
        compiler_params=_params(("arbitrary", "arbitrary"), 56),
        name=name,
    )(*x_args, mod4, mod4, w)


def _rms_mm_body(x_ref, g_ref, w_ref, *rest, norm, emit_norm):
    if emit_norm:
        o_ref, n_ref, h_ref = rest
    else:
        o_ref, h_ref = rest

    @pl.when(pl.program_id(1) == 0)
    def _():
        x = x_ref[...]
        if norm:
            x = x * lax.rsqrt(jnp.mean(x * x, axis=-1, keepdims=True) + RMS_EPS) * g_ref[...]
        h_ref[...] = x.astype(BF16)
        if emit_norm:
            n_ref[...] = x

    o_ref[...] = _dot(h_ref[...], w_ref[...]).astype(o_ref.dtype)


def _rms_matmul(x, col_block, k, g, w, col0_blocks, n_out, tn, tm, out_dtype, name, norm=True, emit_norm=False):
    rows = x.shape[0]
    out_shape = [jax.ShapeDtypeStruct((rows, n_out), out_dtype)]
    out_specs = [pl.BlockSpec((tm, tn), lambda i, j: (i, j))]
    if emit_norm:
        out_shape.append(jax.ShapeDtypeStruct((rows, k), F32))
        out_specs.append(pl.BlockSpec((tm, k), lambda i, j: (i, 0)))
    res = pl.pallas_call(
        functools.partial(_rms_mm_body, norm=norm, emit_norm=emit_norm),
        grid=(rows // tm, n_out // tn),
        in_specs=[pl.BlockSpec((tm, k), lambda i, j: (i, col_block)),
                  pl.BlockSpec((1, k), lambda i, j: (0, 0)),
                  pl.BlockSpec((k, tn), lambda i, j: (0, col0_blocks + j))],
        out_specs=out_specs,
        out_shape=out_shape,
        scratch_shapes=[pltpu.VMEM((tm, k), BF16)],
        compiler_params=_params(("arbitrary", "arbitrary"), 40),
        name=name,
    )(x, g, w)
    return res if emit_norm else res[0]


def _layer_norm(z, g, b):
    mu = jnp.mean(z, axis=-1, keepdims=True)
    zc = z - mu
    var = jnp.mean(zc * zc, axis=-1, keepdims=True)
    return zc * lax.rsqrt(var + LN_EPS) * g + b


LN_ROWS = 256


def _proj_ln_body(*refs, n_lhs, n_x, n_j, tm, tn):
    lhs = refs[:n_lhs]
    ws = refs[n_lhs:2 * n_lhs]
    x_refs = refs[2 * n_lhs:2 * n_lhs + n_x]
    gate_ref, g_ref, b_ref, o_ref = refs[2 * n_lhs + n_x:]
    j = pl.program_id(1)
    y = _dot(lhs[0][...], ws[0][...])
    for a, w in zip(lhs[1:], ws[1:]):
        y = y + _dot(a[...], w[...])
    o_ref[:, pl.ds(pl.multiple_of(j * tn, tn), tn)] = y

    @pl.when(j == n_j - 1)
    def _():
        def normalise(x_ref):
            def body(r, carry):
                rows = pl.ds(pl.multiple_of(r * LN_ROWS, LN_ROWS), LN_ROWS)
                z = DEEPNORM_ALPHA * x_ref[rows, :] + gate_ref[0, 0] * o_ref[rows, :]
                o_ref[rows, :] = _layer_norm(z, g_ref[...], b_ref[...])
                return carry
            lax.fori_loop(0, tm // LN_ROWS, body, 0)
        _for_stream(x_refs, tm, normalise)


def _proj_ln(lhs_list, w, w_layer, x, mod4, layer, gate_chunk, ln_g, ln_b, name, tm=1024, tn=1024):
    n_lhs = len(lhs_list)
    kk = lhs_list[0].shape[1]
    n_j = D_MODEL // tn
    in_specs = [pl.BlockSpec((tm, kk), lambda i, j: (i, 0)) for _ in lhs_list]
    in_specs += [pl.BlockSpec((None, kk, tn), functools.partial(lambda i, j, r: (w_layer, r, j), r=r))
                 for r in range(n_lhs)]
    x_specs, x_args = _stream_specs(x, tm, single_buffer=True)
    in_specs += x_specs + [_mod_spec(layer, gate_chunk, tm),
                           pl.BlockSpec((1, D_MODEL), lambda i, j: (0, 0)),
                           pl.BlockSpec((1, D_MODEL), lambda i, j: (0, 0))]
    return pl.pallas_call(
        functools.partial(_proj_ln_body, n_lhs=n_lhs, n_x=len(x_args), n_j=n_j, tm=tm, tn=tn),
        grid=(NTOK // tm, n_j),
        in_specs=in_specs,
        out_specs=pl.BlockSpec((tm, D_MODEL), lambda i, j: (i, 0)),
        out_shape=jax.ShapeDtypeStruct((NTOK, D_MODEL), F32),
        compiler_params=_params(("arbitrary", "arbitrary"), 56),
        name=name,
    )(*lhs_list, *([w] * n_lhs), *x_args, mod4, ln_g, ln_b)


COMBINE_TM = 256


SUB = 8


def _row(ref, i):
    return ref.at[i >> 3, pl.ds(i & (SUB - 1), 1), :]


def _combine_ln_body(dest_ref, x_ref, rt_ref, gate_ref, g_ref, b_ref, rows_hbm, o_ref, buf, sem):
    tm = COMBINE_TM
    i = pl.program_id(0)
    n_i = pl.num_programs(0)

    def fetch(step, slot):
        def body(g, carry):
            for s in range(SUB):
                for k in range(EXPERT_TOP_K):
                    d = dest_ref[k * NTOK + step * tm + g * SUB + s]
                    pltpu.make_async_copy(_row(rows_hbm, d), buf.at[slot, k, g, pl.ds(s, 1), :],
                                          sem.at[slot]).start()
            return carry
        lax.fori_loop(0, tm // SUB, body, 0)

    @pl.when(i == 0)
    def _():
        fetch(0, 0)

    @pl.when(i + 1 < n_i)
    def _():
        fetch(i + 1, (i + 1) % 2)

    slot = i % 2

    def drain(t, carry):
        for k in range(EXPERT_TOP_K):
            pltpu.make_async_copy(_row(rows_hbm, 0), buf.at[slot, k, 0, pl.ds(0, 1), :], sem.at[slot]).wait()
        return carry
    lax.fori_loop(0, tm, drain, 0, unroll=8)

    rt = rt_ref[...]
    y0 = buf[slot, 0].reshape(tm, D_MODEL)
    y1 = buf[slot, 1].reshape(tm, D_MODEL)
    y = rt[:, RT_W0:RT_W0 + 1] * y0 + rt[:, RT_W1:RT_W1 + 1] * y1
    z = DEEPNORM_ALPHA * x_ref[...] + gate_ref[0, 0] * y
    o_ref[...] = _layer_norm(z, g_ref[...], b_ref[...])


def _combine_ln(x, rows, dest, route, mod4, layer, gate_chunk, ln_g, ln_b):
    tm = COMBINE_TM
    row = pl.BlockSpec((tm, D_MODEL), lambda i, d: (i, 0))
    vec = pl.BlockSpec((1, D_MODEL), lambda i, d: (0, 0))
    grid_spec = pltpu.PrefetchScalarGridSpec(
        num_scalar_prefetch=1,
        grid=(NTOK // tm,),
        in_specs=[row, pl.BlockSpec((tm, 128), lambda i, d: (i, 0)),
                  _mod_spec(layer, gate_chunk, tm), vec, vec,
                  pl.BlockSpec(memory_space=pl.ANY)],
        out_specs=row,
        scratch_shapes=[pltpu.VMEM((2, EXPERT_TOP_K, tm // SUB, SUB, D_MODEL), F32),
                        pltpu.SemaphoreType.DMA((2,))],
    )
    return pl.pallas_call(
        _combine_ln_body,
        grid_spec=grid_spec,
        out_shape=jax.ShapeDtypeStruct((NTOK, D_MODEL), F32),
        compiler_params=_params(("arbitrary",), 40),
        name="moe_combine_ln",
    )(dest, x, route, mod4, ln_g, ln_b, rows.reshape(MOE_ROWS // SUB, SUB, D_MODEL))


def _gla_level_table(reverse):
    t = np.arange(GLA_CHUNK)[:, None]
    s = np.arange(GLA_CHUNK)[None, :]
    x = t ^ s
    lev = np.where(x > 0, np.floor(np.log2(np.maximum(x, 1))).astype(np.int32), -1)
    live = (t < s) if reverse else (t > s)
    return np.where(live | (t == s), lev, -2).astype(np.int32)


def _anchor_rows(u, level, reverse):
    c = u.shape[0]
    n = c // 8
    u3 = u.reshape(n, 8, 128)
    half = 1 << level
    if level < 3:
        blk = 2 * half
        sub = lax.broadcasted_iota(jnp.int32, (n, 8, 128), 1)
        out = None
        for j in range(8 // blk):
            r = j * blk + (half if reverse else half - 1)
            piece = jnp.broadcast_to(u3[:, r:r + 1, :], (n, 8, 128))
            out = piece if out is None else jnp.where(sub >= j * blk, piece, out)
        return out.reshape(c, 128)
    m = 1 << (level - 2)
    r = 0 if reverse else 7
    e = jnp.broadcast_to(u3[:, r:r + 1, :], (n, 8, 128)).reshape(n // m, m, 8, 128)
    idx = m // 2 if reverse else m // 2 - 1
    return jnp.broadcast_to(e[:, idx:idx + 1], (n // m, m, 8, 128)).reshape(c, 128)


def _gla_intra(q, k, la, lev, reverse):
    rowk = lax.broadcasted_iota(jnp.int32, (GLA_CHUNK, 128), 0)
    scores = jnp.where(lev == -1, _dot_nt(q.astype(BF16), k.astype(BF16)), 0.0)
    u = la
    for level in range(GLA_LEVELS):
        anchor = _anchor_rows(u, level, reverse)
        qa = (q * jnp.exp2(u)).astype(BF16)
        ka = (k * jnp.exp2(jnp.minimum(anchor - u, 0.0))).astype(BF16)
        scores = jnp.where(lev == level, _dot_nt(qa, ka), scores)
        query_side = ((rowk >> level) & 1) == (0 if reverse else 1)
        u = u + jnp.where(query_side, anchor, 0.0)
    return scores, u


def _gla_body(*refs, t_len, has_s0, emit_state, has_prev):
    it = iter(refs)
    q_ref, k_ref, v_ref, r_ref, gg_ref, wg_ref, bg_ref, g_ref, levf_ref, levb_ref = [next(it) for _ in range(10)]
    s0_ref = next(it) if has_s0 else None
    if has_prev:
        next(it)
    o_ref = next(it)
    st_ref = next(it) if emit_state else None
    acc_ref, sf_ref, sb_ref, la_ref = [next(it) for _ in range(4)]
    c = GLA_CHUNK
    n_ch = t_len // c

    gl = _dot(gg_ref[...].astype(BF16), wg_ref[...].astype(BF16)) + bg_ref[...]
    la_ref[...] = (jnp.minimum(gl, 0.0) - jnp.log1p(jnp.exp(-jnp.abs(gl)))) * (LOG2E / A_TAU)
    if has_s0:
        sf_ref[...] = s0_ref[0]
        sb_ref[...] = s0_ref[1]

    eye = (lax.broadcasted_iota(jnp.int32, (A_DK, A_DK), 0) == lax.broadcasted_iota(jnp.int32, (A_DK, A_DK), 1))

    def chunk_step(ci, reverse, with_state):
        rows = pl.ds(ci * c if isinstance(ci, int) else pl.multiple_of(ci * c, c), c)
        q = q_ref[rows, :] * (A_DK ** -0.5)
        k = k_ref[rows, :]
        v = v_ref[rows, :].astype(BF16)
        la = la_ref[rows, A_DK:2 * A_DK] if reverse else la_ref[rows, 0:A_DK]
        s_ref = sb_ref if reverse else sf_ref
        scores, cum = _gla_intra(q, k, la, (levb_ref if reverse else levf_ref)[...], reverse)
        o = _dot(scores.astype(BF16), v)
        if with_state:
            o = o + _dot((q * jnp.exp2(cum)).astype(BF16), s_ref[...].astype(BF16))
        tot = jnp.broadcast_to(cum[0:1, :] if reverse else cum[c - 1:c, :], (c, A_DK))
        delta = _dot_tn((k * jnp.exp2(tot - cum)).astype(BF16), v)
        if with_state:
            decay = jnp.sum(jnp.where(eye, jnp.exp2(tot[0:A_DK, :]), 0.0), axis=1, keepdims=True)
            s_ref[...] = decay * s_ref[...] + delta
        else:
            s_ref[...] = delta
        if not reverse:
            acc_ref[rows, :] = o
        else:
            y = acc_ref[rows, :] + o
            y = y * lax.rsqrt(jnp.mean(y * y, axis=-1, keepdims=True) + RMS_EPS) * g_ref[...]
            r = r_ref[rows, :]
            o_ref[rows, :] = (y * (r / (1.0 + jnp.exp(-r)))).astype(o_ref.dtype)

    for reverse in (False, True):
        first = (n_ch - 1) if reverse else 0
        chunk_step(first, reverse, has_s0)
        if n_ch > 1:
            def body(i, carry, reverse=reverse):
                chunk_step((n_ch - 1 - i) if reverse else i, reverse, True)
                return carry
            lax.fori_loop(1, n_ch, body, 0)

    if emit_state:
        st_ref[0] = sf_ref[...]
        st_ref[1] = sb_ref[...]


def _alias_prev(in_specs, args, prev):
    if prev is None:
        return {}
    in_specs.append(pl.BlockSpec(memory_space=pl.ANY))
    args.append(prev)
    return {len(args) - 1: 0}


def _gla(pa, pd, wg, bg, gla_g, levf, levb, row0, n_seq, t_len, s0, s0_layer, emit_state, prev=None):
    rb = row0 // t_len
    kb, vb, rbk = PA_K // A_DK, PA_V // A_DV, PA_R // A_DV
    in_specs = [
        pl.BlockSpec((t_len, A_DK), lambda n, h: (rb + n, h)),
        pl.BlockSpec((t_len, A_DK), lambda n, h: (rb + n, kb + h)),
        pl.BlockSpec((t_len, A_DV), lambda n, h: (rb + n, vb + h)),
        pl.BlockSpec((t_len, A_DV), lambda n, h: (rb + n, rbk + h)),
        pl.BlockSpec((t_len, 128), lambda n, h: (rb + n, PD_GG // 128)),
        pl.BlockSpec((None, 128, 2 * A_DK), lambda n, h: (h, 0, 0)),
        pl.BlockSpec((None, 1, 2 * A_DK), lambda n, h: (h, 0, 0)),
        pl.BlockSpec((1, A_DV), lambda n, h: (0, 0)),
        pl.BlockSpec((GLA_CHUNK, GLA_CHUNK), lambda n, h: (0, 0)),
        pl.BlockSpec((GLA_CHUNK, GLA_CHUNK), lambda n, h: (0, 0)),
    ]
    args = [pa, pa, pa, pa, pd, wg, bg, gla_g, levf, levb]
    if s0 is not None:
        in_specs.append(pl.BlockSpec((None, None, 2, None, A_DK, A_DV), lambda n, h: (n, s0_layer, 0, h, 0, 0)))
        args.append(s0)
    aliases = _alias_prev(in_specs, args, prev)
    out_shape = [jax.ShapeDtypeStruct((NTOK, A_HEADS * A_DV), BF16)]
    out_specs = [pl.BlockSpec((t_len, A_DV), lambda n, h: (rb + n, h))]
    if emit_state:
        out_shape.append(jax.ShapeDtypeStruct((n_seq, 2, A_HEADS, A_DK, A_DV), F32))
        out_specs.append(pl.BlockSpec((None, 2, None, A_DK, A_DV), lambda n, h: (n, 0, h, 0, 0)))
    res = pl.pallas_call(
        functools.partial(_gla_body, t_len=t_len, has_s0=s0 is not None, emit_state=emit_state,
                          has_prev=prev is not None),
        grid=(n_seq, A_HEADS),
        in_specs=in_specs,
        out_specs=out_specs,
        out_shape=out_shape,
        input_output_aliases=aliases,
        scratch_shapes=[pltpu.VMEM((t_len, A_DV), F32), pltpu.VMEM((A_DK, A_DV), F32),
                        pltpu.VMEM((A_DK, A_DV), F32), pltpu.VMEM((t_len, 2 * A_DK), F32)],
        compiler_params=_params(("arbitrary", "arbitrary"), 40),
        name="gla_ctx" if s0 is None else "gla_lat",
    )(*args)
    return res


def _rope(x, cos, sin):
    lane = lax.broadcasted_iota(jnp.int32, x.shape, 1)
    partner = jnp.where((lane & 32) == 0, pltpu.roll(x, 96, 1), pltpu.roll(x, 32, 1))
    return x * cos + partner * sin


def _rope_tables():
    n_rows = DEC_SEQ // GRID_W
    rows = jnp.repeat(jnp.arange(n_rows, dtype=F32), GRID_W)
    cols = jnp.tile(jnp.arange(GRID_W, dtype=F32), n_rows)
    quarter = C_ROPE // 4
    freqs = ROPE_THETA ** (-jnp.arange(quarter, dtype=F32) / quarter)
    ang = jnp.concatenate([rows[:, None] * freqs, cols[:, None] * freqs], axis=-1)
    cos, sin = jnp.cos(ang), jnp.sin(ang)
    cos_t = jnp.tile(cos, (1, 4))
    sin_t = jnp.tile(jnp.concatenate([-sin, sin], axis=-1), (1, 2))
    return cos_t, sin_t


def _softmax_parts(scores, scale):
    m = functools.reduce(jnp.maximum, [jnp.max(s, axis=-1, keepdims=True) for s in scores])
    ps = [jnp.exp2((s - m) * (scale * LOG2E)) for s in scores]
    inv = 1.0 / functools.reduce(jnp.add, [jnp.sum(p, axis=-1, keepdims=True) for p in ps])
    return ps, inv


def _pv(ps, values):
    return functools.reduce(jnp.add, [_dot(p.astype(BF16), v) for p, v in zip(ps, values)])


def _diff_body(*refs, latent, lam_init, has_prev, heads):
    it = iter(refs)
    lam_ref, q_ref, k_ref, v_ref, g_ref = [next(it) for _ in range(5)]
    if latent:
        kc_ref, vc_ref, cq_ref, sq_ref, ck_ref, sk_ref = [next(it) for _ in range(6)]
    if has_prev:
        next(it)
    o_ref = next(it)
    k_s, v_s = next(it), next(it)
    lam = lam_ref[0, 0]

    def head(ref, hh):
        return ref[:, hh * 128:(hh + 1) * 128]

    @pl.when(pl.program_id(2) == 0)
    def _():
        for hh in range(heads):
            k = head(k_ref, hh)
            if latent:
                k = _rope(k, ck_ref[...], sk_ref[...])
            k_s[:, hh * 128:(hh + 1) * 128] = k.astype(BF16)
        v_s[...] = v_ref[...].astype(BF16)

    lane = lax.broadcasted_iota(jnp.int32, (q_ref.shape[0], 128), 1)
    scale = B_DH ** -0.5
    scores = []
    for hh in range(heads):
        q = head(q_ref, hh)
        keys = [head(k_s, hh)]
        if latent:
            q = _rope(q, cq_ref[...], sq_ref[...])
            keys.insert(0, head(kc_ref, hh).astype(BF16))
        for half in range(2):
            qh = jnp.where((lane >> 6) == half, q, 0.0).astype(BF16)
            scores.append([_dot_nt(qh, kk) for kk in keys])
    parts = [_softmax_parts(s, scale) for s in scores]
    for hh in range(heads):
        values = [head(v_s, hh)]
        if latent:
            values.insert(0, head(vc_ref, hh).astype(BF16))
        (p1, inv1), (p2, inv2) = parts[2 * hh], parts[2 * hh + 1]
        o = _pv(p1, values) * inv1 - _pv(p2, values) * (lam * inv2)
        o = o * lax.rsqrt(jnp.mean(o * o, axis=-1, keepdims=True) + RMS_EPS) * g_ref[...] * (1.0 - lam_init)
        o_ref[:, hh * B_DV:(hh + 1) * B_DV] = o.astype(o_ref.dtype)


def _diff_attn(pd, lam, diff_g, lam_init, row0, n_seq, t_len, cache=None, rope=None, bq=256, prev=None, heads=2):
    latent = cache is not None
    rb = row0 // t_len
    qb0 = row0 // bq
    nqb = t_len // bq
    gw = heads * 128
    in_specs = [
        pl.BlockSpec(memory_space=pltpu.SMEM),
        pl.BlockSpec((bq, gw), lambda n, h, b: (qb0 + n * nqb + b, PD_Q // gw + h)),
        pl.BlockSpec((t_len, gw), lambda n, h, b: (rb + n, PD_K // gw + h)),
        pl.BlockSpec((t_len, gw), lambda n, h, b: (rb + n, PD_V // gw + h)),
        pl.BlockSpec((1, B_DV), lambda n, h, b: (0, 0)),
    ]
    args = [lam, pd, pd, pd, diff_g]
    if latent:
        kc, vc, layer = cache
        cos_t, sin_t = rope
        in_specs += [
            pl.BlockSpec((None, None, PAST_LEN, gw), lambda n, h, b: (n, layer, 0, h)),
            pl.BlockSpec((None, None, PAST_LEN, gw), lambda n, h, b: (n, layer, 0, h)),
            pl.BlockSpec((bq, 128), lambda n, h, b: (b, 0)),
            pl.BlockSpec((bq, 128), lambda n, h, b: (b, 0)),
            pl.BlockSpec((t_len, 128), lambda n, h, b: (0, 0)),
            pl.BlockSpec((t_len, 128), lambda n, h, b: (0, 0)),
        ]
        args += [kc, vc, cos_t, sin_t, cos_t, sin_t]
    aliases = _alias_prev(in_specs, args, prev)
    return pl.pallas_call(
        functools.partial(_diff_body, latent=latent, lam_init=lam_init, has_prev=prev is not None, heads=heads),
        grid=(n_seq, B_HEADS // heads, nqb),
        in_specs=in_specs,
        out_specs=pl.BlockSpec((bq, gw), lambda n, h, b: (qb0 + n * nqb + b, h)),
        out_shape=jax.ShapeDtypeStruct((NTOK, B_HEADS * B_DV), BF16),
        input_output_aliases=aliases,
        scratch_shapes=[pltpu.VMEM((t_len, gw), BF16), pltpu.VMEM((t_len, gw), BF16)],
        compiler_params=_params(("arbitrary", "arbitrary", "arbitrary"), 40),
        name="diff_lat" if latent else "diff_ctx",
    )(*args)


def _mla_body(*refs, latent, has_prev, heads):
    it = iter(refs)
    qn_ref, qr_ref, kv_ref, kr_ref = [next(it) for _ in range(4)]
    if latent:
        kvc_ref, krc_ref, cq_ref, sq_ref, ck_ref, sk_ref = [next(it) for _ in range(6)]
    if has_prev:
        next(it)
    o_ref = next(it)
    kr_s = next(it)

    @pl.when((pl.program_id(1) == 0) & (pl.program_id(2) == 0))
    def _():
        kr = kr_ref[...]
        if latent:
            kr = _rope(kr, ck_ref[...], sk_ref[...])
        kr_s[...] = kr.astype(BF16)

    scale = (C_NOPE + C_ROPE) ** -0.5
    lane = lax.broadcasted_iota(jnp.int32, (qr_ref.shape[0], 128), 1)
    kr_new = kr_s[...]
    kr_old = krc_ref[...].astype(BF16) if latent else None
    scores = []
    for hh in range(heads):
        if hh % 2 == 0:
            qr = qr_ref[:, (hh // 2) * 128:(hh // 2 + 1) * 128]
            if latent:
                qr = _rope(qr, cq_ref[...], sq_ref[...])
        qcat = jnp.concatenate([qn_ref[:, hh * C_NOPE:(hh + 1) * C_NOPE],
                                jnp.where((lane >> 6) == hh % 2, qr, 0.0).astype(BF16)], axis=1)
        ks = [jnp.concatenate([kv_ref[:, 2 * hh * C_NOPE:(2 * hh + 1) * C_NOPE], kr_new], axis=1)]
        if latent:
            ks.insert(0, jnp.concatenate([kvc_ref[:, 2 * hh * C_NOPE:(2 * hh + 1) * C_NOPE], kr_old], axis=1))
        scores.append([_dot_nt(qcat, kk) for kk in ks])
    parts = [_softmax_parts(s, scale) for s in scores]
    for hh in range(heads):
        vs = [kv_ref[:, (2 * hh + 1) * C_NOPE:(2 * hh + 2) * C_NOPE]]
        if latent:
            vs.insert(0, kvc_ref[:, (2 * hh + 1) * C_NOPE:(2 * hh + 2) * C_NOPE])
        ps, inv = parts[hh]
        o_ref[:, hh * C_DV:(hh + 1) * C_DV] = (_pv(ps, vs) * inv).astype(o_ref.dtype)


def _mla_attn(qn, qr, kv, kr2, row0, n_seq, t_len, cache=None, rope=None, bq=256, prev=None, heads=2):
    latent = cache is not None
    rb = row0 // t_len
    qb0 = row0 // bq
    nqb = t_len // bq
    group_kv = heads * (C_NOPE + C_DV)
    in_specs = [
        pl.BlockSpec((bq, heads * C_NOPE), lambda n, h, b: (qb0 + n * nqb + b, h)),
        pl.BlockSpec((bq, heads * C_ROPE), lambda n, h, b: (qb0 + n * nqb + b, h)),
        pl.BlockSpec((t_len, group_kv), lambda n, h, b: (rb + n, h)),
        pl.BlockSpec((t_len, 128), lambda n, h, b: (rb + n, 0)),
    ]
    args = [qn, qr, kv, kr2]
    if latent:
        kvc, krc = cache
        cos_t, sin_t = rope
        in_specs += [
            pl.BlockSpec((PAST_LEN, group_kv), lambda n, h, b: (n, h)),
            pl.BlockSpec((PAST_LEN, 128), lambda n, h, b: (n, 0)),
            pl.BlockSpec((bq, 128), lambda n, h, b: (b, 0)),
            pl.BlockSpec((bq, 128), lambda n, h, b: (b, 0)),
            pl.BlockSpec((t_len, 128), lambda n, h, b: (0, 0)),
            pl.BlockSpec((t_len, 128), lambda n, h, b: (0, 0)),
        ]
        args += [kvc, krc, cos_t, sin_t, cos_t, sin_t]
    aliases = _alias_prev(in_specs, args, prev)
    return pl.pallas_call(
        functools.partial(_mla_body, latent=latent, has_prev=prev is not None, heads=heads),
        grid=(n_seq, C_HEADS // heads, nqb),
        in_specs=in_specs,
        out_specs=pl.BlockSpec((bq, heads * C_DV), lambda n, h, b: (qb0 + n * nqb + b, h)),
        out_shape=jax.ShapeDtypeStruct((NTOK, C_HEADS * C_DV), BF16),
        input_output_aliases=aliases,
        scratch_shapes=[pltpu.VMEM((t_len, 128), BF16)],
        compiler_params=_params(("arbitrary", "arbitrary", "arbitrary"), 40),
        name="mla_lat" if latent else "mla_ctx",
    )(*args)


RT_E0, RT_E1, RT_R0, RT_R1, RT_W0, RT_W1 = range(6)
ROUTER_TM = 512


def _router_body(x_ref, sc_ref, sh_ref, w_ref, b_ref, tri_ref, rt_ref, cnt_ref, carry_ref):
    h = x_ref[...] * (1.0 + sc_ref[0, 0]) + sh_ref[0, 0]
    w = w_ref[...]
    h_hi = h.astype(BF16)
    w_hi = w.astype(BF16)
    h_lo = (h - h_hi.astype(F32)).astype(BF16)
    w_lo = (w - w_hi.astype(F32)).astype(BF16)
    lg = _dot(h_hi, w_hi) + (_dot(h_lo, w_hi) + _dot(h_hi, w_lo)) + b_ref[...]

    @pl.when(pl.program_id(0) == 0)
    def _():
        carry_ref[...] = jnp.zeros(carry_ref.shape, F32)

    lane = lax.broadcasted_iota(jnp.int32, lg.shape, 1)
    ninf = -jnp.inf
    is_g = lane < N_GROUPS
    gl = jnp.where(is_g, lg, ninf)
    gmax = jnp.max(gl, axis=1, keepdims=True)
    g_sel = jnp.min(jnp.where(gl == gmax, lane, 128), axis=1, keepdims=True)
    g_w = 1.0 / jnp.sum(jnp.where(is_g, jnp.exp(lg - gmax), 0.0), axis=1, keepdims=True)
    el = jnp.where(((lane - N_GROUPS) >> 2) == g_sel, lg, ninf)
    m1 = jnp.max(el, axis=1, keepdims=True)
    i1 = jnp.min(jnp.where(el == m1, lane, 128), axis=1, keepdims=True)
    el2 = jnp.where(lane == i1, ninf, el)
    m2 = jnp.max(el2, axis=1, keepdims=True)
    i2 = jnp.min(jnp.where(el2 == m2, lane, 128), axis=1, keepdims=True)
    p2 = jnp.exp(m2 - m1)
    w0 = g_w / (1.0 + p2)
    w1 = w0 * p2
    e0 = i1 - N_GROUPS
    e1 = i2 - N_GROUPS
    hit0 = lane == e0
    hit1 = lane == e1
    onehot = jnp.where(hit0, 1.0, jnp.where(hit1, 1.0, 0.0))
    before = _dot(tri_ref[...], onehot.astype(BF16)) + carry_ref[...]
    r0 = jnp.sum(jnp.where(hit0, before, 0.0), axis=1, keepdims=True)
    r1 = jnp.sum(jnp.where(hit1, before, 0.0), axis=1, keepdims=True)
    carry = carry_ref[...] + jnp.sum(onehot, axis=0, keepdims=True)
    carry_ref[...] = carry
    cnt_ref[...] = jnp.broadcast_to(carry, cnt_ref.shape)
    rec = [e0.astype(F32), e1.astype(F32), r0, r1, w0, w1]
    out = jnp.zeros(lg.shape, F32)
    for idx, val in enumerate(rec):
        out = jnp.where(lane == idx, val, out)
    rt_ref[...] = out


def _router(x, mod4, layer, w_r, b_r, tri):
    tm = ROUTER_TM
    return pl.pallas_call(
        _router_body,
        grid=(NTOK // tm,),
        in_specs=[pl.BlockSpec((tm, D_MODEL), lambda i: (i, 0)),
                  _mod_spec(layer, 4, tm), _mod_spec(layer, 3, tm),
                  pl.BlockSpec((D_MODEL, 128), lambda i: (0, 0)),
                  pl.BlockSpec((1, 128), lambda i: (0, 0)),
                  pl.BlockSpec((tm, tm), lambda i: (0, 0))],
        out_specs=[pl.BlockSpec((tm, 128), lambda i: (i, 0)),
                   pl.BlockSpec((8, 128), lambda i: (0, 0))],
        out_shape=[jax.ShapeDtypeStruct((NTOK, 128), F32), jax.ShapeDtypeStruct((8, 128), F32)],
        scratch_shapes=[pltpu.VMEM((1, 128), F32)],
        compiler_params=_params(("arbitrary",), 40),
        name="moe_router",
    )(x, mod4, mod4, w_r, b_r, tri)


DISPATCH_TM = 512


def _dispatch_body(dest_ref, pend_ref, padded_ref, x_ref, sc_ref, sh_ref, xs_hbm, hbuf, zbuf, sem, zsem):
    tm = DISPATCH_TM
    i = pl.program_id(0)

    @pl.when(i == 0)
    def _():
        zbuf[...] = jnp.zeros(zbuf.shape, F32)

        def tail(e):
            start = pl.multiple_of((pend_ref[e] - MOE_TM) // SUB, MOE_TM // SUB)
            return pltpu.make_async_copy(zbuf, xs_hbm.at[pl.ds(start, MOE_TM // SUB)], zsem)

        for e in range(N_EXPERTS):
            @pl.when(padded_ref[e] > 0)
            def _():
                tail(e).start()
        for e in range(N_EXPERTS):
            @pl.when(padded_ref[e] > 0)
            def _():
                tail(e).wait()

    h = x_ref[...] * (1.0 + sc_ref[0, 0]) + sh_ref[0, 0]
    hbuf[...] = h.reshape(tm // SUB, SUB, D_MODEL)

    def send(g, carry):
        for s in range(SUB):
            for k in range(EXPERT_TOP_K):
                d = dest_ref[k * NTOK + i * tm + g * SUB + s]
                pltpu.make_async_copy(hbuf.at[g, pl.ds(s, 1), :], _row(xs_hbm, d), sem).start()
        return carry
    lax.fori_loop(0, tm // SUB, send, 0)

    def drain(t, carry):
        for k in range(EXPERT_TOP_K):
            pltpu.make_async_copy(hbuf.at[0, pl.ds(0, 1), :], _row(xs_hbm, 0), sem).wait()
        return carry
    lax.fori_loop(0, tm, drain, 0, unroll=8)


def _dispatch(x, dest, pend, padded, mod4, layer):
    tm = DISPATCH_TM
    grid_spec = pltpu.PrefetchScalarGridSpec(
        num_scalar_prefetch=3,
        grid=(NTOK // tm,),
        in_specs=[pl.BlockSpec((tm, D_MODEL), lambda i, *_: (i, 0)),
                  _mod_spec(layer, 4, tm), _mod_spec(layer, 3, tm)],
        out_specs=pl.BlockSpec(memory_space=pl.ANY),
        scratch_shapes=[pltpu.VMEM((tm // SUB, SUB, D_MODEL), F32), pltpu.VMEM((MOE_TM // SUB, SUB, D_MODEL), F32),
                        pltpu.SemaphoreType.DMA(()), pltpu.SemaphoreType.DMA(())],
    )
    xs = pl.pallas_call(
        _dispatch_body,
        grid_spec=grid_spec,
        out_shape=jax.ShapeDtypeStruct((MOE_ROWS // SUB, SUB, D_MODEL), F32),
        compiler_params=_params(("arbitrary",), 40),
        name="moe_dispatch",
    )(dest, pend, padded, x, mod4, mod4)
    return xs.reshape(MOE_ROWS, D_MODEL)


def _expert_body(te_ref, tv_ref, x_ref, wg_ref, wu_ref, wd_ref, o_ref):
    t = pl.program_id(0)

    @pl.when(tv_ref[t] > 0)
    def _():
        x = x_ref[...].astype(BF16)
        g = _dot(x, wg_ref[...].astype(BF16))
        u = _dot(x, wu_ref[...].astype(BF16))
        act = (g / (1.0 + jnp.exp(-g))) * u
        o_ref[...] = _dot(act.astype(BF16), wd_ref[...].astype(BF16))

    @pl.when(tv_ref[t] == 0)
    def _():
        o_ref[...] = jnp.zeros(o_ref.shape, o_ref.dtype)


def _experts(xs, tile_e, tile_v, w_gate, w_up, w_down, layer):
    grid_spec = pltpu.PrefetchScalarGridSpec(
        num_scalar_prefetch=2,
        grid=(MOE_TILES,),
        in_specs=[
            pl.BlockSpec((MOE_TM, D_MODEL), lambda t, te, tv: (jnp.where(tv[t] > 0, t, 0), 0)),
            pl.BlockSpec((None, None, D_MODEL, D_EXPERT), lambda t, te, tv: (layer, te[t], 0, 0)),
            pl.BlockSpec((None, None, D_MODEL, D_EXPERT), lambda t, te, tv: (layer, te[t], 0, 0)),
            pl.BlockSpec((None, None, D_EXPERT, D_MODEL), lambda t, te, tv: (layer, te[t], 0, 0)),
        ],
        out_specs=pl.BlockSpec((MOE_TM, D_MODEL), lambda t, te, tv: (t, 0)),
    )
    return pl.pallas_call(
        _expert_body,
        grid_spec=grid_spec,
        out_shape=jax.ShapeDtypeStruct((MOE_ROWS, D_MODEL), F32),
        compiler_params=_params(("arbitrary",), 56),
        name="moe_experts",
    )(tile_e, tile_v, xs, w_gate, w_up, w_down)


def _layout(route, counts):
    counts = counts[0, :N_EXPERTS].astype(jnp.int32)
    padded = ((counts + MOE_TM - 1) // MOE_TM) * MOE_TM
    pend = jnp.cumsum(padded)
    pstart = pend - padded
    experts = jnp.arange(N_EXPERTS, dtype=jnp.int32)[None, :]

    def dest_of(e_lane, r_lane):
        e = route[:, e_lane].astype(jnp.int32)
        base = jnp.sum(jnp.where(e[:, None] == experts, pstart[None, :], 0), axis=1)
        return base + route[:, r_lane].astype(jnp.int32)

    dest = jnp.concatenate([dest_of(RT_E0, RT_R0), dest_of(RT_E1, RT_R1)])
    tile_start = jnp.arange(MOE_TILES, dtype=jnp.int32) * MOE_TM
    tile_e = jnp.minimum(jnp.sum((tile_start[:, None] >= pend[None, :]).astype(jnp.int32), axis=1), N_EXPERTS - 1)
    tile_v = (tile_start < pend[-1]).astype(jnp.int32)
    return dest, pend.astype(jnp.int32), padded, tile_e.astype(jnp.int32), tile_v


def _moe(x, mod4, layer, w_r, b_r, tri, w_gate, w_up, w_down, ln_g, ln_b):
    route, counts = _router(x, mod4, layer, w_r, b_r, tri)
    dest, pend, padded, tile_e, tile_v = _layout(route, counts)
    xs = _dispatch(x, dest, pend, padded, mod4, layer)
    rows = _experts(xs, tile_e, tile_v, w_gate, w_up, w_down, layer)
    return _combine_ln(x, rows, dest, route, mod4, layer, 5, ln_g, ln_b)


def kernel(x_prompt, x_sample, state_gla, cache_diff_k, cache_diff_v, cache_mla_ckv, cache_mla_krope, c, c_ctx, ada_w, ada_b, ln_g, ln_b, ab_w_in, gla_w_gate2, gla_b_gate2, gla_norm_g, diff_lambda, diff_norm_g, ab_w_out, mla_w_in, mla_q_norm_g, mla_w_uq, mla_kv_norm_g, mla_w_ukv, mla_w_out, moe_w_rg, moe_b_rg, moe_w_re, moe_b_re, moe_w_gate, moe_w_up, moe_w_down):
    x = (x_prompt.reshape(NCTX, D_MODEL), x_sample.reshape(NLAT, D_MODEL))
    cond = jnp.concatenate([c_ctx[None, :], c, jnp.zeros((N_COND - 1 - DEC_BATCH, D_MODEL), F32)], axis=0)
    mod4 = _ada_mod(cond, ada_w, ada_b).reshape(DEPTH, N_COND, 1, 6 * D_MODEL)
    cos_t, sin_t = _rope_tables()
    levf = jnp.asarray(_gla_level_table(False))
    levb = jnp.asarray(_gla_level_table(True))
    tri = jnp.asarray(np.tril(np.ones((ROUTER_TM, ROUTER_TM), np.float32), -1), BF16)
    cache_k = cache_diff_k.reshape(DEC_BATCH, -1, PAST_LEN, B_HEADS * 2 * B_DH)
    cache_v = cache_diff_v.reshape(DEC_BATCH, -1, PAST_LEN, B_HEADS * B_DV)

    new_gla, new_dk, new_dv, new_ckv, new_kr = [], [], [], [], []
    for layer in range(DEPTH):
        i = layer // 2
        if layer % 2 == 0:
            w_ab = jnp.concatenate([ab_w_in[i, :, :PA_N], ab_w_in[i, :, AB_D0:], ab_w_in[i, :, AB_GG0:AB_D0],
                                    jnp.zeros((D_MODEL, P_N - PD_GG - 2 * A_GATE_RANK), F32)],
                                   axis=1).astype(BF16)[None]
            pa = pd = _mod_matmul(x, mod4, layer, w_ab, 0, P_N, P_TN, "ab_in")
            wg = jnp.zeros((A_HEADS, 128, 2 * A_DK), F32)
            for j in range(2):
                blk = gla_w_gate2[i, j].reshape(A_GATE_RANK, A_HEADS, A_DK).transpose(1, 0, 2)
                wg = wg.at[:, j * A_GATE_RANK:(j + 1) * A_GATE_RANK, j * A_DK:(j + 1) * A_DK].set(blk)
            bg = gla_b_gate2[i].reshape(2, A_HEADS, A_DK).transpose(1, 0, 2).reshape(A_HEADS, 1, 2 * A_DK)
            g_gla = gla_norm_g[i][None, :]
            o_gla_c, st = _gla(pa, pd, wg, bg, g_gla, levf, levb, 0, BATCH, SEQ, None, 0, True)
            (o_gla,) = _gla(pa, pd, wg, bg, g_gla, levf, levb, NCTX, DEC_BATCH, DEC_SEQ, state_gla, i, False,
                            prev=o_gla_c)
            lv = diff_lambda[i]
            lam_init = 0.8 - 0.6 * math.exp(-0.3 * layer)
            lam = (jnp.exp(jnp.sum(lv[0] * lv[1])) - jnp.exp(jnp.sum(lv[2] * lv[3])) + lam_init).reshape(1, 1)
            g_diff = diff_norm_g[i][None, :]
            o_diff_c = _diff_attn(pd, lam, g_diff, lam_init, 0, BATCH, SEQ, heads=8)
            o_diff = _diff_attn(pd, lam, g_diff, lam_init, NCTX, DEC_BATCH, DEC_SEQ,
                                cache=(cache_k, cache_v, i), rope=(cos_t, sin_t), bq=DEC_SEQ, prev=o_diff_c)
            x = _proj_ln([o_gla, o_diff], ab_w_out[i].astype(BF16)[None], 0, x, mod4, layer, 2,
                         ln_g[layer, 0][None, :], ln_b[layer, 0][None, :], "ab_out_ln")
            new_gla.append(st)
            new_dk.append(pd[:NCTX, PD_K:PD_V].reshape(BATCH, SEQ, B_HEADS, 2, B_DH))
            new_dv.append(pd[:NCTX, PD_V:PD_GG].reshape(BATCH, SEQ, B_HEADS, B_DV))
        else:
            c_in = C_Q_RANK + C_KV_RANK + C_ROPE
            pm = _mod_matmul(x, mod4, layer, mla_w_in[i].astype(BF16)[None], 0, c_in, c_in, "mla_in")
            w_uq = mla_w_uq[i].reshape(C_Q_RANK, C_HEADS, C_NOPE + C_ROPE)
            w_uq = jnp.concatenate([w_uq[:, :, :C_NOPE].reshape(C_Q_RANK, -1),
                                    w_uq[:, :, C_NOPE:].reshape(C_Q_RANK, -1)], axis=1).astype(BF16)
            w_ukv = mla_w_ukv[i].astype(BF16)
            g_q = mla_q_norm_g[i][None, :]
            g_kv = mla_kv_norm_g[i][None, :]
            qn = _rms_matmul(pm, 0, C_Q_RANK, g_q, w_uq, 0, C_HEADS * C_NOPE, 512, 1024, BF16, "mla_q_nope")
            qr = _rms_matmul(pm, 0, C_Q_RANK, g_q, w_uq, C_HEADS * C_NOPE // 512, C_HEADS * C_ROPE, 512, 1024, F32,
                             "mla_q_rope")
            kv, ckv_n = _rms_matmul(pm, C_Q_RANK // C_KV_RANK, C_KV_RANK, g_kv, w_ukv, 0,
                                    C_HEADS * (C_NOPE + C_DV), 1024, 1024, BF16, "mla_kv", emit_norm=True)
            kv_c = _rms_matmul(cache_mla_ckv[:, i].reshape(DEC_BATCH * PAST_LEN, C_KV_RANK), 0, C_KV_RANK, g_kv,
                               w_ukv, 0, C_HEADS * (C_NOPE + C_DV), 1024, 1024, BF16, "mla_kv_cache",
                               norm=False)
            krope = pm[:, C_Q_RANK + C_KV_RANK:]
            kr2 = jnp.concatenate([krope, krope], axis=1)
            krc = cache_mla_krope[:, i].reshape(DEC_BATCH * PAST_LEN, C_ROPE)
            krc2 = jnp.concatenate([krc, krc], axis=1)
            o_c = _mla_attn(qn, qr, kv, kr2, 0, BATCH, SEQ, heads=8)
            o_mla = _mla_attn(qn, qr, kv, kr2, NCTX, DEC_BATCH, DEC_SEQ, cache=(kv_c, krc2), rope=(cos_t, sin_t),
                              bq=DEC_SEQ, prev=o_c, heads=2)
            x = _proj_ln([o_mla], mla_w_out[i].astype(BF16)[None], 0, x, mod4, layer, 2,
                         ln_g[layer, 0][None, :], ln_b[layer, 0][None, :], "mla_out_ln")
            new_ckv.append(ckv_n[:NCTX].reshape(BATCH, SEQ, C_KV_RANK))
            new_kr.append(krope[:NCTX].reshape(BATCH, SEQ, C_ROPE))
        w_r = jnp.concatenate([moe_w_rg[layer], moe_w_re[layer],
                               jnp.zeros((D_MODEL, 128 - N_GROUPS - N_EXPERTS), F32)], axis=1)
        b_r = jnp.concatenate([moe_b_rg[layer], moe_b_re[layer],
                               jnp.zeros((128 - N_GROUPS - N_EXPERTS,), F32)])[None, :]
        x = _moe(x, mod4, layer, w_r, b_r, tri, moe_w_gate, moe_w_up, moe_w_down,
                 ln_g[layer, 1][None, :], ln_b[layer, 1][None, :])

    return (x[:NCTX].reshape(BATCH, SEQ, D_MODEL),
            x[NCTX:].reshape(DEC_BATCH, DEC_SEQ, D_MODEL),
            jnp.stack(new_gla, axis=1), jnp.stack(new_dk, axis=1), jnp.stack(new_dv, axis=1),
            jnp.stack(new_ckv, axis=1), jnp.stack(new_kr, axis=1))
```
